```python
import jax, jax.numpy as jnp
from jax import lax
import numpy as np

D_MODEL = 1024
BATCH = 4
SEQ = 4096
DEPTH = 1

HEAD_DIM = 64
N_ATTN_HEADS = D_MODEL // (2 * HEAD_DIM)
N_GMLP_GROUPS = D_MODEL // (2 * HEAD_DIM)
ATTN_WIDTH = N_ATTN_HEADS * HEAD_DIM
GMLP_WIDTH = N_GMLP_GROUPS * HEAD_DIM
MIX_WIDTH = ATTN_WIDTH + GMLP_WIDTH
IN_PROJ_WIDTH = 3 * ATTN_WIDTH + 2 * GMLP_WIDTH
MOBA_BLOCK = 256
MOBA_TOPK = 3
Q_CHUNK = 128
GMLP_CHUNK = 128
ROPE_THETA = 10000.0
FFN_HIDDEN = -(-(8 * D_MODEL) // (3 * 256)) * 256
DEEPNORM_ALPHA = (2 * DEPTH) ** 0.25
DEEPNORM_BETA = (8 * DEPTH) ** -0.25
LN_EPS = 1e-5
RMS_EPS = 1e-6
NEG_INF = -1e30

kernel_name = "hymba_moba_gmlp_deepnorm_block"


def layer_norm(x, g, b):
    xf = x.astype(jnp.float32)
    mu = jnp.mean(xf, axis=-1, keepdims=True)
    var = jnp.mean(jnp.square(xf - mu), axis=-1, keepdims=True)
    return ((xf - mu) * lax.rsqrt(var + LN_EPS) * g.astype(jnp.float32) + b.astype(jnp.float32)).astype(x.dtype)


def rms_norm(x, g):
    xf = x.astype(jnp.float32)
    ms = jnp.mean(jnp.square(xf), axis=-1, keepdims=True)
    return (xf * lax.rsqrt(ms + RMS_EPS) * g.astype(jnp.float32)).astype(x.dtype)


def rope(t, pos):
    inv_freq = ROPE_THETA ** (-jnp.arange(0, HEAD_DIM, 2, dtype=jnp.float32) / HEAD_DIM)
    ang = pos.astype(jnp.float32)[:, None] * inv_freq[None, :]
    ang = jnp.concatenate([ang, ang], axis=-1)
    cos = jnp.cos(ang).astype(t.dtype)
    sin = jnp.sin(ang).astype(t.dtype)
    t1, t2 = t[..., : HEAD_DIM // 2], t[..., HEAD_DIM // 2:]
    rot = jnp.concatenate([-t2, t1], axis=-1)
    return t * cos + rot * sin


def moba_attention(q, k, v):
    B, H, S, hd = q.shape
    nb = -(-S // MOBA_BLOCK)
    pad = nb * MOBA_BLOCK - S
    kb = jnp.pad(k, ((0, 0), (0, 0), (0, pad), (0, 0))).reshape(B, H, nb, MOBA_BLOCK, hd)
    vb = jnp.pad(v, ((0, 0), (0, 0), (0, pad), (0, 0))).reshape(B, H, nb, MOBA_BLOCK, hd)
    pos = jnp.arange(S, dtype=jnp.int32)
    own = pos // MOBA_BLOCK
    own_idx = jnp.broadcast_to(own[None, None, :, None], (B, H, S, 1)).astype(jnp.int32)
    n_sel = min(MOBA_TOPK, nb - 1)
    if n_sel > 0:
        k_mean = jnp.mean(kb.astype(jnp.float32), axis=3)
        gate = jnp.einsum('bhsd,bhnd->bhsn', q.astype(jnp.float32), k_mean)
        past = jnp.arange(nb, dtype=jnp.int32)[None, :] < own[:, None]
        gate = jnp.where(past[None, None], gate, NEG_INF)
        _, sel = lax.top_k(gate, n_sel)
        idx = jnp.concatenate([sel.astype(jnp.int32), own_idx], axis=-1)
    else:
        idx = own_idx
    nsb = idx.shape[-1]
    n_chunks = S // Q_CHUNK
    q_c = q.reshape(B, H, n_chunks, Q_CHUNK, hd).transpose(2, 0, 1, 3, 4)
    idx_c = idx.reshape(B, H, n_chunks, Q_CHUNK, nsb).transpose(2, 0, 1, 3, 4)
    pos_c = pos.reshape(n_chunks, Q_CHUNK)
    b_ix = jnp.arange(B)[:, None, None, None]
    h_ix = jnp.arange(H)[None, :, None, None]
    key_off = jnp.arange(MOBA_BLOCK, dtype=jnp.int32)
    slot_own = jnp.arange(nsb) == nsb - 1
    scale = HEAD_DIM ** -0.5

    def attend(args):
        qc, ic, pc = args
        kg = kb[b_ix, h_ix, ic]
        vg = vb[b_ix, h_ix, ic]
        s = jnp.einsum('bhqd,bhqnkd->bhqnk', qc, kg).astype(jnp.float32) * scale
        own_c = pc // MOBA_BLOCK
        valid_blk = slot_own | (ic < own_c[:, None])
        key_pos = ic[..., None] * MOBA_BLOCK + key_off
        mask = valid_blk[..., None] & (key_pos <= pc[:, None, None])
        s = jnp.where(mask, s, NEG_INF)
        p = jax.nn.softmax(s.reshape(B, H, Q_CHUNK, nsb * MOBA_BLOCK), axis=-1)
        p = p.reshape(B, H, Q_CHUNK, nsb, MOBA_BLOCK).astype(vg.dtype)
        return jnp.einsum('bhqnk,bhqnkd->bhqd', p, vg)

    out = lax.map(attend, (q_c, idx_c, pos_c))
    return out.transpose(1, 2, 0, 3, 4).reshape(B, H, S, hd)


def spatial_gating(u, z, ln_g, ln_b, w_s, b_s):
    B, S, G, dg = z.shape
    z = layer_norm(z, ln_g, ln_b)
    nc = S // GMLP_CHUNK
    zc = z.reshape(B, nc, GMLP_CHUNK, G, dg)
    tril = jnp.tril(jnp.ones((GMLP_CHUNK, GMLP_CHUNK), dtype=bool))
    w = jnp.where(tril[None], w_s, jnp.zeros_like(w_s))
    mixed = jnp.einsum('gij,bcjgd->bcigd', w, zc) + b_s.T[None, None, :, :, None]
    return u * mixed.reshape(B, S, G, dg)


def hybrid_mixer(x, w_in, attn_out_g, gmlp_out_g, gmlp_ln_g, gmlp_ln_b, w_spatial, b_spatial, w_out):
    B, S, _ = x.shape
    h = x @ w_in
    A, G = ATTN_WIDTH, GMLP_WIDTH
    q, k, v, u, z = jnp.split(h, [A, 2 * A, 3 * A, 3 * A + G], axis=-1)

    def heads(t):
        return t.reshape(B, S, N_ATTN_HEADS, HEAD_DIM).transpose(0, 2, 1, 3)

    pos = jnp.arange(S, dtype=jnp.int32)
    qh = rope(heads(q), pos)
    kh = rope(heads(k), pos)
    attn = moba_attention(qh, kh, heads(v)).transpose(0, 2, 1, 3).reshape(B, S, A)

    u = jax.nn.gelu(u, approximate=False).reshape(B, S, N_GMLP_GROUPS, HEAD_DIM)
    z = jax.nn.gelu(z, approximate=False).reshape(B, S, N_GMLP_GROUPS, HEAD_DIM)
    sg = spatial_gating(u, z, gmlp_ln_g, gmlp_ln_b, w_spatial, b_spatial).reshape(B, S, G)

    merged = jnp.concatenate([rms_norm(attn, attn_out_g), rms_norm(sg, gmlp_out_g)], axis=-1)
    return merged @ w_out


def swiglu_ffn(x, w_gate, w_up, w_down):
    return (jax.nn.silu(x @ w_gate) * (x @ w_up)) @ w_down


def setup_inputs(seed: int = 0) -> dict:
    key = jax.random.key(seed)
    ks = jax.random.split(key, 16)
    nrm = jax.random.normal
    f32 = jnp.float32
    x = nrm(ks[0], (BATCH, SEQ, D_MODEL), f32)
    w_in = nrm(ks[1], (DEPTH, D_MODEL, IN_PROJ_WIDTH), f32) * D_MODEL ** -0.5
    attn_out_g = 1.0 + 0.1 * nrm(ks[2], (DEPTH, ATTN_WIDTH), f32)
    gmlp_out_g = 1.0 + 0.1 * nrm(ks[3], (DEPTH, GMLP_WIDTH), f32)
    gmlp_ln_g = 1.0 + 0.1 * nrm(ks[4], (DEPTH, N_GMLP_GROUPS, HEAD_DIM), f32)
    gmlp_ln_b = 0.02 * nrm(ks[5], (DEPTH, N_GMLP_GROUPS, HEAD_DIM), f32)
    w_spatial = nrm(ks[6], (DEPTH, N_GMLP_GROUPS, GMLP_CHUNK, GMLP_CHUNK), f32) * GMLP_CHUNK ** -0.5
    b_spatial = 1.0 + 0.1 * nrm(ks[7], (DEPTH, N_GMLP_GROUPS, GMLP_CHUNK), f32)
    w_out = nrm(ks[8], (DEPTH, MIX_WIDTH, D_MODEL), f32) * (MIX_WIDTH ** -0.5 * DEEPNORM_BETA)
    ln1_g = 1.0 + 0.1 * nrm(ks[9], (DEPTH, D_MODEL), f32)
    ln1_b = 0.02 * nrm(ks[10], (DEPTH, D_MODEL), f32)
    w_gate = nrm(ks[11], (DEPTH, D_MODEL, FFN_HIDDEN), f32) * D_MODEL ** -0.5
    w_up = nrm(ks[12], (DEPTH, D_MODEL, FFN_HIDDEN), f32) * D_MODEL ** -0.5
    w_down = nrm(ks[13], (DEPTH, FFN_HIDDEN, D_MODEL), f32) * (FFN_HIDDEN ** -0.5 * DEEPNORM_BETA)
    ln2_g = 1.0 + 0.1 * nrm(ks[14], (DEPTH, D_MODEL), f32)
    ln2_b = 0.02 * nrm(ks[15], (DEPTH, D_MODEL), f32)
    return {"x": x, "w_in": w_in, "attn_out_g": attn_out_g, "gmlp_out_g": gmlp_out_g,
            "gmlp_ln_g": gmlp_ln_g, "gmlp_ln_b": gmlp_ln_b, "w_spatial": w_spatial,
            "b_spatial": b_spatial, "w_out": w_out, "ln1_g": ln1_g, "ln1_b": ln1_b,
            "w_gate": w_gate, "w_up": w_up, "w_down": w_down, "ln2_g": ln2_g, "ln2_b": ln2_b}


def reference(x, w_in, attn_out_g, gmlp_out_g, gmlp_ln_g, gmlp_ln_b, w_spatial, b_spatial,
              w_out, ln1_g, ln1_b, w_gate, w_up, w_down, ln2_g, ln2_b):
    for l in range(DEPTH):
        mix = hybrid_mixer(x, w_in[l], attn_out_g[l], gmlp_out_g[l], gmlp_ln_g[l], gmlp_ln_b[l],
                           w_spatial[l], b_spatial[l], w_out[l])
        x = layer_norm(DEEPNORM_ALPHA * x + mix, ln1_g[l], ln1_b[l])
        ffn = swiglu_ffn(x, w_gate[l], w_up[l], w_down[l])
        x = layer_norm(DEEPNORM_ALPHA * x + ffn, ln2_g[l], ln2_b[l])
    return x
```

```python
import functools

import jax
import jax.numpy as jnp
from jax import lax
from jax.experimental import pallas as pl
from jax.experimental.pallas import tpu as pltpu

HEAD_DIM = 64
MOBA_BLOCK = 256
MOBA_TOPK = 3
GMLP_CHUNK = 128
ROPE_THETA = 10000.0
LN_EPS = 1e-5
RMS_EPS = 1e-6
NEG_INF = -1e30
LANES = 128

F32 = jnp.float32
BF16 = jnp.bfloat16
_NT = (((1,), (1,)), ((), ()))

VMEM_LIMIT = 48 * 1024 * 1024


def _erf_gelu(t):
    return 0.5 * t * (1.0 + lax.erf(t * 0.7071067811865476))


def _in_proj_kernel(x_ref, w_ref, cos_ref, sina_ref, sinb_ref, lng_ref, lnb_ref,
                    q_ref, k_ref, v_ref, u_ref, z_ref, *, width):
    x = x_ref[...].astype(BF16)
    cos = cos_ref[...]
    sina = sina_ref[...]
    sinb = sinb_ref[...]
    lane = lax.broadcasted_iota(jnp.int32, (1, LANES), 1)
    lo = lane < HEAD_DIM

    def proj(seg):
        return jnp.dot(x, w_ref[:, seg * width:(seg + 1) * width], preferred_element_type=F32)

    def rope(t):
        return t * cos + pltpu.roll(t, 96, 1) * sina + pltpu.roll(t, 32, 1) * sinb

    hq = proj(0)
    hk = proj(1)
    for c in range(width // LANES):
        sl = slice(c * LANES, (c + 1) * LANES)
        q_ref[:, sl] = rope(hq[:, sl] * (HEAD_DIM ** -0.5)).astype(BF16)
        k_ref[:, sl] = rope(hk[:, sl]).astype(BF16)
    v_ref[...] = proj(2).astype(BF16)
    u_ref[...] = _erf_gelu(proj(3)).astype(BF16)
    hz = _erf_gelu(proj(4))
    for c in range(width // LANES):
        sl = slice(c * LANES, (c + 1) * LANES)
        zs = hz[:, sl]
        s_lo = jnp.sum(jnp.where(lo, zs, 0.0), axis=1, keepdims=True)
        s_hi = jnp.sum(jnp.where(lo, 0.0, zs), axis=1, keepdims=True)
        d = zs - jnp.where(lo, s_lo, s_hi) * (1.0 / HEAD_DIM)
        dd = d * d
        v_lo = jnp.sum(jnp.where(lo, dd, 0.0), axis=1, keepdims=True)
        v_hi = jnp.sum(jnp.where(lo, 0.0, dd), axis=1, keepdims=True)
        var = jnp.where(lo, v_lo, v_hi) * (1.0 / HEAD_DIM)
        z_ref[:, sl] = (d * lax.rsqrt(var + LN_EPS) * lng_ref[:, sl] + lnb_ref[:, sl]).astype(BF16)


def _in_proj(x2, w_in, cos, sina, sinb, lng, lnb, *, seq, width, tm):
    tokens, d_model = x2.shape
    n_pos_tiles = seq // tm
    out = jax.ShapeDtypeStruct((tokens, width), BF16)
    tile = pl.BlockSpec((tm, width), lambda t: (t, 0))
    table = pl.BlockSpec((tm, LANES), lambda t: (t % n_pos_tiles, 0))
    row = pl.BlockSpec((1, width), lambda t: (0, 0))
    return pl.pallas_call(
        functools.partial(_in_proj_kernel, width=width),
        grid=(tokens // tm,),
        in_specs=[pl.BlockSpec((tm, d_model), lambda t: (t, 0)),
                  pl.BlockSpec(w_in.shape, lambda t: (0, 0)),
                  table, table, table, row, row],
        out_specs=[tile] * 5,
        out_shape=[out] * 5,
        compiler_params=pltpu.CompilerParams(dimension_semantics=("arbitrary",),
                                             vmem_limit_bytes=VMEM_LIMIT),
        name="in_proj",
    )(x2, w_in, cos, sina, sinb, lng, lnb)


def _attn_kernel(q_ref, k_ref, v_ref, o_ref, kh_ref, vt_ref, kmh_ref, kml_ref, bias_ref, *, nb):
    blk = MOBA_BLOCK
    lane = lax.broadcasted_iota(jnp.int32, (1, LANES), 1)
    k = k_ref[...]
    for h in range(2):
        in_head = (lane >= HEAD_DIM * h) & (lane < HEAD_DIM * (h + 1))
        kh = jnp.where(in_head, k, jnp.zeros_like(k))
        kh_ref[h] = kh
        km = jnp.sum(kh.astype(F32).reshape(nb, blk, LANES), axis=1) * (1.0 / blk)
        hi = km.astype(BF16)
        kmh_ref[h] = hi
        kml_ref[h] = (km - hi.astype(F32)).astype(BF16)
    for j in range(nb):
        vt_ref[j] = v_ref[j * blk:(j + 1) * blk, :].astype(F32).T.astype(BF16)

    row = lax.broadcasted_iota(jnp.int32, (nb, blk), 0)
    rowf = row.astype(F32)
    kk = lax.broadcasted_iota(jnp.int32, (blk, blk), 0)
    qq = lax.broadcasted_iota(jnp.int32, (blk, blk), 1)

    def tile_update(h, kj, q_i, bias, vt, m, l, acc):
        s = lax.dot_general(kj, q_i, _NT, preferred_element_type=F32) + bias
        m_new = jnp.maximum(m, jnp.max(s, axis=0, keepdims=True))
        alpha = jnp.exp(m - m_new)
        p = jnp.exp(s - m_new)
        l = alpha * l + jnp.sum(p, axis=0, keepdims=True)
        acc = alpha * acc + jnp.dot(vt, p.astype(BF16), preferred_element_type=F32)
        return m_new, l, acc

    def q_block(i, _):
        q0 = pl.multiple_of(i * blk, blk)
        q_i = q_ref[pl.ds(q0, blk), :]
        for h in range(2):
            g = (lax.dot_general(kmh_ref[h], q_i, _NT, preferred_element_type=F32)
                 + lax.dot_general(kml_ref[h], q_i, _NT, preferred_element_type=F32))
            past = row < i
            g = jnp.where(past, g, NEG_INF)
            bias = jnp.full((nb, blk), NEG_INF, F32)
            for _ in range(MOBA_TOPK):
                top = jnp.max(g, axis=0, keepdims=True)
                first = jnp.min(jnp.where(g == top, rowf, float(nb)), axis=0, keepdims=True)
                pick = rowf == first
                bias = jnp.where(pick & past, 0.0, bias)
                g = jnp.where(pick, -jnp.inf, g)
            bias_ref[h] = bias

        causal = jnp.where(kk <= qq, 0.0, NEG_INF)
        state = []
        for h in range(2):
            m0 = jnp.full((1, blk), NEG_INF, F32)
            l0 = jnp.zeros((1, blk), F32)
            a0 = jnp.zeros((HEAD_DIM, blk), F32)
            state.append(tile_update(h, kh_ref[h, pl.ds(q0, blk), :], q_i, causal,
                                     vt_ref[i, HEAD_DIM * h:HEAD_DIM * (h + 1), :], m0, l0, a0))

        def kv_block(j, carry):
            k0 = pl.multiple_of(j * blk, blk)
            out = []
            for h in range(2):
                m, l, acc = carry[h]
                out.append(tile_update(h, kh_ref[h, pl.ds(k0, blk), :], q_i,
                                       bias_ref[h, pl.ds(j, 1), :],
                                       vt_ref[j, HEAD_DIM * h:HEAD_DIM * (h + 1), :], m, l, acc))
            return tuple(out)

        state = lax.fori_loop(0, i, kv_block, tuple(state))
        o_t = jnp.concatenate([state[0][2] / state[0][1], state[1][2] / state[1][1]], axis=0)
        o_ref[pl.ds(q0, blk), :] = o_t.T
        return 0

    lax.fori_loop(0, nb, q_block, 0)


def _attention(q, k, v, *, batch, seq):
    width = q.shape[-1]
    nb = seq // MOBA_BLOCK
    q3, k3, v3 = (t.reshape(batch, seq, width) for t in (q, k, v))
    spec = pl.BlockSpec((None, seq, LANES), lambda b, p: (b, 0, p))
    out = pl.pallas_call(
        functools.partial(_attn_kernel, nb=nb),
        grid=(batch, width // LANES),
        in_specs=[spec, spec, spec],
        out_specs=spec,
        out_shape=jax.ShapeDtypeStruct((batch, seq, width), F32),
        scratch_shapes=[pltpu.VMEM((2, seq, LANES), BF16),
                        pltpu.VMEM((nb, LANES, MOBA_BLOCK), BF16),
                        pltpu.VMEM((2, nb, LANES), BF16),
                        pltpu.VMEM((2, nb, LANES), BF16),
                        pltpu.VMEM((2, nb, MOBA_BLOCK), F32)],
        compiler_params=pltpu.CompilerParams(dimension_semantics=("arbitrary", "arbitrary"),
                                             vmem_limit_bytes=VMEM_LIMIT),
        name="moba_attention",
    )(q3, k3, v3)
    return out.reshape(batch * seq, width)


def _gmlp_kernel(u_ref, z_ref, w_ref, b_ref, o_ref, *, n_chunks, n_groups):
    c = GMLP_CHUNK
    ii = lax.broadcasted_iota(jnp.int32, (c, c), 0)
    jj = lax.broadcasted_iota(jnp.int32, (c, c), 1)
    lane = lax.broadcasted_iota(jnp.int32, (1, LANES), 1)
    lo = lane < HEAD_DIM
    w = [jnp.where(jj <= ii, w_ref[g], 0.0).astype(BF16) for g in range(n_groups)]
    for ch in range(n_chunks):
        rows = slice(ch * c, (ch + 1) * c)
        for pair in range(n_groups // 2):
            sl = slice(pair * LANES, (pair + 1) * LANES)
            z = z_ref[rows, sl]
            z_lo = jnp.where(lo, z, jnp.zeros_like(z))
            z_hi = jnp.where(lo, jnp.zeros_like(z), z)
            mixed = (jnp.dot(w[2 * pair], z_lo, preferred_element_type=F32)
                     + jnp.dot(w[2 * pair + 1], z_hi, preferred_element_type=F32)
                     + b_ref[:, sl])
            o_ref[rows, sl] = u_ref[rows, sl].astype(F32) * mixed


def _gmlp(u, z, w_spatial, bias_full, *, tm):
    tokens, width = u.shape
    n_groups = w_spatial.shape[0]
    tile = pl.BlockSpec((tm, width), lambda t: (t, 0))
    return pl.pallas_call(
        functools.partial(_gmlp_kernel, n_chunks=tm // GMLP_CHUNK, n_groups=n_groups),
        grid=(tokens // tm,),
        in_specs=[tile, tile,
                  pl.BlockSpec(w_spatial.shape, lambda t: (0, 0, 0)),
                  pl.BlockSpec(bias_full.shape, lambda t: (0, 0))],
        out_specs=tile,
        out_shape=jax.ShapeDtypeStruct((tokens, width), F32),
        compiler_params=pltpu.CompilerParams(dimension_semantics=("arbitrary",),
                                             vmem_limit_bytes=VMEM_LIMIT),
        name="gmlp_gating",
    )(u, z, w_spatial, bias_full)


def _layer_norm(y, g, b):
    mu = jnp.mean(y, axis=-1, keepdims=True)
    d = y - mu
    var = jnp.mean(d * d, axis=-1, keepdims=True)
    return d * lax.rsqrt(var + LN_EPS) * g + b


def _rms_norm(t, g):
    ms = jnp.mean(t * t, axis=-1, keepdims=True)
    return t * lax.rsqrt(ms + RMS_EPS) * g


def _out_proj_kernel(attn_ref, sg_ref, x_ref, ag_ref, gg_ref, w_ref, lg_ref, lb_ref, o_ref,
                     *, alpha, width):
    an = _rms_norm(attn_ref[...], ag_ref[...]).astype(BF16)
    sn = _rms_norm(sg_ref[...], gg_ref[...]).astype(BF16)
    mix = (jnp.dot(an, w_ref[:width, :], preferred_element_type=F32)
           + jnp.dot(sn, w_ref[width:, :], preferred_element_type=F32))
    o_ref[...] = _layer_norm(alpha * x_ref[...] + mix, lg_ref[...], lb_ref[...])


def _out_proj(attn, sg, x2, ag, gg, w_out, lg, lb, *, alpha, tm):
    tokens, width = attn.shape
    d_model = x2.shape[1]
    half = pl.BlockSpec((tm, width), lambda t: (t, 0))
    full = pl.BlockSpec((tm, d_model), lambda t: (t, 0))
    hrow = pl.BlockSpec((1, width), lambda t: (0, 0))
    frow = pl.BlockSpec((1, d_model), lambda t: (0, 0))
    return pl.pallas_call(
        functools.partial(_out_proj_kernel, alpha=alpha, width=width),
        grid=(tokens // tm,),
        in_specs=[half, half, full, hrow, hrow,
                  pl.BlockSpec(w_out.shape, lambda t: (0, 0)), frow, frow],
        out_specs=full,
        out_shape=jax.ShapeDtypeStruct((tokens, d_model), F32),
        compiler_params=pltpu.CompilerParams(dimension_semantics=("arbitrary",),
                                             vmem_limit_bytes=VMEM_LIMIT),
        name="out_proj_ln",
    )(attn, sg, x2, ag, gg, w_out, lg, lb)


def _ffn_kernel(x_ref, wg_ref, wu_ref, wd_ref, lg_ref, lb_ref, o_ref, acc_ref, *, alpha, hc):
    x = x_ref[...]
    xb = x.astype(BF16)
    hidden = wg_ref.shape[1]
    for c in range(hidden // hc):
        sl = slice(c * hc, (c + 1) * hc)
        g = jnp.dot(xb, wg_ref[:, sl], preferred_element_type=F32)
        u = jnp.dot(xb, wu_ref[:, sl], preferred_element_type=F32)
        h = (g / (1.0 + jnp.exp(-g)) * u).astype(BF16)
        part = jnp.dot(h, wd_ref[sl, :], preferred_element_type=F32)
        if c == 0:
            acc_ref[...] = part
        else:
            acc_ref[...] += part
    o_ref[...] = _layer_norm(alpha * x + acc_ref[...], lg_ref[...], lb_ref[...])


def _ffn(x1, w_gate, w_up, w_down, lg, lb, *, alpha, tm, hc):
    tokens, d_model = x1.shape
    full = pl.BlockSpec((tm, d_model), lambda t: (t, 0))
    frow = pl.BlockSpec((1, d_model), lambda t: (0, 0))
    const = lambda a: pl.BlockSpec(a.shape, lambda t: (0, 0), pipeline_mode=pl.Buffered(1))
    return pl.pallas_call(
        functools.partial(_ffn_kernel, alpha=alpha, hc=hc),
        grid=(tokens // tm,),
        in_specs=[full, const(w_gate), const(w_up), const(w_down), frow, frow],
        out_specs=full,
        out_shape=jax.ShapeDtypeStruct((tokens, d_model), F32),
        scratch_shapes=[pltpu.VMEM((tm, d_model), F32)],
        compiler_params=pltpu.CompilerParams(dimension_semantics=("arbitrary",),
                                             vmem_limit_bytes=VMEM_LIMIT),
        name="swiglu_ffn_ln",
    )(x1, w_gate, w_up, w_down, lg, lb)


def _rope_tables(seq):
    inv_freq = ROPE_THETA ** (-jnp.arange(0, HEAD_DIM, 2, dtype=F32) / HEAD_DIM)
    ang = jnp.arange(seq, dtype=jnp.int32).astype(F32)[:, None] * inv_freq[None, :]
    ang = jnp.concatenate([ang, ang, ang, ang], axis=-1)
    cos = jnp.cos(ang)
    sin = jnp.sin(ang)
    first_half = (jnp.arange(LANES) % HEAD_DIM) < HEAD_DIM // 2
    sina = jnp.where(first_half[None, :], -sin, 0.0)
    sinb = jnp.where(first_half[None, :], 0.0, sin)
    return cos, sina, sinb


def kernel(x, w_in, attn_out_g, gmlp_out_g, gmlp_ln_g, gmlp_ln_b, w_spatial, b_spatial, w_out,
           ln1_g, ln1_b, w_gate, w_up, w_down, ln2_g, ln2_b):
    batch, seq, d_model = x.shape
    depth = w_in.shape[0]
    width = attn_out_g.shape[-1]
    alpha = (2 * depth) ** 0.25
    assert w_in.shape[-1] == 5 * width and seq % MOBA_BLOCK == 0 and width % LANES == 0
    assert min(MOBA_TOPK, seq // MOBA_BLOCK - 1) == MOBA_TOPK
    cos, sina, sinb = _rope_tables(seq)
    x2 = x.reshape(batch * seq, d_model)
    for l in range(depth):
        q, k, v, u, z = _in_proj(x2, w_in[l].astype(BF16), cos, sina, sinb,
                                 gmlp_ln_g[l].reshape(1, width), gmlp_ln_b[l].reshape(1, width),
                                 seq=seq, width=width, tm=512)
        attn = _attention(q, k, v, batch=batch, seq=seq)
        bias_full = jnp.repeat(b_spatial[l].T, HEAD_DIM, axis=1)
        sg = _gmlp(u, z, w_spatial[l], bias_full, tm=512)
        x1 = _out_proj(attn, sg, x2, attn_out_g[l].reshape(1, width), gmlp_out_g[l].reshape(1, width),
                       w_out[l].astype(BF16), ln1_g[l].reshape(1, d_model), ln1_b[l].reshape(1, d_model),
                       alpha=alpha, tm=512)
        x2 = _ffn(x1, w_gate[l].astype(BF16), w_up[l].astype(BF16), w_down[l].astype(BF16),
                  ln2_g[l].reshape(1, d_model), ln2_b[l].reshape(1, d_model),
                  alpha=alpha, tm=512, hc=256)
    return x2.reshape(batch, seq, d_model)
```

```python
import functools

import jax
import jax.numpy as jnp
from jax import lax
from jax.experimental import pallas as pl
from jax.experimental.pallas import tpu as pltpu

HEAD_DIM = 64
MOBA_BLOCK = 256
MOBA_TOPK = 3
GMLP_CHUNK = 128
ROPE_THETA = 10000.0
LN_EPS = 1e-5
RMS_EPS = 1e-6
NEG_INF = -1e30
LANES = 128
VT_ROWS = HEAD_DIM + 16

F32 = jnp.float32
BF16 = jnp.bfloat16
_NT = (((1,), (1,)), ((), ()))

VMEM_LIMIT = 48 * 1024 * 1024


def _erf_gelu(t):
    return 0.5 * t * (1.0 + lax.erf(t * 0.7071067811865476))


def _in_proj_kernel(x_ref, w_ref, cos_ref, sina_ref, sinb_ref, lng_ref, lnb_ref,
                    q_ref, k_ref, v_ref, u_ref, z_ref, *, width):
    x = x_ref[...].astype(BF16)
    cos = cos_ref[...]
    sina = sina_ref[...]
    sinb = sinb_ref[...]
    lane = lax.broadcasted_iota(jnp.int32, (1, LANES), 1)
    lo = lane < HEAD_DIM

    def proj(seg):
        return jnp.dot(x, w_ref[:, seg * width:(seg + 1) * width], preferred_element_type=F32)

    def rope(t):
        return t * cos + pltpu.roll(t, 96, 1) * sina + pltpu.roll(t, 32, 1) * sinb

    hq = proj(0)
    hk = proj(1)
    for c in range(width // LANES):
        sl = slice(c * LANES, (c + 1) * LANES)
        q_ref[:, sl] = rope(hq[:, sl] * (HEAD_DIM ** -0.5)).astype(BF16)
        k_ref[:, sl] = rope(hk[:, sl]).astype(BF16)
    v_ref[...] = proj(2).astype(BF16)
    u_ref[...] = _erf_gelu(proj(3)).astype(BF16)
    hz = _erf_gelu(proj(4))
    for c in range(width // LANES):
        sl = slice(c * LANES, (c + 1) * LANES)
        zs = hz[:, sl]
        s_lo = jnp.sum(jnp.where(lo, zs, 0.0), axis=1, keepdims=True)
        s_hi = jnp.sum(jnp.where(lo, 0.0, zs), axis=1, keepdims=True)
        d = zs - jnp.where(lo, s_lo, s_hi) * (1.0 / HEAD_DIM)
        dd = d * d
        v_lo = jnp.sum(jnp.where(lo, dd, 0.0), axis=1, keepdims=True)
        v_hi = jnp.sum(jnp.where(lo, 0.0, dd), axis=1, keepdims=True)
        var = jnp.where(lo, v_lo, v_hi) * (1.0 / HEAD_DIM)
        z_ref[:, sl] = (d * lax.rsqrt(var + LN_EPS) * lng_ref[:, sl] + lnb_ref[:, sl]).astype(BF16)


def _in_proj(x2, w_in, cos, sina, sinb, lng, lnb, *, seq, width, tm):
    tokens, d_model = x2.shape
    n_pos_tiles = seq // tm
    out = jax.ShapeDtypeStruct((tokens, width), BF16)
    tile = pl.BlockSpec((tm, width), lambda t: (t, 0))
    table = pl.BlockSpec((tm, LANES), lambda t: (t % n_pos_tiles, 0))
    row = pl.BlockSpec((1, width), lambda t: (0, 0))
    return pl.pallas_call(
        functools.partial(_in_proj_kernel, width=width),
        grid=(tokens // tm,),
        in_specs=[pl.BlockSpec((tm, d_model), lambda t: (t, 0)),
                  pl.BlockSpec(w_in.shape, lambda t: (0, 0)),
                  table, table, table, row, row],
        out_specs=[tile] * 5,
        out_shape=[out] * 5,
        compiler_params=pltpu.CompilerParams(dimension_semantics=("arbitrary",),
                                             vmem_limit_bytes=VMEM_LIMIT),
        name="in_proj",
    )(x2, w_in, cos, sina, sinb, lng, lnb)


def _attn_kernel(q_ref, k_ref, v_ref, o_ref, ka_ref, vt_ref, kmh_ref, kml_ref, qa_ref, s_ref,
                 m_ref, acc_ref, *, nb):
    blk = MOBA_BLOCK
    sub = blk // 8
    lane = lax.broadcasted_iota(jnp.int32, (1, LANES), 1)
    head_lanes = [(lane >= HEAD_DIM * h) & (lane < HEAD_DIM * (h + 1)) for h in range(2)]
    bias_off = [HEAD_DIM, 0]

    kf = k_ref[...].astype(F32)
    for h in range(2):
        km = jnp.sum(jnp.where(head_lanes[h], kf, 0.0).reshape(nb, blk, LANES), axis=1) * (1.0 / blk)
        hi = km.astype(BF16)
        kmh_ref[h] = hi
        kml_ref[h] = (km - hi.astype(F32)).astype(BF16)
    ones_row = jnp.where(lax.broadcasted_iota(jnp.int32, (VT_ROWS - HEAD_DIM, blk), 0) == 0, 1.0, 0.0)
    for j in range(nb):
        rows = slice(j * blk, (j + 1) * blk)
        kj = k_ref[rows, :]
        v_t = v_ref[rows, :].astype(F32).T
        for h in range(2):
            onehot = jnp.where(lane == bias_off[h] + j, 1.0, 0.0).astype(BF16)
            ka_ref[h, rows, :] = jnp.where(head_lanes[h], kj, onehot)
            vt_ref[j, h] = jnp.concatenate(
                [v_t[HEAD_DIM * h:HEAD_DIM * (h + 1)], ones_row], axis=0).astype(BF16)

    row = lax.broadcasted_iota(jnp.int32, (nb, blk), 0)
    rowf = row.astype(F32)
    kk = lax.broadcasted_iota(jnp.int32, (blk, blk), 0)
    qq = lax.broadcasted_iota(jnp.int32, (blk, blk), 1)

    def prepare_row(rs, r):
        q_r = q_ref[pl.ds(pl.multiple_of(r * blk, blk), blk), :]
        for h in range(2):
            g = (lax.dot_general(kmh_ref[h], q_r, _NT, preferred_element_type=F32)
                 + lax.dot_general(kml_ref[h], q_r, _NT, preferred_element_type=F32))
            past = row < r
            g = jnp.where(past, g, NEG_INF)
            bias = jnp.where(row == r, 0.0, NEG_INF)
            for _ in range(MOBA_TOPK):
                top = jnp.max(g, axis=0, keepdims=True)
                first = jnp.min(jnp.where(g == top, rowf, float(nb)), axis=0, keepdims=True)
                pick = rowf == first
                bias = jnp.where(pick & past, 0.0, bias)
                g = jnp.where(pick, -jnp.inf, g)
            pieces = [bias, jnp.zeros((LANES - bias_off[h] - nb, blk), F32)]
            if bias_off[h]:
                pieces.insert(0, jnp.zeros((bias_off[h], blk), F32))
            bias_t = jnp.concatenate(pieces, axis=0).T
            qa_ref[rs, h] = jnp.where(head_lanes[h], q_r.astype(F32), bias_t).astype(BF16)
            m_ref[rs, h] = jnp.full((8, blk), NEG_INF, F32)
            acc_ref[rs, h] = jnp.zeros((VT_ROWS, blk), F32)

    def score_tile(t, rs, j, causal):
        k0 = pl.multiple_of(j * blk, blk)
        for h in range(2):
            s = lax.dot_general(ka_ref[h, pl.ds(k0, blk), :], qa_ref[rs, h], _NT,
                                preferred_element_type=F32)
            if causal is not None:
                s = s + causal
            s_ref[t, h] = s
            m_ref[rs, h] = jnp.maximum(m_ref[rs, h], jnp.max(s.reshape(sub, 8, blk), axis=0))

    def value_tile(t, rs, j):
        for h in range(2):
            p = jnp.exp(s_ref[t, h].reshape(sub, 8, blk) - m_ref[rs, h]).reshape(blk, blk)
            acc_ref[rs, h] += jnp.dot(vt_ref[j, h], p.astype(BF16), preferred_element_type=F32)

    def past_tile(u, pi):
        first = u < pi
        return jnp.where(first, 0, 1), jnp.where(first, u, u - pi)

    def pair_body(pi, _):
        rows = (pi, nb - 1 - pi)
        for rs in range(2):
            prepare_row(rs, rows[rs])
        causal = jnp.where(kk <= qq, 0.0, NEG_INF)
        for rs in range(2):
            score_tile(rs, rs, rows[rs], causal)
        for u in range(nb - 1):
            rs, j = past_tile(u, pi)
            score_tile(2 + u, rs, j, None)
        for rs in range(2):
            for h in range(2):
                m_ref[rs, h] = jnp.broadcast_to(jnp.max(m_ref[rs, h], axis=0, keepdims=True), (8, blk))
        for rs in range(2):
            value_tile(rs, rs, rows[rs])
        for u in range(nb - 1):
            rs, j = past_tile(u, pi)
            value_tile(2 + u, rs, j)
        for rs in range(2):
            outs = []
            for h in range(2):
                a = acc_ref[rs, h]
                outs.append(a[:HEAD_DIM] / a[HEAD_DIM:HEAD_DIM + 1])
            q0 = pl.multiple_of(rows[rs] * blk, blk)
            o_ref[pl.ds(q0, blk), :] = jnp.concatenate(outs, axis=0).T
        return 0

    lax.fori_loop(0, nb // 2, pair_body, 0)


def _attention(q, k, v, *, batch, seq):
    width = q.shape[-1]
    nb = seq // MOBA_BLOCK
    q3, k3, v3 = (t.reshape(batch, seq, width) for t in (q, k, v))
    spec = pl.BlockSpec((None, seq, LANES), lambda b, p: (b, 0, p))
    out = pl.pallas_call(
        functools.partial(_attn_kernel, nb=nb),
        grid=(batch, width // LANES),
        in_specs=[spec, spec, spec],
        out_specs=spec,
        out_shape=jax.ShapeDtypeStruct((batch, seq, width), F32),
        scratch_shapes=[pltpu.VMEM((2, seq, LANES), BF16),
                        pltpu.VMEM((nb, 2, VT_ROWS, MOBA_BLOCK), BF16),
                        pltpu.VMEM((2, nb, LANES), BF16),
                        pltpu.VMEM((2, nb, LANES), BF16),
                        pltpu.VMEM((2, 2, MOBA_BLOCK, LANES), BF16),
                        pltpu.VMEM((nb + 1, 2, MOBA_BLOCK, MOBA_BLOCK), F32),
                        pltpu.VMEM((2, 2, 8, MOBA_BLOCK), F32),
                        pltpu.VMEM((2, 2, VT_ROWS, MOBA_BLOCK), F32)],
        compiler_params=pltpu.CompilerParams(dimension_semantics=("arbitrary", "arbitrary"),
                                             vmem_limit_bytes=VMEM_LIMIT),
        name="moba_attention",
    )(q3, k3, v3)
    return out.reshape(batch * seq, width)


def _gmlp_kernel(u_ref, z_ref, w_ref, b_ref, o_ref, *, n_chunks, n_groups):
    c = GMLP_CHUNK
    ii = lax.broadcasted_iota(jnp.int32, (c, c), 0)
    jj = lax.broadcasted_iota(jnp.int32, (c, c), 1)
    lane = lax.broadcasted_iota(jnp.int32, (1, LANES), 1)
    lo = lane < HEAD_DIM
    w = [jnp.where(jj <= ii, w_ref[g], 0.0).astype(BF16) for g in range(n_groups)]
    for ch in range(n_chunks):
        rows = slice(ch * c, (ch + 1) * c)
        for pair in range(n_groups // 2):
            sl = slice(pair * LANES, (pair + 1) * LANES)
            z = z_ref[rows, sl]
            z_lo = jnp.where(lo, z, jnp.zeros_like(z))
            z_hi = jnp.where(lo, jnp.zeros_like(z), z)
            mixed = (jnp.dot(w[2 * pair], z_lo, preferred_element_type=F32)
                     + jnp.dot(w[2 * pair + 1], z_hi, preferred_element_type=F32)
                     + b_ref[:, sl])
            o_ref[rows, sl] = u_ref[rows, sl].astype(F32) * mixed


def _gmlp(u, z, w_spatial, bias_full, *, tm):
    tokens, width = u.shape
    n_groups = w_spatial.shape[0]
    tile = pl.BlockSpec((tm, width), lambda t: (t, 0))
    return pl.pallas_call(
        functools.partial(_gmlp_kernel, n_chunks=tm // GMLP_CHUNK, n_groups=n_groups),
        grid=(tokens // tm,),
        in_specs=[tile, tile,
                  pl.BlockSpec(w_spatial.shape, lambda t: (0, 0, 0)),
                  pl.BlockSpec(bias_full.shape, lambda t: (0, 0))],
        out_specs=tile,
        out_shape=jax.ShapeDtypeStruct((tokens, width), F32),
        compiler_params=pltpu.CompilerParams(dimension_semantics=("arbitrary",),
                                             vmem_limit_bytes=VMEM_LIMIT),
        name="gmlp_gating",
    )(u, z, w_spatial, bias_full)


def _layer_norm(y, g, b):
    mu = jnp.mean(y, axis=-1, keepdims=True)
    d = y - mu
    var = jnp.mean(d * d, axis=-1, keepdims=True)
    return d * lax.rsqrt(var + LN_EPS) * g + b


def _rms_norm(t, g):
    ms = jnp.mean(t * t, axis=-1, keepdims=True)
    return t * lax.rsqrt(ms + RMS_EPS) * g


def _out_proj_kernel(attn_ref, sg_ref, x_ref, ag_ref, gg_ref, w_ref, lg_ref, lb_ref, o_ref,
                     *, alpha, width):
    an = _rms_norm(attn_ref[...], ag_ref[...]).astype(BF16)
    sn = _rms_norm(sg_ref[...], gg_ref[...]).astype(BF16)
    mix = (jnp.dot(an, w_ref[:width, :], preferred_element_type=F32)
           + jnp.dot(sn, w_ref[width:, :], preferred_element_type=F32))
    o_ref[...] = _layer_norm(alpha * x_ref[...] + mix, lg_ref[...], lb_ref[...])


def _out_proj(attn, sg, x2, ag, gg, w_out, lg, lb, *, alpha, tm):
    tokens, width = attn.shape
    d_model = x2.shape[1]
    half = pl.BlockSpec((tm, width), lambda t: (t, 0))
    full = pl.BlockSpec((tm, d_model), lambda t: (t, 0))
    hrow = pl.BlockSpec((1, width), lambda t: (0, 0))
    frow = pl.BlockSpec((1, d_model), lambda t: (0, 0))
    return pl.pallas_call(
        functools.partial(_out_proj_kernel, alpha=alpha, width=width),
        grid=(tokens // tm,),
        in_specs=[half, half, full, hrow, hrow,
                  pl.BlockSpec(w_out.shape, lambda t: (0, 0)), frow, frow],
        out_specs=full,
        out_shape=jax.ShapeDtypeStruct((tokens, d_model), F32),
        compiler_params=pltpu.CompilerParams(dimension_semantics=("arbitrary",),
                                             vmem_limit_bytes=VMEM_LIMIT),
        name="out_proj_ln",
    )(attn, sg, x2, ag, gg, w_out, lg, lb)


def _ffn_kernel(x_ref, wg_ref, wu_ref, wd_ref, lg_ref, lb_ref, o_ref, acc_ref, *, alpha, hc):
    x = x_ref[...]
    xb = x.astype(BF16)
    hidden = wg_ref.shape[1]
    for c in range(hidden // hc):
        sl = slice(c * hc, (c + 1) * hc)
        g = jnp.dot(xb, wg_ref[:, sl], preferred_element_type=F32)
        u = jnp.dot(xb, wu_ref[:, sl], preferred_element_type=F32)
        h = (g / (1.0 + jnp.exp(-g)) * u).astype(BF16)
        part = jnp.dot(h, wd_ref[sl, :], preferred_element_type=F32)
        if c == 0:
            acc_ref[...] = part
        else:
            acc_ref[...] += part
    o_ref[...] = _layer_norm(alpha * x + acc_ref[...], lg_ref[...], lb_ref[...])


def _ffn(x1, w_gate, w_up, w_down, lg, lb, *, alpha, tm, hc):
    tokens, d_model = x1.shape
    full = pl.BlockSpec((tm, d_model), lambda t: (t, 0))
    frow = pl.BlockSpec((1, d_model), lambda t: (0, 0))
    const = lambda a: pl.BlockSpec(a.shape, lambda t: (0, 0), pipeline_mode=pl.Buffered(1))
    return pl.pallas_call(
        functools.partial(_ffn_kernel, alpha=alpha, hc=hc),
        grid=(tokens // tm,),
        in_specs=[full, const(w_gate), const(w_up), const(w_down), frow, frow],
        out_specs=full,
        out_shape=jax.ShapeDtypeStruct((tokens, d_model), F32),
        scratch_shapes=[pltpu.VMEM((tm, d_model), F32)],
        compiler_params=pltpu.CompilerParams(dimension_semantics=("arbitrary",),
                                             vmem_limit_bytes=VMEM_LIMIT),
        name="swiglu_ffn_ln",
    )(x1, w_gate, w_up, w_down, lg, lb)


def _rope_tables(seq):
    inv_freq = ROPE_THETA ** (-jnp.arange(0, HEAD_DIM, 2, dtype=F32) / HEAD_DIM)
    ang = jnp.arange(seq, dtype=jnp.int32).astype(F32)[:, None] * inv_freq[None, :]
    ang = jnp.concatenate([ang, ang, ang, ang], axis=-1)
    cos = jnp.cos(ang)
    sin = jnp.sin(ang)
    first_half = (jnp.arange(LANES) % HEAD_DIM) < HEAD_DIM // 2
    sina = jnp.where(first_half[None, :], -sin, 0.0)
    sinb = jnp.where(first_half[None, :], 0.0, sin)
    return cos, sina, sinb


def kernel(x, w_in, attn_out_g, gmlp_out_g, gmlp_ln_g, gmlp_ln_b, w_spatial, b_spatial, w_out,
           ln1_g, ln1_b, w_gate, w_up, w_down, ln2_g, ln2_b):
    batch, seq, d_model = x.shape
    depth = w_in.shape[0]
    width = attn_out_g.shape[-1]
    alpha = (2 * depth) ** 0.25
    assert w_in.shape[-1] == 5 * width and seq % MOBA_BLOCK == 0 and width % LANES == 0
    assert min(MOBA_TOPK, seq // MOBA_BLOCK - 1) == MOBA_TOPK
    cos, sina, sinb = _rope_tables(seq)
    x2 = x.reshape(batch * seq, d_model)
    for l in range(depth):
        q, k, v, u, z = _in_proj(x2, w_in[l].astype(BF16), cos, sina, sinb,
                                 gmlp_ln_g[l].reshape(1, width), gmlp_ln_b[l].reshape(1, width),
                                 seq=seq, width=width, tm=512)
        attn = _attention(q, k, v, batch=batch, seq=seq)
        bias_full = jnp.repeat(b_spatial[l].T, HEAD_DIM, axis=1)
        sg = _gmlp(u, z, w_spatial[l], bias_full, tm=512)
        x1 = _out_proj(attn, sg, x2, attn_out_g[l].reshape(1, width), gmlp_out_g[l].reshape(1, width),
                       w_out[l].astype(BF16), ln1_g[l].reshape(1, d_model), ln1_b[l].reshape(1, d_model),
                       alpha=alpha, tm=512)
        x2 = _ffn(x1, w_gate[l].astype(BF16), w_up[l].astype(BF16), w_down[l].astype(BF16),
                  ln2_g[l].reshape(1, d_model), ln2_b[l].reshape(1, d_model),
                  alpha=alpha, tm=512, hc=256)
    return x2.reshape(batch, seq, d_model)
```

```python
import functools

import jax
import jax.numpy as jnp
from jax import lax
from jax.experimental import pallas as pl
from jax.experimental.pallas import tpu as pltpu

HEAD_DIM = 64
MOBA_BLOCK = 256
MOBA_TOPK = 3
GMLP_CHUNK = 128
ROPE_THETA = 10000.0
LN_EPS = 1e-5
RMS_EPS = 1e-6
NEG_INF = -1e30
LANES = 128
VT_ROWS = HEAD_DIM + 16

F32 = jnp.float32
BF16 = jnp.bfloat16
_NT = (((1,), (1,)), ((), ()))

VMEM_LIMIT = 48 * 1024 * 1024


def _erf_gelu(t):
    return 0.5 * t * (1.0 + lax.erf(t * 0.7071067811865476))


def _in_proj_kernel(x_ref, w_ref, cos_ref, sina_ref, sinb_ref, lng_ref, lnb_ref,
                    q_ref, k_ref, v_ref, u_ref, z_ref, *, width):
    x = x_ref[...].astype(BF16)
    cos = cos_ref[...]
    sina = sina_ref[...]
    sinb = sinb_ref[...]
    lane = lax.broadcasted_iota(jnp.int32, (1, LANES), 1)
    lo = lane < HEAD_DIM

    def proj(seg):
        return jnp.dot(x, w_ref[:, seg * width:(seg + 1) * width], preferred_element_type=F32)

    def rope(t):
        return t * cos + pltpu.roll(t, 96, 1) * sina + pltpu.roll(t, 32, 1) * sinb

    hq = proj(0)
    hk = proj(1)
    for c in range(width // LANES):
        sl = slice(c * LANES, (c + 1) * LANES)
        q_ref[:, sl] = rope(hq[:, sl] * (HEAD_DIM ** -0.5)).astype(BF16)
        k_ref[:, sl] = rope(hk[:, sl]).astype(BF16)
    v_ref[...] = proj(2).astype(BF16)
    u_ref[...] = _erf_gelu(proj(3)).astype(BF16)
    hz = _erf_gelu(proj(4))
    for c in range(width // LANES):
        sl = slice(c * LANES, (c + 1) * LANES)
        zs = hz[:, sl]
        s_lo = jnp.sum(jnp.where(lo, zs, 0.0), axis=1, keepdims=True)
        s_hi = jnp.sum(jnp.where(lo, 0.0, zs), axis=1, keepdims=True)
        d = zs - jnp.where(lo, s_lo, s_hi) * (1.0 / HEAD_DIM)
        dd = d * d
        v_lo = jnp.sum(jnp.where(lo, dd, 0.0), axis=1, keepdims=True)
        v_hi = jnp.sum(jnp.where(lo, 0.0, dd), axis=1, keepdims=True)
        var = jnp.where(lo, v_lo, v_hi) * (1.0 / HEAD_DIM)
        z_ref[:, sl] = (d * lax.rsqrt(var + LN_EPS) * lng_ref[:, sl] + lnb_ref[:, sl]).astype(BF16)


def _in_proj(x2, w_in, cos, sina, sinb, lng, lnb, *, seq, width, tm):
    tokens, d_model = x2.shape
    n_pos_tiles = seq // tm
    out = jax.ShapeDtypeStruct((tokens, width), BF16)
    tile = pl.BlockSpec((tm, width), lambda t: (t, 0))
    table = pl.BlockSpec((tm, LANES), lambda t: (t % n_pos_tiles, 0))
    row = pl.BlockSpec((1, width), lambda t: (0, 0))
    return pl.pallas_call(
        functools.partial(_in_proj_kernel, width=width),
        grid=(tokens // tm,),
        in_specs=[pl.BlockSpec((tm, d_model), lambda t: (t, 0)),
                  pl.BlockSpec(w_in.shape, lambda t: (0, 0)),
                  table, table, table, row, row],
        out_specs=[tile] * 5,
        out_shape=[out] * 5,
        compiler_params=pltpu.CompilerParams(dimension_semantics=("arbitrary",),
                                             vmem_limit_bytes=VMEM_LIMIT),
        name="in_proj",
    )(x2, w_in, cos, sina, sinb, lng, lnb)


def _aligned(start, multiple):
    return start if isinstance(start, int) else pl.multiple_of(start, multiple)


def _attn_kernel(q_ref, k_ref, v_ref, o_ref, ka_ref, vt_ref, kmh_ref, kml_ref, qa0_ref, qa1_ref,
                 s0_ref, s1_ref, m0_ref, m1_ref, acc_ref, *, nb):
    blk = MOBA_BLOCK
    sub = blk // 8
    qa_refs, s_refs, m_refs = (qa0_ref, qa1_ref), (s0_ref, s1_ref), (m0_ref, m1_ref)
    assert nb % 4 == 0
    lane = lax.broadcasted_iota(jnp.int32, (1, LANES), 1)
    head_lanes = [(lane >= HEAD_DIM * h) & (lane < HEAD_DIM * (h + 1)) for h in range(2)]
    bias_off = [HEAD_DIM, 0]

    kf = k_ref[...].astype(F32)
    for h in range(2):
        km = jnp.sum(jnp.where(head_lanes[h], kf, 0.0).reshape(nb, blk, LANES), axis=1) * (1.0 / blk)
        hi = km.astype(BF16)
        kmh_ref[h] = hi
        kml_ref[h] = (km - hi.astype(F32)).astype(BF16)
    ones_row = jnp.where(lax.broadcasted_iota(jnp.int32, (VT_ROWS - HEAD_DIM, blk), 0) == 0, 1.0, 0.0)
    for j in range(nb):
        rows = slice(j * blk, (j + 1) * blk)
        kj = k_ref[rows, :]
        v_t = v_ref[rows, :].astype(F32).T
        for h in range(2):
            onehot = jnp.where(lane == bias_off[h] + j, 1.0, 0.0).astype(BF16)
            ka_ref[h, rows, :] = jnp.where(head_lanes[h], kj, onehot)
            vt_ref[j, h] = jnp.concatenate(
                [v_t[HEAD_DIM * h:HEAD_DIM * (h + 1)], ones_row], axis=0).astype(BF16)

    row = lax.broadcasted_iota(jnp.int32, (nb, blk), 0)
    rowf = row.astype(F32)
    kk = lax.broadcasted_iota(jnp.int32, (blk, blk), 0)
    qq = lax.broadcasted_iota(jnp.int32, (blk, blk), 1)

    def prepare_row(slot, rs, r):
        qa_ref, m_ref = qa_refs[slot], m_refs[slot]
        q_r = q_ref[pl.ds(_aligned(r * blk, blk), blk), :]
        for h in range(2):
            g = (lax.dot_general(kmh_ref[h], q_r, _NT, preferred_element_type=F32)
                 + lax.dot_general(kml_ref[h], q_r, _NT, preferred_element_type=F32))
            past = row < r
            g = jnp.where(past, g, NEG_INF)
            bias = jnp.where(row == r, 0.0, NEG_INF)
            for _ in range(MOBA_TOPK):
                top = jnp.max(g, axis=0, keepdims=True)
                first = jnp.min(jnp.where(g == top, rowf, float(nb)), axis=0, keepdims=True)
                pick = rowf == first
                bias = jnp.where(pick & past, 0.0, bias)
                g = jnp.where(pick, -jnp.inf, g)
            pieces = [bias, jnp.zeros((LANES - bias_off[h] - nb, blk), F32)]
            if bias_off[h]:
                pieces.insert(0, jnp.zeros((bias_off[h], blk), F32))
            bias_t = jnp.concatenate(pieces, axis=0).T
            qa_ref[rs, h] = jnp.where(head_lanes[h], q_r.astype(F32), bias_t).astype(BF16)
            m_ref[rs, h] = jnp.full((8, blk), NEG_INF, F32)

    def score_tile(slot, t, rs, j, causal):
        qa_ref, m_ref, s_ref = qa_refs[slot], m_refs[slot], s_refs[slot]
        k0 = _aligned(j * blk, blk)
        for h in range(2):
            s = lax.dot_general(ka_ref[h, pl.ds(k0, blk), :], qa_ref[rs, h], _NT,
                                preferred_element_type=F32)
            if causal is not None:
                s = s + causal
            s_ref[t, h] = s
            m_ref[rs, h] = jnp.maximum(m_ref[rs, h], jnp.max(s.reshape(sub, 8, blk), axis=0))

    def value_tile(slot, t, rs, j):
        m_ref, s_ref = m_refs[slot], s_refs[slot]
        for h in range(2):
            p = jnp.exp(s_ref[t, h].reshape(sub, 8, blk) - m_ref[rs, h]).reshape(blk, blk)
            acc_ref[rs, h] += jnp.dot(vt_ref[j, h], p.astype(BF16), preferred_element_type=F32)

    def past_tile(u, pi):
        if isinstance(pi, int):
            return (0, u) if u < pi else (1, u - pi)
        first = u < pi
        return jnp.where(first, 0, 1), jnp.where(first, u, u - pi)

    def tile_of(t, pi):
        rows = (pi, nb - 1 - pi)
        return (t, rows[t]) if t < 2 else past_tile(t - 2, pi)

    def step(score_pair, score_slot, value_pair, value_slot):
        if score_pair is not None:
            for rs in range(2):
                prepare_row(score_slot, rs, (score_pair, nb - 1 - score_pair)[rs])
            causal = jnp.where(kk <= qq, 0.0, NEG_INF)
        if value_pair is not None:
            for rs in range(2):
                for h in range(2):
                    acc_ref[rs, h] = jnp.zeros((VT_ROWS, blk), F32)
        for t in range(nb + 1):
            if score_pair is not None:
                rs, j = tile_of(t, score_pair)
                score_tile(score_slot, t, rs, j, causal if t < 2 else None)
            if value_pair is not None:
                rs, j = tile_of(t, value_pair)
                value_tile(value_slot, t, rs, j)
        if score_pair is not None:
            m_ref = m_refs[score_slot]
            for rs in range(2):
                for h in range(2):
                    m_ref[rs, h] = jnp.broadcast_to(jnp.max(m_ref[rs, h], axis=0, keepdims=True),
                                                    (8, blk))
        if value_pair is not None:
            for rs in range(2):
                outs = []
                for h in range(2):
                    a = acc_ref[rs, h]
                    outs.append(a[:HEAD_DIM] / a[HEAD_DIM:HEAD_DIM + 1])
                q0 = _aligned((value_pair, nb - 1 - value_pair)[rs] * blk, blk)
                o_ref[pl.ds(q0, blk), :] = jnp.concatenate(outs, axis=0).T

    n_pairs = nb // 2
    step(0, 0, None, None)

    def two_steps(i2, _):
        p = 2 * i2
        step(p + 1, 1, p, 0)
        step(p + 2, 0, p + 1, 1)
        return 0

    lax.fori_loop(0, (n_pairs - 2) // 2, two_steps, 0)
    step(n_pairs - 1, 1, n_pairs - 2, 0)
    step(None, None, n_pairs - 1, 1)


def _attention(q, k, v, *, batch, seq):
    width = q.shape[-1]
    nb = seq // MOBA_BLOCK
    q3, k3, v3 = (t.reshape(batch, seq, width) for t in (q, k, v))
    spec = pl.BlockSpec((None, seq, LANES), lambda b, p: (b, 0, p))
    out = pl.pallas_call(
        functools.partial(_attn_kernel, nb=nb),
        grid=(batch, width // LANES),
        in_specs=[spec, spec, spec],
        out_specs=spec,
        out_shape=jax.ShapeDtypeStruct((batch, seq, width), F32),
        scratch_shapes=[pltpu.VMEM((2, seq, LANES), BF16),
                        pltpu.VMEM((nb, 2, VT_ROWS, MOBA_BLOCK), BF16),
                        pltpu.VMEM((2, nb, LANES), BF16),
                        pltpu.VMEM((2, nb, LANES), BF16),
                        *[pltpu.VMEM((2, 2, MOBA_BLOCK, LANES), BF16)] * 2,
                        *[pltpu.VMEM((nb + 1, 2, MOBA_BLOCK, MOBA_BLOCK), F32)] * 2,
                        *[pltpu.VMEM((2, 2, 8, MOBA_BLOCK), F32)] * 2,
                        pltpu.VMEM((2, 2, VT_ROWS, MOBA_BLOCK), F32)],
        compiler_params=pltpu.CompilerParams(dimension_semantics=("arbitrary", "arbitrary"),
                                             vmem_limit_bytes=VMEM_LIMIT),
        name="moba_attention",
    )(q3, k3, v3)
    return out.reshape(batch * seq, width)


def _gmlp_kernel(u_ref, z_ref, w_ref, b_ref, o_ref, *, n_chunks, n_groups):
    c = GMLP_CHUNK
    ii = lax.broadcasted_iota(jnp.int32, (c, c), 0)
    jj = lax.broadcasted_iota(jnp.int32, (c, c), 1)
    lane = lax.broadcasted_iota(jnp.int32, (1, LANES), 1)
    lo = lane < HEAD_DIM
    w = [jnp.where(jj <= ii, w_ref[g], 0.0).astype(BF16) for g in range(n_groups)]
    for ch in range(n_chunks):
        rows = slice(ch * c, (ch + 1) * c)
        for pair in range(n_groups // 2):
            sl = slice(pair * LANES, (pair + 1) * LANES)
            z = z_ref[rows, sl]
            z_lo = jnp.where(lo, z, jnp.zeros_like(z))
            z_hi = jnp.where(lo, jnp.zeros_like(z), z)
            mixed = (jnp.dot(w[2 * pair], z_lo, preferred_element_type=F32)
                     + jnp.dot(w[2 * pair + 1], z_hi, preferred_element_type=F32)
                     + b_ref[:, sl])
            o_ref[rows, sl] = u_ref[rows, sl].astype(F32) * mixed


def _gmlp(u, z, w_spatial, bias_full, *, tm):
    tokens, width = u.shape
    n_groups = w_spatial.shape[0]
    tile = pl.BlockSpec((tm, width), lambda t: (t, 0))
    return pl.pallas_call(
        functools.partial(_gmlp_kernel, n_chunks=tm // GMLP_CHUNK, n_groups=n_groups),
        grid=(tokens // tm,),
        in_specs=[tile, tile,
                  pl.BlockSpec(w_spatial.shape, lambda t: (0, 0, 0)),
                  pl.BlockSpec(bias_full.shape, lambda t: (0, 0))],
        out_specs=tile,
        out_shape=jax.ShapeDtypeStruct((tokens, width), F32),
        compiler_params=pltpu.CompilerParams(dimension_semantics=("arbitrary",),
                                             vmem_limit_bytes=VMEM_LIMIT),
        name="gmlp_gating",
    )(u, z, w_spatial, bias_full)


def _layer_norm(y, g, b):
    mu = jnp.mean(y, axis=-1, keepdims=True)
    d = y - mu
    var = jnp.mean(d * d, axis=-1, keepdims=True)
    return d * lax.rsqrt(var + LN_EPS) * g + b


def _rms_norm(t, g):
    ms = jnp.mean(t * t, axis=-1, keepdims=True)
    return t * lax.rsqrt(ms + RMS_EPS) * g


def _out_proj_kernel(attn_ref, sg_ref, x_ref, ag_ref, gg_ref, w_ref, lg_ref, lb_ref, o_ref,
                     *, alpha, width):
    an = _rms_norm(attn_ref[...], ag_ref[...]).astype(BF16)
    sn = _rms_norm(sg_ref[...], gg_ref[...]).astype(BF16)
    mix = (jnp.dot(an, w_ref[:width, :], preferred_element_type=F32)
           + jnp.dot(sn, w_ref[width:, :], preferred_element_type=F32))
    o_ref[...] = _layer_norm(alpha * x_ref[...] + mix, lg_ref[...], lb_ref[...])


def _out_proj(attn, sg, x2, ag, gg, w_out, lg, lb, *, alpha, tm):
    tokens, width = attn.shape
    d_model = x2.shape[1]
    half = pl.BlockSpec((tm, width), lambda t: (t, 0))
    full = pl.BlockSpec((tm, d_model), lambda t: (t, 0))
    hrow = pl.BlockSpec((1, width), lambda t: (0, 0))
    frow = pl.BlockSpec((1, d_model), lambda t: (0, 0))
    return pl.pallas_call(
        functools.partial(_out_proj_kernel, alpha=alpha, width=width),
        grid=(tokens // tm,),
        in_specs=[half, half, full, hrow, hrow,
                  pl.BlockSpec(w_out.shape, lambda t: (0, 0)), frow, frow],
        out_specs=full,
        out_shape=jax.ShapeDtypeStruct((tokens, d_model), F32),
        compiler_params=pltpu.CompilerParams(dimension_semantics=("arbitrary",),
                                             vmem_limit_bytes=VMEM_LIMIT),
        name="out_proj_ln",
    )(attn, sg, x2, ag, gg, w_out, lg, lb)


def _ffn_kernel(x_ref, wg_ref, wu_ref, wd_ref, lg_ref, lb_ref, o_ref, acc_ref, *, alpha, hc):
    x = x_ref[...]
    xb = x.astype(BF16)
    hidden = wg_ref.shape[1]
    for c in range(hidden // hc):
        sl = slice(c * hc, (c + 1) * hc)
        g = jnp.dot(xb, wg_ref[:, sl], preferred_element_type=F32)
        u = jnp.dot(xb, wu_ref[:, sl], preferred_element_type=F32)
        h = (g / (1.0 + jnp.exp(-g)) * u).astype(BF16)
        part = jnp.dot(h, wd_ref[sl, :], preferred_element_type=F32)
        if c == 0:
            acc_ref[...] = part
        else:
            acc_ref[...] += part
    o_ref[...] = _layer_norm(alpha * x + acc_ref[...], lg_ref[...], lb_ref[...])


def _ffn(x1, w_gate, w_up, w_down, lg, lb, *, alpha, tm, hc):
    tokens, d_model = x1.shape
    full = pl.BlockSpec((tm, d_model), lambda t: (t, 0))
    frow = pl.BlockSpec((1, d_model), lambda t: (0, 0))
    const = lambda a: pl.BlockSpec(a.shape, lambda t: (0, 0), pipeline_mode=pl.Buffered(1))
    return pl.pallas_call(
        functools.partial(_ffn_kernel, alpha=alpha, hc=hc),
        grid=(tokens // tm,),
        in_specs=[full, const(w_gate), const(w_up), const(w_down), frow, frow],
        out_specs=full,
        out_shape=jax.ShapeDtypeStruct((tokens, d_model), F32),
        scratch_shapes=[pltpu.VMEM((tm, d_model), F32)],
        compiler_params=pltpu.CompilerParams(dimension_semantics=("arbitrary",),
                                             vmem_limit_bytes=VMEM_LIMIT),
        name="swiglu_ffn_ln",
    )(x1, w_gate, w_up, w_down, lg, lb)


def _rope_tables(seq):
    inv_freq = ROPE_THETA ** (-jnp.arange(0, HEAD_DIM, 2, dtype=F32) / HEAD_DIM)
    ang = jnp.arange(seq, dtype=jnp.int32).astype(F32)[:, None] * inv_freq[None, :]
    ang = jnp.concatenate([ang, ang, ang, ang], axis=-1)
    cos = jnp.cos(ang)
    sin = jnp.sin(ang)
    first_half = (jnp.arange(LANES) % HEAD_DIM) < HEAD_DIM // 2
    sina = jnp.where(first_half[None, :], -sin, 0.0)
    sinb = jnp.where(first_half[None, :], 0.0, sin)
    return cos, sina, sinb


def kernel(x, w_in, attn_out_g, gmlp_out_g, gmlp_ln_g, gmlp_ln_b, w_spatial, b_spatial, w_out,
           ln1_g, ln1_b, w_gate, w_up, w_down, ln2_g, ln2_b):
    batch, seq, d_model = x.shape
    depth = w_in.shape[0]
    width = attn_out_g.shape[-1]
    alpha = (2 * depth) ** 0.25
    assert w_in.shape[-1] == 5 * width and seq % MOBA_BLOCK == 0 and width % LANES == 0
    assert min(MOBA_TOPK, seq // MOBA_BLOCK - 1) == MOBA_TOPK
    cos, sina, sinb = _rope_tables(seq)
    x2 = x.reshape(batch * seq, d_model)
    for l in range(depth):
        q, k, v, u, z = _in_proj(x2, w_in[l].astype(BF16), cos, sina, sinb,
                                 gmlp_ln_g[l].reshape(1, width), gmlp_ln_b[l].reshape(1, width),
                                 seq=seq, width=width, tm=512)
        attn = _attention(q, k, v, batch=batch, seq=seq)
        bias_full = jnp.repeat(b_spatial[l].T, HEAD_DIM, axis=1)
        sg = _gmlp(u, z, w_spatial[l], bias_full, tm=512)
        x1 = _out_proj(attn, sg, x2, attn_out_g[l].reshape(1, width), gmlp_out_g[l].reshape(1, width),
                       w_out[l].astype(BF16), ln1_g[l].reshape(1, d_model), ln1_b[l].reshape(1, d_model),
                       alpha=alpha, tm=512)
        x2 = _ffn(x1, w_gate[l].astype(BF16), w_up[l].astype(BF16), w_down[l].astype(BF16),
                  ln2_g[l].reshape(1, d_model), ln2_b[l].reshape(1, d_model),
                  alpha=alpha, tm=512, hc=256)
    return x2.reshape(batch, seq, d_model)
```

```python
import functools

import jax
import jax.numpy as jnp
from jax import lax
from jax.experimental import pallas as pl
from jax.experimental.pallas import tpu as pltpu

HEAD_DIM = 64
MOBA_BLOCK = 256
MOBA_TOPK = 3
GMLP_CHUNK = 128
ROPE_THETA = 10000.0
LN_EPS = 1e-5
RMS_EPS = 1e-6
NEG_INF = -1e30
LANES = 128
VT_ROWS = HEAD_DIM + 16

F32 = jnp.float32
BF16 = jnp.bfloat16
_NT = (((1,), (1,)), ((), ()))

VMEM_LIMIT = 48 * 1024 * 1024


def _erf_gelu(t):
    return 0.5 * t * (1.0 + lax.erf(t * 0.7071067811865476))


def _in_proj_kernel(x_ref, w_ref, cos_ref, sina_ref, sinb_ref, lng_ref, lnb_ref,
                    q_ref, k_ref, v_ref, u_ref, z_ref, *, width, n_sub):
    lane = lax.broadcasted_iota(jnp.int32, (1, LANES), 1)
    lo = lane < HEAD_DIM
    sm = x_ref.shape[0] // n_sub
    n_seg = w_ref.shape[1] // width
    row_slices = [slice(s * sm, (s + 1) * sm) for s in range(n_sub)]

    def rope(t, rows):
        return (t * cos_ref[rows, :] + pltpu.roll(t, 96, 1) * sina_ref[rows, :]
                + pltpu.roll(t, 32, 1) * sinb_ref[rows, :])

    def group_norm(zs, sl):
        s_lo = jnp.sum(jnp.where(lo, zs, 0.0), axis=1, keepdims=True)
        s_hi = jnp.sum(jnp.where(lo, 0.0, zs), axis=1, keepdims=True)
        d = zs - jnp.where(lo, s_lo, s_hi) * (1.0 / HEAD_DIM)
        dd = d * d
        v_lo = jnp.sum(jnp.where(lo, dd, 0.0), axis=1, keepdims=True)
        v_hi = jnp.sum(jnp.where(lo, 0.0, dd), axis=1, keepdims=True)
        var = jnp.where(lo, v_lo, v_hi) * (1.0 / HEAD_DIM)
        return d * lax.rsqrt(var + LN_EPS) * lng_ref[:, sl] + lnb_ref[:, sl]

    def epilogue(seg, h, rows):
        if seg == 2:
            v_ref[rows, :] = h.astype(BF16)
        elif seg == 3:
            u_ref[rows, :] = _erf_gelu(h).astype(BF16)
        else:
            if seg == 4:
                h = _erf_gelu(h)
            for c in range(width // LANES):
                sl = slice(c * LANES, (c + 1) * LANES)
                if seg == 0:
                    q_ref[rows, sl] = rope(h[:, sl] * (HEAD_DIM ** -0.5), rows).astype(BF16)
                elif seg == 1:
                    k_ref[rows, sl] = rope(h[:, sl], rows).astype(BF16)
                else:
                    z_ref[rows, sl] = group_norm(h[:, sl], sl).astype(BF16)

    xs = [None] * n_sub
    hs = [[None] * n_seg for _ in range(n_sub)]
    for s in range(n_sub + 1):
        if s < n_sub:
            xs[s] = x_ref[row_slices[s], :].astype(BF16)
        for seg in range(n_seg):
            if s < n_sub:
                hs[s][seg] = jnp.dot(xs[s], w_ref[:, seg * width:(seg + 1) * width],
                                     preferred_element_type=F32)
            if s >= 1:
                epilogue(seg, hs[s - 1][seg], row_slices[s - 1])


def _in_proj(x2, w_in, cos, sina, sinb, lng, lnb, *, seq, width, tm, n_sub):
    tokens, d_model = x2.shape
    n_pos_tiles = seq // tm
    out = jax.ShapeDtypeStruct((tokens, width), BF16)
    tile = pl.BlockSpec((tm, width), lambda t: (t, 0))
    table = pl.BlockSpec((tm, LANES), lambda t: (t % n_pos_tiles, 0))
    row = pl.BlockSpec((1, width), lambda t: (0, 0))
    return pl.pallas_call(
        functools.partial(_in_proj_kernel, width=width, n_sub=n_sub),
        grid=(tokens // tm,),
        in_specs=[pl.BlockSpec((tm, d_model), lambda t: (t, 0)),
                  pl.BlockSpec(w_in.shape, lambda t: (0, 0)),
                  table, table, table, row, row],
        out_specs=[tile] * 5,
        out_shape=[out] * 5,
        compiler_params=pltpu.CompilerParams(dimension_semantics=("arbitrary",),
                                             vmem_limit_bytes=VMEM_LIMIT),
        name="in_proj",
    )(x2, w_in, cos, sina, sinb, lng, lnb)


def _aligned(start, multiple):
    return start if isinstance(start, int) else pl.multiple_of(start, multiple)


def _attn_kernel(q_ref, k_ref, v_ref, o_ref, ka_ref, vt_ref, kmh_ref, kml_ref, qa0_ref, qa1_ref,
                 s0_ref, s1_ref, m0_ref, m1_ref, acc_ref, *, nb):
    blk = MOBA_BLOCK
    sub = blk // 8
    qa_refs, s_refs, m_refs = (qa0_ref, qa1_ref), (s0_ref, s1_ref), (m0_ref, m1_ref)
    assert nb % 4 == 0
    lane = lax.broadcasted_iota(jnp.int32, (1, LANES), 1)
    head_lanes = [(lane >= HEAD_DIM * h) & (lane < HEAD_DIM * (h + 1)) for h in range(2)]
    bias_off = [HEAD_DIM, 0]

    kf = k_ref[...].astype(F32)
    for h in range(2):
        km = jnp.sum(jnp.where(head_lanes[h], kf, 0.0).reshape(nb, blk, LANES), axis=1) * (1.0 / blk)
        hi = km.astype(BF16)
        kmh_ref[h] = hi
        kml_ref[h] = (km - hi.astype(F32)).astype(BF16)
    ones_row = jnp.where(lax.broadcasted_iota(jnp.int32, (VT_ROWS - HEAD_DIM, blk), 0) == 0, 1.0, 0.0)
    for j in range(nb):
        rows = slice(j * blk, (j + 1) * blk)
        kj = k_ref[rows, :]
        v_t = v_ref[rows, :].astype(F32).T
        for h in range(2):
            onehot = jnp.where(lane == bias_off[h] + j, 1.0, 0.0).astype(BF16)
            ka_ref[h, rows, :] = jnp.where(head_lanes[h], kj, onehot)
            vt_ref[j, h] = jnp.concatenate(
                [v_t[HEAD_DIM * h:HEAD_DIM * (h + 1)], ones_row], axis=0).astype(BF16)

    row = lax.broadcasted_iota(jnp.int32, (nb, blk), 0)
    rowf = row.astype(F32)
    kk = lax.broadcasted_iota(jnp.int32, (blk, blk), 0)
    qq = lax.broadcasted_iota(jnp.int32, (blk, blk), 1)

    def prepare_row(slot, rs, r):
        qa_ref, m_ref = qa_refs[slot], m_refs[slot]
        q_r = q_ref[pl.ds(_aligned(r * blk, blk), blk), :]
        for h in range(2):
            g = (lax.dot_general(kmh_ref[h], q_r, _NT, preferred_element_type=F32)
                 + lax.dot_general(kml_ref[h], q_r, _NT, preferred_element_type=F32))
            past = row < r
            g = jnp.where(past, g, NEG_INF)
            bias = jnp.where(row == r, 0.0, NEG_INF)
            for _ in range(MOBA_TOPK):
                top = jnp.max(g, axis=0, keepdims=True)
                first = jnp.min(jnp.where(g == top, rowf, float(nb)), axis=0, keepdims=True)
                pick = rowf == first
                bias = jnp.where(pick & past, 0.0, bias)
                g = jnp.where(pick, -jnp.inf, g)
            pieces = [bias, jnp.zeros((LANES - bias_off[h] - nb, blk), F32)]
            if bias_off[h]:
                pieces.insert(0, jnp.zeros((bias_off[h], blk), F32))
            bias_t = jnp.concatenate(pieces, axis=0).T
            qa_ref[rs, h] = jnp.where(head_lanes[h], q_r.astype(F32), bias_t).astype(BF16)
            m_ref[rs, h] = jnp.full((8, blk), NEG_INF, F32)

    def score_tile(slot, t, rs, j, causal):
        qa_ref, m_ref, s_ref = qa_refs[slot], m_refs[slot], s_refs[slot]
        k0 = _aligned(j * blk, blk)
        for h in range(2):
            s = lax.dot_general(ka_ref[h, pl.ds(k0, blk), :], qa_ref[rs, h], _NT,
                                preferred_element_type=F32)
            if causal is not None:
                s = s + causal
            s_ref[t, h] = s
            m_ref[rs, h] = jnp.maximum(m_ref[rs, h], jnp.max(s.reshape(sub, 8, blk), axis=0))

    def value_tile(slot, t, rs, j):
        m_ref, s_ref = m_refs[slot], s_refs[slot]
        for h in range(2):
            p = jnp.exp(s_ref[t, h].reshape(sub, 8, blk) - m_ref[rs, h]).reshape(blk, blk)
            acc_ref[rs, h] += jnp.dot(vt_ref[j, h], p.astype(BF16), preferred_element_type=F32)

    def past_tile(u, pi):
        if isinstance(pi, int):
            return (0, u) if u < pi else (1, u - pi)
        first = u < pi
        return jnp.where(first, 0, 1), jnp.where(first, u, u - pi)

    def tile_of(t, pi):
        rows = (pi, nb - 1 - pi)
        return (t, rows[t]) if t < 2 else past_tile(t - 2, pi)

    def step(score_pair, score_slot, value_pair, value_slot):
        if score_pair is not None:
            for rs in range(2):
                prepare_row(score_slot, rs, (score_pair, nb - 1 - score_pair)[rs])
            causal = jnp.where(kk <= qq, 0.0, NEG_INF)
        if value_pair is not None:
            for rs in range(2):
                for h in range(2):
                    acc_ref[rs, h] = jnp.zeros((VT_ROWS, blk), F32)
        for t in range(nb + 1):
            if score_pair is not None:
                rs, j = tile_of(t, score_pair)
                score_tile(score_slot, t, rs, j, causal if t < 2 else None)
            if value_pair is not None:
                rs, j = tile_of(t, value_pair)
                value_tile(value_slot, t, rs, j)
        if score_pair is not None:
            m_ref = m_refs[score_slot]
            for rs in range(2):
                for h in range(2):
                    m_ref[rs, h] = jnp.broadcast_to(jnp.max(m_ref[rs, h], axis=0, keepdims=True),
                                                    (8, blk))
        if value_pair is not None:
            for rs in range(2):
                outs = []
                for h in range(2):
                    a = acc_ref[rs, h]
                    outs.append(a[:HEAD_DIM] / a[HEAD_DIM:HEAD_DIM + 1])
                q0 = _aligned((value_pair, nb - 1 - value_pair)[rs] * blk, blk)
                o_ref[pl.ds(q0, blk), :] = jnp.concatenate(outs, axis=0).T.astype(BF16)

    n_pairs = nb // 2
    step(0, 0, None, None)

    def two_steps(i2, _):
        p = 2 * i2
        step(p + 1, 1, p, 0)
        step(p + 2, 0, p + 1, 1)
        return 0

    lax.fori_loop(0, (n_pairs - 2) // 2, two_steps, 0)
    step(n_pairs - 1, 1, n_pairs - 2, 0)
    step(None, None, n_pairs - 1, 1)


def _attention(q, k, v, *, batch, seq):
    width = q.shape[-1]
    nb = seq // MOBA_BLOCK
    q3, k3, v3 = (t.reshape(batch, seq, width) for t in (q, k, v))
    spec = pl.BlockSpec((None, seq, LANES), lambda b, p: (b, 0, p))
    out = pl.pallas_call(
        functools.partial(_attn_kernel, nb=nb),
        grid=(batch, width // LANES),
        in_specs=[spec, spec, spec],
        out_specs=spec,
        out_shape=jax.ShapeDtypeStruct((batch, seq, width), BF16),
        scratch_shapes=[pltpu.VMEM((2, seq, LANES), BF16),
                        pltpu.VMEM((nb, 2, VT_ROWS, MOBA_BLOCK), BF16),
                        pltpu.VMEM((2, nb, LANES), BF16),
                        pltpu.VMEM((2, nb, LANES), BF16),
                        *[pltpu.VMEM((2, 2, MOBA_BLOCK, LANES), BF16)] * 2,
                        *[pltpu.VMEM((nb + 1, 2, MOBA_BLOCK, MOBA_BLOCK), F32)] * 2,
                        *[pltpu.VMEM((2, 2, 8, MOBA_BLOCK), F32)] * 2,
                        pltpu.VMEM((2, 2, VT_ROWS, MOBA_BLOCK), F32)],
        compiler_params=pltpu.CompilerParams(dimension_semantics=("arbitrary", "arbitrary"),
                                             vmem_limit_bytes=VMEM_LIMIT),
        name="moba_attention",
    )(q3, k3, v3)
    return out.reshape(batch * seq, width)


def _layer_norm(y, g, b):
    mu = jnp.mean(y, axis=-1, keepdims=True)
    d = y - mu
    var = jnp.mean(d * d, axis=-1, keepdims=True)
    return d * lax.rsqrt(var + LN_EPS) * g + b


def _rms_norm(t, g):
    ms = jnp.mean(t * t, axis=-1, keepdims=True)
    return t * lax.rsqrt(ms + RMS_EPS) * g


def _mix_kernel(attn_ref, u_ref, z_ref, x_ref, ws_ref, bs_ref, ag_ref, gg_ref, w_ref, lg_ref, lb_ref,
                o_ref, sg_ref, *, alpha, width, n_sub):
    c = GMLP_CHUNK
    sm = x_ref.shape[0] // n_sub
    n_pairs = ws_ref.shape[0] // 2
    ii = lax.broadcasted_iota(jnp.int32, (c, c), 0)
    jj = lax.broadcasted_iota(jnp.int32, (c, c), 1)
    lane = lax.broadcasted_iota(jnp.int32, (1, LANES), 1)
    lo = lane < HEAD_DIM
    w_cat = [jnp.concatenate([jnp.where(jj <= ii, ws_ref[2 * p + g], 0.0) for g in range(2)],
                             axis=1).astype(BF16) for p in range(n_pairs)]

    def z_stack(rows, sl):
        z = z_ref[rows, sl]
        zero = jnp.zeros_like(z)
        return jnp.concatenate([jnp.where(lo, z, zero), jnp.where(lo, zero, z)], axis=0)

    def gate_and_norm(s):
        for c2 in range(sm // (2 * c)):
            r0 = s * sm + c2 * 2 * c
            rows = [slice(r0, r0 + c), slice(r0 + c, r0 + 2 * c)]
            for p in range(n_pairs):
                sl = slice(p * LANES, (p + 1) * LANES)
                rhs = jnp.concatenate([z_stack(rows[0], sl), z_stack(rows[1], sl)], axis=1)
                mixed = jnp.dot(w_cat[p], rhs, preferred_element_type=F32)
                for k in range(2):
                    sg_ref[rows[k], sl] = (u_ref[rows[k], sl].astype(F32)
                                           * (mixed[:, k * LANES:(k + 1) * LANES] + bs_ref[:, sl]))
        rows = slice(s * sm, (s + 1) * sm)
        an = _rms_norm(attn_ref[rows, :].astype(F32), ag_ref[...]).astype(BF16)
        sn = _rms_norm(sg_ref[rows, :], gg_ref[...]).astype(BF16)
        return an, sn

    def project(normed, cols):
        an, sn = normed
        return (jnp.dot(an, w_ref[:width, cols], preferred_element_type=F32)
                + jnp.dot(sn, w_ref[width:, cols], preferred_element_type=F32))

    def residual_norm(s, mix):
        rows = slice(s * sm, (s + 1) * sm)
        o_ref[rows, :] = _layer_norm(alpha * x_ref[rows, :] + mix, lg_ref[...], lb_ref[...])

    d_model = w_ref.shape[1]
    col_halves = [slice(0, d_model // 2), slice(d_model // 2, d_model)]
    normed = [None] * n_sub
    mixes = [None] * n_sub
    for s in range(n_sub + 2):
        if s < n_sub:
            normed[s] = gate_and_norm(s)
        if 1 <= s <= n_sub:
            left = project(normed[s - 1], col_halves[0])
        if 2 <= s:
            residual_norm(s - 2, mixes[s - 2])
        if 1 <= s <= n_sub:
            mixes[s - 1] = jnp.concatenate([left, project(normed[s - 1], col_halves[1])], axis=1)


def _mix_proj(attn, u, z, x2, w_spatial, bias_full, ag, gg, w_out, lg, lb, *, alpha, tm, n_sub):
    tokens, width = attn.shape
    d_model = x2.shape[1]
    assert (tm // n_sub) % (2 * GMLP_CHUNK) == 0
    half = pl.BlockSpec((tm, width), lambda t: (t, 0))
    full = pl.BlockSpec((tm, d_model), lambda t: (t, 0))
    hrow = pl.BlockSpec((1, width), lambda t: (0, 0))
    frow = pl.BlockSpec((1, d_model), lambda t: (0, 0))
    return pl.pallas_call(
        functools.partial(_mix_kernel, alpha=alpha, width=width, n_sub=n_sub),
        grid=(tokens // tm,),
        in_specs=[half, half, half, full,
                  pl.BlockSpec(w_spatial.shape, lambda t: (0, 0, 0)),
                  pl.BlockSpec(bias_full.shape, lambda t: (0, 0)),
                  hrow, hrow, pl.BlockSpec(w_out.shape, lambda t: (0, 0)), frow, frow],
        out_specs=full,
        out_shape=jax.ShapeDtypeStruct((tokens, d_model), F32),
        scratch_shapes=[pltpu.VMEM((tm, width), F32)],
        compiler_params=pltpu.CompilerParams(dimension_semantics=("arbitrary",),
                                             vmem_limit_bytes=VMEM_LIMIT),
        name="gmlp_out_proj_ln",
    )(attn, u, z, x2, w_spatial, bias_full, ag, gg, w_out, lg, lb)


def _ffn_kernel(x_ref, wg_ref, wu_ref, wd_ref, lg_ref, lb_ref, o_ref, acc_ref, *, alpha, hc, n_sub):
    hidden = wg_ref.shape[1]
    sm = x_ref.shape[0] // n_sub

    def residual_norm(s):
        rows = slice(s * sm, (s + 1) * sm)
        o_ref[rows, :] = _layer_norm(alpha * x_ref[rows, :] + acc_ref[rows, :], lg_ref[...], lb_ref[...])

    for s in range(n_sub):
        rows = slice(s * sm, (s + 1) * sm)
        xb = x_ref[rows, :].astype(BF16)
        for c in range(hidden // hc):
            sl = slice(c * hc, (c + 1) * hc)
            g = jnp.dot(xb, wg_ref[:, sl], preferred_element_type=F32)
            u = jnp.dot(xb, wu_ref[:, sl], preferred_element_type=F32)
            h = (g / (1.0 + jnp.exp(-g)) * u).astype(BF16)
            part = jnp.dot(h, wd_ref[sl, :], preferred_element_type=F32)
            if c == 0:
                acc_ref[rows, :] = part
            else:
                acc_ref[rows, :] += part
            if c == 1 and s >= 1:
                residual_norm(s - 1)
    residual_norm(n_sub - 1)


def _ffn(x1, w_gate, w_up, w_down, lg, lb, *, alpha, tm, hc, n_sub):
    tokens, d_model = x1.shape
    full = pl.BlockSpec((tm, d_model), lambda t: (t, 0))
    frow = pl.BlockSpec((1, d_model), lambda t: (0, 0))
    const = lambda a: pl.BlockSpec(a.shape, lambda t: (0, 0), pipeline_mode=pl.Buffered(1))
    return pl.pallas_call(
        functools.partial(_ffn_kernel, alpha=alpha, hc=hc, n_sub=n_sub),
        grid=(tokens // tm,),
        in_specs=[full, const(w_gate), const(w_up), const(w_down), frow, frow],
        out_specs=full,
        out_shape=jax.ShapeDtypeStruct((tokens, d_model), F32),
        scratch_shapes=[pltpu.VMEM((tm, d_model), F32)],
        compiler_params=pltpu.CompilerParams(dimension_semantics=("arbitrary",),
                                             vmem_limit_bytes=VMEM_LIMIT),
        name="swiglu_ffn_ln",
    )(x1, w_gate, w_up, w_down, lg, lb)


def _rope_tables(seq):
    inv_freq = ROPE_THETA ** (-jnp.arange(0, HEAD_DIM, 2, dtype=F32) / HEAD_DIM)
    ang = jnp.arange(seq, dtype=jnp.int32).astype(F32)[:, None] * inv_freq[None, :]
    ang = jnp.concatenate([ang, ang, ang, ang], axis=-1)
    cos = jnp.cos(ang)
    sin = jnp.sin(ang)
    first_half = (jnp.arange(LANES) % HEAD_DIM) < HEAD_DIM // 2
    sina = jnp.where(first_half[None, :], -sin, 0.0)
    sinb = jnp.where(first_half[None, :], 0.0, sin)
    return cos, sina, sinb


def kernel(x, w_in, attn_out_g, gmlp_out_g, gmlp_ln_g, gmlp_ln_b, w_spatial, b_spatial, w_out,
           ln1_g, ln1_b, w_gate, w_up, w_down, ln2_g, ln2_b):
    batch, seq, d_model = x.shape
    depth = w_in.shape[0]
    width = attn_out_g.shape[-1]
    alpha = (2 * depth) ** 0.25
    assert w_in.shape[-1] == 5 * width and seq % MOBA_BLOCK == 0 and width % LANES == 0
    assert min(MOBA_TOPK, seq // MOBA_BLOCK - 1) == MOBA_TOPK
    cos, sina, sinb = _rope_tables(seq)
    x2 = x.reshape(batch * seq, d_model)
    for l in range(depth):
        q, k, v, u, z = _in_proj(x2, w_in[l].astype(BF16), cos, sina, sinb,
                                 gmlp_ln_g[l].reshape(1, width), gmlp_ln_b[l].reshape(1, width),
                                 seq=seq, width=width, tm=1024, n_sub=4)
        attn = _attention(q, k, v, batch=batch, seq=seq)
        bias_full = jnp.repeat(b_spatial[l].T, HEAD_DIM, axis=1)
        x1 = _mix_proj(attn, u, z, x2, w_spatial[l], bias_full,
                       attn_out_g[l].reshape(1, width), gmlp_out_g[l].reshape(1, width),
                       w_out[l].astype(BF16), ln1_g[l].reshape(1, d_model), ln1_b[l].reshape(1, d_model),
                       alpha=alpha, tm=1024, n_sub=4)
        x2 = _ffn(x1, w_gate[l].astype(BF16), w_up[l].astype(BF16), w_down[l].astype(BF16),
                  ln2_g[l].reshape(1, d_model), ln2_b[l].reshape(1, d_model),
                  alpha=alpha, tm=1024, hc=256, n_sub=2)
    return x2.reshape(batch, seq, d_model)
```

```python
import functools

import jax
import jax.numpy as jnp
from jax import lax
from jax.experimental import pallas as pl
from jax.experimental.pallas import tpu as pltpu

HEAD_DIM = 64
MOBA_BLOCK = 256
MOBA_TOPK = 3
GMLP_CHUNK = 128
ROPE_THETA = 10000.0
LN_EPS = 1e-5
RMS_EPS = 1e-6
NEG_INF = -1e30
LANES = 128
VT_ROWS = HEAD_DIM + 16

F32 = jnp.float32
BF16 = jnp.bfloat16
_NT = (((1,), (1,)), ((), ()))

VMEM_LIMIT = 48 * 1024 * 1024


def _erf_gelu(t):
    return 0.5 * t * (1.0 + lax.erf(t * 0.7071067811865476))


def _in_proj_kernel(x_ref, w_ref, cos_ref, sina_ref, sinb_ref, lng_ref, lnb_ref,
                    q_ref, k_ref, v_ref, u_ref, z_ref, *, width, n_sub):
    lane = lax.broadcasted_iota(jnp.int32, (1, LANES), 1)
    lo = lane < HEAD_DIM
    sm = x_ref.shape[0] // n_sub
    n_seg = w_ref.shape[1] // width
    row_slices = [slice(s * sm, (s + 1) * sm) for s in range(n_sub)]

    def rope(t, rows):
        return (t * cos_ref[rows, :] + pltpu.roll(t, 96, 1) * sina_ref[rows, :]
                + pltpu.roll(t, 32, 1) * sinb_ref[rows, :])

    def group_norm(zs, sl):
        s_lo = jnp.sum(jnp.where(lo, zs, 0.0), axis=1, keepdims=True)
        s_hi = jnp.sum(jnp.where(lo, 0.0, zs), axis=1, keepdims=True)
        d = zs - jnp.where(lo, s_lo, s_hi) * (1.0 / HEAD_DIM)
        dd = d * d
        v_lo = jnp.sum(jnp.where(lo, dd, 0.0), axis=1, keepdims=True)
        v_hi = jnp.sum(jnp.where(lo, 0.0, dd), axis=1, keepdims=True)
        var = jnp.where(lo, v_lo, v_hi) * (1.0 / HEAD_DIM)
        return d * lax.rsqrt(var + LN_EPS) * lng_ref[:, sl] + lnb_ref[:, sl]

    def epilogue(seg, h, rows):
        if seg == 2:
            v_ref[rows, :] = h.astype(BF16)
        elif seg == 3:
            u_ref[rows, :] = _erf_gelu(h).astype(BF16)
        else:
            if seg == 4:
                h = _erf_gelu(h)
            for c in range(width // LANES):
                sl = slice(c * LANES, (c + 1) * LANES)
                if seg == 0:
                    q_ref[rows, sl] = rope(h[:, sl] * (HEAD_DIM ** -0.5), rows).astype(BF16)
                elif seg == 1:
                    k_ref[rows, sl] = rope(h[:, sl], rows).astype(BF16)
                else:
                    z_ref[rows, sl] = group_norm(h[:, sl], sl).astype(BF16)

    xs = [None] * n_sub
    hs = [[None] * n_seg for _ in range(n_sub)]
    for s in range(n_sub + 1):
        if s < n_sub:
            xs[s] = x_ref[row_slices[s], :].astype(BF16)
        for seg in range(n_seg):
            if s < n_sub:
                hs[s][seg] = jnp.dot(xs[s], w_ref[:, seg * width:(seg + 1) * width],
                                     preferred_element_type=F32)
            if s >= 1:
                epilogue(seg, hs[s - 1][seg], row_slices[s - 1])


def _in_proj(x2, w_in, cos, sina, sinb, lng, lnb, *, seq, width, tm, n_sub):
    tokens, d_model = x2.shape
    n_pos_tiles = seq // tm
    out = jax.ShapeDtypeStruct((tokens, width), BF16)
    tile = pl.BlockSpec((tm, width), lambda t: (t, 0))
    table = pl.BlockSpec((tm, LANES), lambda t: (t % n_pos_tiles, 0))
    row = pl.BlockSpec((1, width), lambda t: (0, 0))
    return pl.pallas_call(
        functools.partial(_in_proj_kernel, width=width, n_sub=n_sub),
        grid=(tokens // tm,),
        in_specs=[pl.BlockSpec((tm, d_model), lambda t: (t, 0)),
                  pl.BlockSpec(w_in.shape, lambda t: (0, 0)),
                  table, table, table, row, row],
        out_specs=[tile] * 5,
        out_shape=[out] * 5,
        compiler_params=pltpu.CompilerParams(dimension_semantics=("arbitrary",),
                                             vmem_limit_bytes=VMEM_LIMIT),
        name="in_proj",
    )(x2, w_in, cos, sina, sinb, lng, lnb)


def _aligned(start, multiple):
    return start if isinstance(start, int) else pl.multiple_of(start, multiple)


def _attn_kernel(q_ref, k_ref, v_ref, o_ref, ka_ref, vt_ref, kmh_ref, kml_ref, qa0_ref, qa1_ref,
                 s0_ref, s1_ref, m0_ref, m1_ref, acc0_ref, acc1_ref, *, nb):
    blk = MOBA_BLOCK
    sub = blk // 8
    qa_refs, s_refs, m_refs = (qa0_ref, qa1_ref), (s0_ref, s1_ref), (m0_ref, m1_ref)
    acc_refs = (acc0_ref, acc1_ref)
    assert nb % 4 == 0
    lane = lax.broadcasted_iota(jnp.int32, (1, LANES), 1)
    head_lanes = [(lane >= HEAD_DIM * h) & (lane < HEAD_DIM * (h + 1)) for h in range(2)]
    bias_off = [HEAD_DIM, 0]

    kf = k_ref[...].astype(F32)
    for h in range(2):
        km = jnp.sum(jnp.where(head_lanes[h], kf, 0.0).reshape(nb, blk, LANES), axis=1) * (1.0 / blk)
        hi = km.astype(BF16)
        kmh_ref[h] = hi
        kml_ref[h] = (km - hi.astype(F32)).astype(BF16)
    ones_row = jnp.where(lax.broadcasted_iota(jnp.int32, (VT_ROWS - HEAD_DIM, blk), 0) == 0, 1.0, 0.0)
    for j in range(nb):
        rows = slice(j * blk, (j + 1) * blk)
        kj = k_ref[rows, :]
        v_t = v_ref[rows, :].astype(F32).T
        for h in range(2):
            onehot = jnp.where(lane == bias_off[h] + j, 1.0, 0.0).astype(BF16)
            ka_ref[h, rows, :] = jnp.where(head_lanes[h], kj, onehot)
            vt_ref[j, h] = jnp.concatenate(
                [v_t[HEAD_DIM * h:HEAD_DIM * (h + 1)], ones_row], axis=0).astype(BF16)

    row = lax.broadcasted_iota(jnp.int32, (nb, blk), 0)
    rowf = row.astype(F32)
    kk = lax.broadcasted_iota(jnp.int32, (blk, blk), 0)
    qq = lax.broadcasted_iota(jnp.int32, (blk, blk), 1)

    def prepare_row(slot, rs, r):
        qa_ref = qa_refs[slot]
        q_r = q_ref[pl.ds(_aligned(r * blk, blk), blk), :]
        for h in range(2):
            g = (lax.dot_general(kmh_ref[h], q_r, _NT, preferred_element_type=F32)
                 + lax.dot_general(kml_ref[h], q_r, _NT, preferred_element_type=F32))
            past = row < r
            g = jnp.where(past, g, NEG_INF)
            bias = jnp.where(row == r, 0.0, NEG_INF)
            for _ in range(MOBA_TOPK):
                top = jnp.max(g, axis=0, keepdims=True)
                first = jnp.min(jnp.where(g == top, rowf, float(nb)), axis=0, keepdims=True)
                pick = rowf == first
                bias = jnp.where(pick & past, 0.0, bias)
                g = jnp.where(pick, -jnp.inf, g)
            pieces = [bias, jnp.zeros((LANES - bias_off[h] - nb, blk), F32)]
            if bias_off[h]:
                pieces.insert(0, jnp.zeros((bias_off[h], blk), F32))
            bias_t = jnp.concatenate(pieces, axis=0).T
            qa_ref[rs, h] = jnp.where(head_lanes[h], q_r.astype(F32), bias_t).astype(BF16)

    def score_tile(slot, t, rs, j, causal):
        qa_ref, m_ref, s_ref = qa_refs[slot], m_refs[slot], s_refs[slot]
        k0 = _aligned(j * blk, blk)
        for h in range(2):
            s = lax.dot_general(ka_ref[h, pl.ds(k0, blk), :], qa_ref[rs, h], _NT,
                                preferred_element_type=F32)
            if causal is not None:
                s = s + causal
            s_ref[t, h] = s
            m_ref[rs, h] = jnp.maximum(m_ref[rs, h], jnp.max(s.reshape(sub, 8, blk), axis=0))

    def value_tile(slot, t, rs, j):
        m_ref, s_ref, acc_ref = m_refs[slot], s_refs[slot], acc_refs[slot]
        for h in range(2):
            p = jnp.exp(s_ref[t, h].reshape(sub, 8, blk) - m_ref[rs, h]).reshape(blk, blk)
            acc_ref[rs, h] += jnp.dot(vt_ref[j, h], p.astype(BF16), preferred_element_type=F32)

    def past_tile(u, pi):
        if isinstance(pi, int):
            return (0, u) if u < pi else (1, u - pi)
        first = u < pi
        return jnp.where(first, 0, 1), jnp.where(first, u, u - pi)

    def tile_of(t, pi):
        rows = (pi, nb - 1 - pi)
        return (t, rows[t]) if t < 2 else past_tile(t - 2, pi)

    def write_rows(pair, slot, rs):
        outs = []
        for h in range(2):
            a = acc_refs[slot][rs, h]
            outs.append(a[:HEAD_DIM] / a[HEAD_DIM:HEAD_DIM + 1])
        q0 = _aligned((pair, nb - 1 - pair)[rs] * blk, blk)
        o_ref[pl.ds(q0, blk), :] = jnp.concatenate(outs, axis=0).T.astype(BF16)

    def step(k, slot):
        live = lambda p: not isinstance(p, int) or 0 <= p < nb // 2
        score_pair, value_pair, prep_pair, write_pair = k, k - 1, k + 1, k - 2
        do_score, do_value, do_prep, do_write = (live(p) for p in
                                                 (score_pair, value_pair, prep_pair, write_pair))
        if do_score:
            causal = jnp.where(kk <= qq, 0.0, NEG_INF)
            for rs in range(2):
                for h in range(2):
                    m_refs[slot][rs, h] = jnp.full((8, blk), NEG_INF, F32)
        if do_value:
            for rs in range(2):
                for h in range(2):
                    acc_refs[1 - slot][rs, h] = jnp.zeros((VT_ROWS, blk), F32)
        prep_at = {2: 0, nb // 2 + 1: 1}
        write_at = {1: 0, nb // 2: 1}
        for t in range(nb + 1):
            if do_score:
                rs, j = tile_of(t, score_pair)
                score_tile(slot, t, rs, j, causal if t < 2 else None)
            if do_value:
                rs, j = tile_of(t, value_pair)
                value_tile(1 - slot, t, rs, j)
            if do_prep and t in prep_at:
                rs = prep_at[t]
                prepare_row(1 - slot, rs, (prep_pair, nb - 1 - prep_pair)[rs])
            if do_write and t in write_at:
                write_rows(write_pair, slot, write_at[t])
        if do_score:
            m_ref = m_refs[slot]
            for rs in range(2):
                for h in range(2):
                    m_ref[rs, h] = jnp.broadcast_to(jnp.max(m_ref[rs, h], axis=0, keepdims=True),
                                                    (8, blk))

    n_pairs = nb // 2
    for rs in range(2):
        prepare_row(0, rs, (0, nb - 1)[rs])
    step(0, 0)
    step(1, 1)

    def two_steps(i2, _):
        step(2 * i2 + 2, 0)
        step(2 * i2 + 3, 1)
        return 0

    n_rolled = (n_pairs - 3) // 2
    lax.fori_loop(0, n_rolled, two_steps, 0)
    for k in range(2 + 2 * n_rolled, n_pairs + 2):
        step(k, k % 2)


def _attention(q, k, v, *, batch, seq):
    width = q.shape[-1]
    nb = seq // MOBA_BLOCK
    q3, k3, v3 = (t.reshape(batch, seq, width) for t in (q, k, v))
    spec = pl.BlockSpec((None, seq, LANES), lambda b, p: (b, 0, p))
    out = pl.pallas_call(
        functools.partial(_attn_kernel, nb=nb),
        grid=(batch, width // LANES),
        in_specs=[spec, spec, spec],
        out_specs=spec,
        out_shape=jax.ShapeDtypeStruct((batch, seq, width), BF16),
        scratch_shapes=[pltpu.VMEM((2, seq, LANES), BF16),
                        pltpu.VMEM((nb, 2, VT_ROWS, MOBA_BLOCK), BF16),
                        pltpu.VMEM((2, nb, LANES), BF16),
                        pltpu.VMEM((2, nb, LANES), BF16),
                        *[pltpu.VMEM((2, 2, MOBA_BLOCK, LANES), BF16)] * 2,
                        *[pltpu.VMEM((nb + 1, 2, MOBA_BLOCK, MOBA_BLOCK), F32)] * 2,
                        *[pltpu.VMEM((2, 2, 8, MOBA_BLOCK), F32)] * 2,
                        *[pltpu.VMEM((2, 2, VT_ROWS, MOBA_BLOCK), F32)] * 2],
        compiler_params=pltpu.CompilerParams(dimension_semantics=("arbitrary", "arbitrary"),
                                             vmem_limit_bytes=VMEM_LIMIT),
        name="moba_attention",
    )(q3, k3, v3)
    return out.reshape(batch * seq, width)


def _layer_norm(y, g, b):
    mu = jnp.mean(y, axis=-1, keepdims=True)
    d = y - mu
    var = jnp.mean(d * d, axis=-1, keepdims=True)
    return d * lax.rsqrt(var + LN_EPS) * g + b


def _rms_norm(t, g):
    ms = jnp.mean(t * t, axis=-1, keepdims=True)
    return t * lax.rsqrt(ms + RMS_EPS) * g


def _mix_kernel(attn_ref, u_ref, z_ref, x_ref, ws_ref, bs_ref, ag_ref, gg_ref, w_ref, lg_ref, lb_ref,
                o_ref, sg_ref, *, alpha, width, n_sub):
    c = GMLP_CHUNK
    sm = x_ref.shape[0] // n_sub
    n_pairs = ws_ref.shape[0] // 2
    ii = lax.broadcasted_iota(jnp.int32, (c, c), 0)
    jj = lax.broadcasted_iota(jnp.int32, (c, c), 1)
    lane = lax.broadcasted_iota(jnp.int32, (1, LANES), 1)
    lo = lane < HEAD_DIM
    w_cat = [jnp.concatenate([jnp.where(jj <= ii, ws_ref[2 * p + g], 0.0) for g in range(2)],
                             axis=1).astype(BF16) for p in range(n_pairs)]

    def z_stack(rows, sl):
        z = z_ref[rows, sl]
        zero = jnp.zeros_like(z)
        return jnp.concatenate([jnp.where(lo, z, zero), jnp.where(lo, zero, z)], axis=0)

    def gate_and_norm(s):
        for c2 in range(sm // (2 * c)):
            r0 = s * sm + c2 * 2 * c
            rows = [slice(r0, r0 + c), slice(r0 + c, r0 + 2 * c)]
            for p in range(n_pairs):
                sl = slice(p * LANES, (p + 1) * LANES)
                rhs = jnp.concatenate([z_stack(rows[0], sl), z_stack(rows[1], sl)], axis=1)
                mixed = jnp.dot(w_cat[p], rhs, preferred_element_type=F32)
                for k in range(2):
                    sg_ref[rows[k], sl] = (u_ref[rows[k], sl].astype(F32)
                                           * (mixed[:, k * LANES:(k + 1) * LANES] + bs_ref[:, sl]))
        rows = slice(s * sm, (s + 1) * sm)
        an = _rms_norm(attn_ref[rows, :].astype(F32), ag_ref[...]).astype(BF16)
        sn = _rms_norm(sg_ref[rows, :], gg_ref[...]).astype(BF16)
        return an, sn

    def project(normed, cols):
        an, sn = normed
        return (jnp.dot(an, w_ref[:width, cols], preferred_element_type=F32)
                + jnp.dot(sn, w_ref[width:, cols], preferred_element_type=F32))

    def residual_norm(s, mix):
        rows = slice(s * sm, (s + 1) * sm)
        o_ref[rows, :] = _layer_norm(alpha * x_ref[rows, :] + mix, lg_ref[...], lb_ref[...])

    d_model = w_ref.shape[1]
    col_halves = [slice(0, d_model // 2), slice(d_model // 2, d_model)]
    normed = [None] * n_sub
    mixes = [None] * n_sub
    for s in range(n_sub + 2):
        if s < n_sub:
            normed[s] = gate_and_norm(s)
        if 1 <= s <= n_sub:
            left = project(normed[s - 1], col_halves[0])
        if 2 <= s:
            residual_norm(s - 2, mixes[s - 2])
        if 1 <= s <= n_sub:
            mixes[s - 1] = jnp.concatenate([left, project(normed[s - 1], col_halves[1])], axis=1)


def _mix_proj(attn, u, z, x2, w_spatial, bias_full, ag, gg, w_out, lg, lb, *, alpha, tm, n_sub):
    tokens, width = attn.shape
    d_model = x2.shape[1]
    assert (tm // n_sub) % (2 * GMLP_CHUNK) == 0
    half = pl.BlockSpec((tm, width), lambda t: (t, 0))
    full = pl.BlockSpec((tm, d_model), lambda t: (t, 0))
    hrow = pl.BlockSpec((1, width), lambda t: (0, 0))
    frow = pl.BlockSpec((1, d_model), lambda t: (0, 0))
    return pl.pallas_call(
        functools.partial(_mix_kernel, alpha=alpha, width=width, n_sub=n_sub),
        grid=(tokens // tm,),
        in_specs=[half, half, half, full,
                  pl.BlockSpec(w_spatial.shape, lambda t: (0, 0, 0)),
                  pl.BlockSpec(bias_full.shape, lambda t: (0, 0)),
                  hrow, hrow, pl.BlockSpec(w_out.shape, lambda t: (0, 0)), frow, frow],
        out_specs=full,
        out_shape=jax.ShapeDtypeStruct((tokens, d_model), F32),
        scratch_shapes=[pltpu.VMEM((tm, width), F32)],
        compiler_params=pltpu.CompilerParams(dimension_semantics=("arbitrary",),
                                             vmem_limit_bytes=VMEM_LIMIT),
        name="gmlp_out_proj_ln",
    )(attn, u, z, x2, w_spatial, bias_full, ag, gg, w_out, lg, lb)


def _ffn_kernel(x_ref, wg_ref, wu_ref, wd_ref, lg_ref, lb_ref, o_ref, acc_ref, *, alpha, hc, n_sub):
    hidden = wg_ref.shape[1]
    sm = x_ref.shape[0] // n_sub

    def residual_norm(s):
        rows = slice(s * sm, (s + 1) * sm)
        o_ref[rows, :] = _layer_norm(alpha * x_ref[rows, :] + acc_ref[rows, :], lg_ref[...], lb_ref[...])

    for s in range(n_sub):
        rows = slice(s * sm, (s + 1) * sm)
        xb = x_ref[rows, :].astype(BF16)
        for c in range(hidden // hc):
            sl = slice(c * hc, (c + 1) * hc)
            g = jnp.dot(xb, wg_ref[:, sl], preferred_element_type=F32)
            u = jnp.dot(xb, wu_ref[:, sl], preferred_element_type=F32)
            h = (g / (1.0 + jnp.exp(-g)) * u).astype(BF16)
            part = jnp.dot(h, wd_ref[sl, :], preferred_element_type=F32)
            if c == 0:
                acc_ref[rows, :] = part
            else:
                acc_ref[rows, :] += part
            if c == 1 and s >= 1:
                residual_norm(s - 1)
    residual_norm(n_sub - 1)


def _ffn(x1, w_gate, w_up, w_down, lg, lb, *, alpha, tm, hc, n_sub):
    tokens, d_model = x1.shape
    full = pl.BlockSpec((tm, d_model), lambda t: (t, 0))
    frow = pl.BlockSpec((1, d_model), lambda t: (0, 0))
    const = lambda a: pl.BlockSpec(a.shape, lambda t: (0, 0), pipeline_mode=pl.Buffered(1))
    return pl.pallas_call(
        functools.partial(_ffn_kernel, alpha=alpha, hc=hc, n_sub=n_sub),
        grid=(tokens // tm,),
        in_specs=[full, const(w_gate), const(w_up), const(w_down), frow, frow],
        out_specs=full,
        out_shape=jax.ShapeDtypeStruct((tokens, d_model), F32),
        scratch_shapes=[pltpu.VMEM((tm, d_model), F32)],
        compiler_params=pltpu.CompilerParams(dimension_semantics=("arbitrary",),
                                             vmem_limit_bytes=VMEM_LIMIT),
        name="swiglu_ffn_ln",
    )(x1, w_gate, w_up, w_down, lg, lb)


def _rope_tables(seq):
    inv_freq = ROPE_THETA ** (-jnp.arange(0, HEAD_DIM, 2, dtype=F32) / HEAD_DIM)
    ang = jnp.arange(seq, dtype=jnp.int32).astype(F32)[:, None] * inv_freq[None, :]
    tile = lambda t: jnp.concatenate([t, t, t, t], axis=-1)
    cos = tile(jnp.cos(ang))
    sin = tile(jnp.sin(ang))
    first_half = (jnp.arange(LANES) % HEAD_DIM) < HEAD_DIM // 2
    sina = jnp.where(first_half[None, :], -sin, 0.0)
    sinb = jnp.where(first_half[None, :], 0.0, sin)
    return cos, sina, sinb


def kernel(x, w_in, attn_out_g, gmlp_out_g, gmlp_ln_g, gmlp_ln_b, w_spatial, b_spatial, w_out,
           ln1_g, ln1_b, w_gate, w_up, w_down, ln2_g, ln2_b):
    batch, seq, d_model = x.shape
    depth = w_in.shape[0]
    width = attn_out_g.shape[-1]
    alpha = (2 * depth) ** 0.25
    assert w_in.shape[-1] == 5 * width and seq % MOBA_BLOCK == 0 and width % LANES == 0
    assert min(MOBA_TOPK, seq // MOBA_BLOCK - 1) == MOBA_TOPK
    cos, sina, sinb = _rope_tables(seq)
    x2 = x.reshape(batch * seq, d_model)
    for l in range(depth):
        q, k, v, u, z = _in_proj(x2, w_in[l].astype(BF16), cos, sina, sinb,
                                 gmlp_ln_g[l].reshape(1, width), gmlp_ln_b[l].reshape(1, width),
                                 seq=seq, width=width, tm=1024, n_sub=4)
        attn = _attention(q, k, v, batch=batch, seq=seq)
        bias_full = jnp.repeat(b_spatial[l].T, HEAD_DIM, axis=1)
        x1 = _mix_proj(attn, u, z, x2, w_spatial[l], bias_full,
                       attn_out_g[l].reshape(1, width), gmlp_out_g[l].reshape(1, width),
                       w_out[l].astype(BF16), ln1_g[l].reshape(1, d_model), ln1_b[l].reshape(1, d_model),
                       alpha=alpha, tm=1024, n_sub=4)
        x2 = _ffn(x1, w_gate[l].astype(BF16), w_up[l].astype(BF16), w_down[l].astype(BF16),
                  ln2_g[l].reshape(1, d_model), ln2_b[l].reshape(1, d_model),
                  alpha=alpha, tm=1024, hc=256, n_sub=2)
    return x2.reshape(batch, seq, d_model)
```

```python
import functools

import jax
import jax.numpy as jnp
from jax import lax
from jax.experimental import pallas as pl
from jax.experimental.pallas import tpu as pltpu

HEAD_DIM = 64
MOBA_BLOCK = 256
MOBA_TOPK = 3
GMLP_CHUNK = 128
ROPE_THETA = 10000.0
LN_EPS = 1e-5
RMS_EPS = 1e-6
NEG_INF = -1e30
LANES = 128
VT_ROWS = HEAD_DIM + 16

F32 = jnp.float32
BF16 = jnp.bfloat16
_NT = (((1,), (1,)), ((), ()))

VMEM_LIMIT = 48 * 1024 * 1024


def _erf_gelu(t):
    return 0.5 * t * (1.0 + lax.erf(t * 0.7071067811865476))


def _in_proj_kernel(x_ref, w32_ref, cos_ref, sina_ref, sinb_ref, lng_ref, lnb_ref, *rest,
                    width, n_sub, n_cast):
    cast_in, rest = rest[:n_cast], rest[n_cast:]
    (q_ref, k_ref, v_ref, u_ref, z_ref), rest = rest[:5], rest[5:]
    cast_out, (w_ref,) = rest[:n_cast], rest[n_cast:]

    @pl.when(pl.program_id(0) == 0)
    def _():
        w_ref[...] = w32_ref[...].astype(BF16)

    for src, dst in zip(cast_in, cast_out):
        dst[...] = src[...].astype(BF16)

    lane = lax.broadcasted_iota(jnp.int32, (1, LANES), 1)
    lo = lane < HEAD_DIM
    sm = x_ref.shape[0] // n_sub
    n_seg = w_ref.shape[1] // width
    row_slices = [slice(s * sm, (s + 1) * sm) for s in range(n_sub)]

    def rope(t, rows):
        return (t * cos_ref[rows, :] + pltpu.roll(t, 96, 1) * sina_ref[rows, :]
                + pltpu.roll(t, 32, 1) * sinb_ref[rows, :])

    def group_norm(zs, sl):
        s_lo = jnp.sum(jnp.where(lo, zs, 0.0), axis=1, keepdims=True)
        s_hi = jnp.sum(jnp.where(lo, 0.0, zs), axis=1, keepdims=True)
        d = zs - jnp.where(lo, s_lo, s_hi) * (1.0 / HEAD_DIM)
        dd = d * d
        v_lo = jnp.sum(jnp.where(lo, dd, 0.0), axis=1, keepdims=True)
        v_hi = jnp.sum(jnp.where(lo, 0.0, dd), axis=1, keepdims=True)
        var = jnp.where(lo, v_lo, v_hi) * (1.0 / HEAD_DIM)
        return d * lax.rsqrt(var + LN_EPS) * lng_ref[:, sl] + lnb_ref[:, sl]

    def epilogue(seg, h, rows):
        if seg == 2:
            v_ref[rows, :] = h.astype(BF16)
        elif seg == 3:
            u_ref[rows, :] = _erf_gelu(h).astype(BF16)
        else:
            if seg == 4:
                h = _erf_gelu(h)
            for c in range(width // LANES):
                sl = slice(c * LANES, (c + 1) * LANES)
                if seg == 0:
                    q_ref[rows, sl] = rope(h[:, sl] * (HEAD_DIM ** -0.5), rows).astype(BF16)
                elif seg == 1:
                    k_ref[rows, sl] = rope(h[:, sl], rows).astype(BF16)
                else:
                    z_ref[rows, sl] = group_norm(h[:, sl], sl).astype(BF16)

    xs = [None] * n_sub
    hs = [[None] * n_seg for _ in range(n_sub)]
    for s in range(n_sub + 1):
        if s < n_sub:
            xs[s] = x_ref[row_slices[s], :].astype(BF16)
        for seg in range(n_seg):
            if s < n_sub:
                hs[s][seg] = jnp.dot(xs[s], w_ref[:, seg * width:(seg + 1) * width],
                                     preferred_element_type=F32)
            if s >= 1:
                epilogue(seg, hs[s - 1][seg], row_slices[s - 1])


def _in_proj(x2, w_in, cos, sina, sinb, lng, lnb, later_weights, *, seq, width, tm, n_sub):
    tokens, d_model = x2.shape
    n_steps = tokens // tm
    n_pos_tiles = seq // tm
    out = jax.ShapeDtypeStruct((tokens, width), BF16)
    tile = pl.BlockSpec((tm, width), lambda t: (t, 0))
    table = pl.BlockSpec((tm, LANES), lambda t: (t % n_pos_tiles, 0))
    row = pl.BlockSpec((1, width), lambda t: (0, 0))
    slabs = []
    for w in later_weights:
        assert w.shape[0] % (16 * n_steps) == 0
        slabs.append(pl.BlockSpec((w.shape[0] // n_steps, w.shape[1]), lambda t: (t, 0)))
    res = pl.pallas_call(
        functools.partial(_in_proj_kernel, width=width, n_sub=n_sub, n_cast=len(later_weights)),
        grid=(n_steps,),
        in_specs=[pl.BlockSpec((tm, d_model), lambda t: (t, 0)),
                  pl.BlockSpec(w_in.shape, lambda t: (0, 0), pipeline_mode=pl.Buffered(1)),
                  table, table, table, row, row, *slabs],
        out_specs=[tile] * 5 + slabs,
        out_shape=[out] * 5 + [jax.ShapeDtypeStruct(w.shape, BF16) for w in later_weights],
        scratch_shapes=[pltpu.VMEM(w_in.shape, BF16)],
        compiler_params=pltpu.CompilerParams(dimension_semantics=("arbitrary",),
                                             vmem_limit_bytes=VMEM_LIMIT),
        name="in_proj",
    )(x2, w_in, cos, sina, sinb, lng, lnb, *later_weights)
    return res[:5], res[5:]


def _aligned(start, multiple):
    return start if isinstance(start, int) else pl.multiple_of(start, multiple)


def _attn_kernel(q_ref, k_ref, v_ref, o_ref, ka_ref, vt_ref, kmh_ref, kml_ref, qa0_ref, qa1_ref,
                 s0_ref, s1_ref, m0_ref, m1_ref, acc0_ref, acc1_ref, *, nb):
    blk = MOBA_BLOCK
    sub = blk // 8
    qa_refs, s_refs, m_refs = (qa0_ref, qa1_ref), (s0_ref, s1_ref), (m0_ref, m1_ref)
    acc_refs = (acc0_ref, acc1_ref)
    assert nb % 4 == 0
    lane = lax.broadcasted_iota(jnp.int32, (1, LANES), 1)
    head_lanes = [(lane >= HEAD_DIM * h) & (lane < HEAD_DIM * (h + 1)) for h in range(2)]
    bias_off = [HEAD_DIM, 0]

    kf = k_ref[...].astype(F32)
    for h in range(2):
        km = jnp.sum(jnp.where(head_lanes[h], kf, 0.0).reshape(nb, blk, LANES), axis=1) * (1.0 / blk)
        hi = km.astype(BF16)
        kmh_ref[h] = hi
        kml_ref[h] = (km - hi.astype(F32)).astype(BF16)
    ones_row = jnp.where(lax.broadcasted_iota(jnp.int32, (VT_ROWS - HEAD_DIM, blk), 0) == 0, 1.0, 0.0)
    for j in range(nb):
        rows = slice(j * blk, (j + 1) * blk)
        kj = k_ref[rows, :]
        v_t = v_ref[rows, :].astype(F32).T
        for h in range(2):
            onehot = jnp.where(lane == bias_off[h] + j, 1.0, 0.0).astype(BF16)
            ka_ref[h, rows, :] = jnp.where(head_lanes[h], kj, onehot)
            vt_ref[j, h] = jnp.concatenate(
                [v_t[HEAD_DIM * h:HEAD_DIM * (h + 1)], ones_row], axis=0).astype(BF16)

    row = lax.broadcasted_iota(jnp.int32, (nb, blk), 0)
    rowf = row.astype(F32)
    kk = lax.broadcasted_iota(jnp.int32, (blk, blk), 0)
    qq = lax.broadcasted_iota(jnp.int32, (blk, blk), 1)

    def prepare_row(slot, rs, r):
        qa_ref = qa_refs[slot]
        q_r = q_ref[pl.ds(_aligned(r * blk, blk), blk), :]
        for h in range(2):
            g = (lax.dot_general(kmh_ref[h], q_r, _NT, preferred_element_type=F32)
                 + lax.dot_general(kml_ref[h], q_r, _NT, preferred_element_type=F32))
            past = row < r
            g = jnp.where(past, g, NEG_INF)
            bias = jnp.where(row == r, 0.0, NEG_INF)
            for _ in range(MOBA_TOPK):
                top = jnp.max(g, axis=0, keepdims=True)
                first = jnp.min(jnp.where(g == top, rowf, float(nb)), axis=0, keepdims=True)
                pick = rowf == first
                bias = jnp.where(pick & past, 0.0, bias)
                g = jnp.where(pick, -jnp.inf, g)
            pieces = [bias, jnp.zeros((LANES - bias_off[h] - nb, blk), F32)]
            if bias_off[h]:
                pieces.insert(0, jnp.zeros((bias_off[h], blk), F32))
            bias_t = jnp.concatenate(pieces, axis=0).T
            qa_ref[rs, h] = jnp.where(head_lanes[h], q_r.astype(F32), bias_t).astype(BF16)

    def score_tile(slot, t, rs, j, causal):
        qa_ref, m_ref, s_ref = qa_refs[slot], m_refs[slot], s_refs[slot]
        k0 = _aligned(j * blk, blk)
        for h in range(2):
            s = lax.dot_general(ka_ref[h, pl.ds(k0, blk), :], qa_ref[rs, h], _NT,
                                preferred_element_type=F32)
            if causal is not None:
                s = s + causal
            s_ref[t, h] = s
            m_ref[rs, h] = jnp.maximum(m_ref[rs, h], jnp.max(s.reshape(sub, 8, blk), axis=0))

    def value_tile(slot, t, rs, j):
        m_ref, s_ref, acc_ref = m_refs[slot], s_refs[slot], acc_refs[slot]
        for h in range(2):
            p = jnp.exp(s_ref[t, h].reshape(sub, 8, blk) - m_ref[rs, h]).reshape(blk, blk)
            acc_ref[rs, h] += jnp.dot(vt_ref[j, h], p.astype(BF16), preferred_element_type=F32)

    def past_tile(u, pi):
        if isinstance(pi, int):
            return (0, u) if u < pi else (1, u - pi)
        first = u < pi
        return jnp.where(first, 0, 1), jnp.where(first, u, u - pi)

    def tile_of(t, pi):
        rows = (pi, nb - 1 - pi)
        return (t, rows[t]) if t < 2 else past_tile(t - 2, pi)

    def write_rows(pair, slot, rs):
        outs = []
        for h in range(2):
            a = acc_refs[slot][rs, h]
            outs.append(a[:HEAD_DIM] / a[HEAD_DIM:HEAD_DIM + 1])
        q0 = _aligned((pair, nb - 1 - pair)[rs] * blk, blk)
        o_ref[pl.ds(q0, blk), :] = jnp.concatenate(outs, axis=0).T.astype(BF16)

    def step(k, slot):
        live = lambda p: not isinstance(p, int) or 0 <= p < nb // 2
        score_pair, value_pair, write_pair = k, k - 1, k - 2
        do_score, do_value, do_write = (live(p) for p in (score_pair, value_pair, write_pair))
        if do_score:
            causal = jnp.where(kk <= qq, 0.0, NEG_INF)
            for rs in range(2):
                prepare_row(slot, rs, (score_pair, nb - 1 - score_pair)[rs])
                for h in range(2):
                    m_refs[slot][rs, h] = jnp.full((8, blk), NEG_INF, F32)
        if do_value:
            for rs in range(2):
                for h in range(2):
                    acc_refs[1 - slot][rs, h] = jnp.zeros((VT_ROWS, blk), F32)
        write_at = {1: 0, nb // 2: 1}
        for t in range(nb + 1):
            if do_score:
                rs, j = tile_of(t, score_pair)
                score_tile(slot, t, rs, j, causal if t < 2 else None)
            if do_value:
                rs, j = tile_of(t, value_pair)
                value_tile(1 - slot, t, rs, j)
            if do_write and t in write_at:
                write_rows(write_pair, slot, write_at[t])
        if do_score:
            m_ref = m_refs[slot]
            for rs in range(2):
                for h in range(2):
                    m_ref[rs, h] = jnp.broadcast_to(jnp.max(m_ref[rs, h], axis=0, keepdims=True),
                                                    (8, blk))

    n_pairs = nb // 2
    step(0, 0)
    step(1, 1)

    def two_steps(i2, _):
        step(2 * i2 + 2, 0)
        step(2 * i2 + 3, 1)
        return 0

    n_rolled = (n_pairs - 2) // 2
    lax.fori_loop(0, n_rolled, two_steps, 0)
    for k in range(2 + 2 * n_rolled, n_pairs + 2):
        step(k, k % 2)


def _attention(q, k, v, *, batch, seq):
    width = q.shape[-1]
    nb = seq // MOBA_BLOCK
    q3, k3, v3 = (t.reshape(batch, seq, width) for t in (q, k, v))
    spec = pl.BlockSpec((None, seq, LANES), lambda b, p: (b, 0, p))
    out = pl.pallas_call(
        functools.partial(_attn_kernel, nb=nb),
        grid=(batch, width // LANES),
        in_specs=[spec, spec, spec],
        out_specs=spec,
        out_shape=jax.ShapeDtypeStruct((batch, seq, width), BF16),
        scratch_shapes=[pltpu.VMEM((2, seq, LANES), BF16),
                        pltpu.VMEM((nb, 2, VT_ROWS, MOBA_BLOCK), BF16),
                        pltpu.VMEM((2, nb, LANES), BF16),
                        pltpu.VMEM((2, nb, LANES), BF16),
                        *[pltpu.VMEM((2, 2, MOBA_BLOCK, LANES), BF16)] * 2,
                        *[pltpu.VMEM((nb + 1, 2, MOBA_BLOCK, MOBA_BLOCK), F32)] * 2,
                        *[pltpu.VMEM((2, 2, 8, MOBA_BLOCK), F32)] * 2,
                        *[pltpu.VMEM((2, 2, VT_ROWS, MOBA_BLOCK), F32)] * 2],
        compiler_params=pltpu.CompilerParams(dimension_semantics=("arbitrary", "arbitrary"),
                                             vmem_limit_bytes=VMEM_LIMIT),
        name="moba_attention",
    )(q3, k3, v3)
    return out.reshape(batch * seq, width)


def _layer_norm(y, g, b):
    mu = jnp.mean(y, axis=-1, keepdims=True)
    d = y - mu
    var = jnp.mean(d * d, axis=-1, keepdims=True)
    return d * lax.rsqrt(var + LN_EPS) * g + b


def _rms_norm(t, g):
    ms = jnp.mean(t * t, axis=-1, keepdims=True)
    return t * lax.rsqrt(ms + RMS_EPS) * g


def _mix_kernel(attn_ref, u_ref, z_ref, x_ref, ws_ref, bs_ref, ag_ref, gg_ref, w_ref, lg_ref, lb_ref,
                o_ref, sg_ref, *, alpha, width, n_sub):
    c = GMLP_CHUNK
    sm = x_ref.shape[0] // n_sub
    n_pairs = ws_ref.shape[0] // 2
    ii = lax.broadcasted_iota(jnp.int32, (c, c), 0)
    jj = lax.broadcasted_iota(jnp.int32, (c, c), 1)
    lane = lax.broadcasted_iota(jnp.int32, (1, LANES), 1)
    lo = lane < HEAD_DIM
    w_cat = [jnp.concatenate([jnp.where(jj <= ii, ws_ref[2 * p + g], 0.0) for g in range(2)],
                             axis=1).astype(BF16) for p in range(n_pairs)]

    def z_stack(rows, sl):
        z = z_ref[rows, sl]
        zero = jnp.zeros_like(z)
        return jnp.concatenate([jnp.where(lo, z, zero), jnp.where(lo, zero, z)], axis=0)

    def gate_and_norm(s):
        for c2 in range(sm // (2 * c)):
            r0 = s * sm + c2 * 2 * c
            rows = [slice(r0, r0 + c), slice(r0 + c, r0 + 2 * c)]
            for p in range(n_pairs):
                sl = slice(p * LANES, (p + 1) * LANES)
                rhs = jnp.concatenate([z_stack(rows[0], sl), z_stack(rows[1], sl)], axis=1)
                mixed = jnp.dot(w_cat[p], rhs, preferred_element_type=F32)
                for k in range(2):
                    sg_ref[rows[k], sl] = (u_ref[rows[k], sl].astype(F32)
                                           * (mixed[:, k * LANES:(k + 1) * LANES] + bs_ref[:, sl]))
        rows = slice(s * sm, (s + 1) * sm)
        an = _rms_norm(attn_ref[rows, :].astype(F32), ag_ref[...]).astype(BF16)
        sn = _rms_norm(sg_ref[rows, :], gg_ref[...]).astype(BF16)
        return an, sn

    def project(normed, cols):
        an, sn = normed
        return (jnp.dot(an, w_ref[:width, cols], preferred_element_type=F32)
                + jnp.dot(sn, w_ref[width:, cols], preferred_element_type=F32))

    def residual_norm(s, mix):
        rows = slice(s * sm, (s + 1) * sm)
        o_ref[rows, :] = _layer_norm(alpha * x_ref[rows, :] + mix, lg_ref[...], lb_ref[...])

    d_model = w_ref.shape[1]
    col_halves = [slice(0, d_model // 2), slice(d_model // 2, d_model)]
    normed = [None] * n_sub
    mixes = [None] * n_sub
    for s in range(n_sub + 2):
        if s < n_sub:
            normed[s] = gate_and_norm(s)
        if 1 <= s <= n_sub:
            left = project(normed[s - 1], col_halves[0])
        if 2 <= s:
            residual_norm(s - 2, mixes[s - 2])
        if 1 <= s <= n_sub:
            mixes[s - 1] = jnp.concatenate([left, project(normed[s - 1], col_halves[1])], axis=1)


def _mix_proj(attn, u, z, x2, w_spatial, bias_full, ag, gg, w_out, lg, lb, *, alpha, tm, n_sub):
    tokens, width = attn.shape
    d_model = x2.shape[1]
    assert (tm // n_sub) % (2 * GMLP_CHUNK) == 0
    half = pl.BlockSpec((tm, width), lambda t: (t, 0))
    full = pl.BlockSpec((tm, d_model), lambda t: (t, 0))
    hrow = pl.BlockSpec((1, width), lambda t: (0, 0))
    frow = pl.BlockSpec((1, d_model), lambda t: (0, 0))
    return pl.pallas_call(
        functools.partial(_mix_kernel, alpha=alpha, width=width, n_sub=n_sub),
        grid=(tokens // tm,),
        in_specs=[half, half, half, full,
                  pl.BlockSpec(w_spatial.shape, lambda t: (0, 0, 0)),
                  pl.BlockSpec(bias_full.shape, lambda t: (0, 0)),
                  hrow, hrow, pl.BlockSpec(w_out.shape, lambda t: (0, 0)), frow, frow],
        out_specs=full,
        out_shape=jax.ShapeDtypeStruct((tokens, d_model), F32),
        scratch_shapes=[pltpu.VMEM((tm, width), F32)],
        compiler_params=pltpu.CompilerParams(dimension_semantics=("arbitrary",),
                                             vmem_limit_bytes=VMEM_LIMIT),
        name="gmlp_out_proj_ln",
    )(attn, u, z, x2, w_spatial, bias_full, ag, gg, w_out, lg, lb)


def _ffn_kernel(x_ref, wg_ref, wu_ref, wd_ref, lg_ref, lb_ref, o_ref, acc_ref, *, alpha, hc, n_sub):
    hidden = wg_ref.shape[1]
    sm = x_ref.shape[0] // n_sub

    def residual_norm(s):
        rows = slice(s * sm, (s + 1) * sm)
        o_ref[rows, :] = _layer_norm(alpha * x_ref[rows, :] + acc_ref[rows, :], lg_ref[...], lb_ref[...])

    for s in range(n_sub):
        rows = slice(s * sm, (s + 1) * sm)
        xb = x_ref[rows, :].astype(BF16)
        for c in range(hidden // hc):
            sl = slice(c * hc, (c + 1) * hc)
            g = jnp.dot(xb, wg_ref[:, sl], preferred_element_type=F32)
            u = jnp.dot(xb, wu_ref[:, sl], preferred_element_type=F32)
            h = (g / (1.0 + jnp.exp(-g)) * u).astype(BF16)
            part = jnp.dot(h, wd_ref[sl, :], preferred_element_type=F32)
            if c == 0:
                acc_ref[rows, :] = part
            else:
                acc_ref[rows, :] += part
            if c == 1 and s >= 1:
                residual_norm(s - 1)
    residual_norm(n_sub - 1)


def _ffn(x1, w_gate, w_up, w_down, lg, lb, *, alpha, tm, hc, n_sub):
    tokens, d_model = x1.shape
    full = pl.BlockSpec((tm, d_model), lambda t: (t, 0))
    frow = pl.BlockSpec((1, d_model), lambda t: (0, 0))
    const = lambda a: pl.BlockSpec(a.shape, lambda t: (0, 0), pipeline_mode=pl.Buffered(1))
    return pl.pallas_call(
        functools.partial(_ffn_kernel, alpha=alpha, hc=hc, n_sub=n_sub),
        grid=(tokens // tm,),
        in_specs=[full, const(w_gate), const(w_up), const(w_down), frow, frow],
        out_specs=full,
        out_shape=jax.ShapeDtypeStruct((tokens, d_model), F32),
        scratch_shapes=[pltpu.VMEM((tm, d_model), F32)],
        compiler_params=pltpu.CompilerParams(dimension_semantics=("arbitrary",),
                                             vmem_limit_bytes=VMEM_LIMIT),
        name="swiglu_ffn_ln",
    )(x1, w_gate, w_up, w_down, lg, lb)


def _rope_tables(seq):
    inv_freq = ROPE_THETA ** (-jnp.arange(0, HEAD_DIM, 2, dtype=F32) / HEAD_DIM)
    ang = jnp.arange(seq, dtype=jnp.int32).astype(F32)[:, None] * inv_freq[None, :]
    tile = lambda t: jnp.concatenate([t, t, t, t], axis=-1)
    cos = tile(jnp.cos(ang))
    sin = tile(jnp.sin(ang))
    first_half = (jnp.arange(LANES) % HEAD_DIM) < HEAD_DIM // 2
    sina = jnp.where(first_half[None, :], -sin, 0.0)
    sinb = jnp.where(first_half[None, :], 0.0, sin)
    return cos, sina, sinb


def kernel(x, w_in, attn_out_g, gmlp_out_g, gmlp_ln_g, gmlp_ln_b, w_spatial, b_spatial, w_out,
           ln1_g, ln1_b, w_gate, w_up, w_down, ln2_g, ln2_b):
    batch, seq, d_model = x.shape
    depth = w_in.shape[0]
    width = attn_out_g.shape[-1]
    alpha = (2 * depth) ** 0.25
    assert w_in.shape[-1] == 5 * width and seq % MOBA_BLOCK == 0 and width % LANES == 0
    assert min(MOBA_TOPK, seq // MOBA_BLOCK - 1) == MOBA_TOPK
    cos, sina, sinb = _rope_tables(seq)
    x2 = x.reshape(batch * seq, d_model)
    for l in range(depth):
        (q, k, v, u, z), (w_out_b, w_gate_b, w_up_b, w_down_b) = _in_proj(
            x2, w_in[l], cos, sina, sinb,
            gmlp_ln_g[l].reshape(1, width), gmlp_ln_b[l].reshape(1, width),
            (w_out[l], w_gate[l], w_up[l], w_down[l]), seq=seq, width=width, tm=1024, n_sub=4)
        attn = _attention(q, k, v, batch=batch, seq=seq)
        bias_full = jnp.repeat(b_spatial[l].T, HEAD_DIM, axis=1)
        x1 = _mix_proj(attn, u, z, x2, w_spatial[l], bias_full,
                       attn_out_g[l].reshape(1, width), gmlp_out_g[l].reshape(1, width),
                       w_out_b, ln1_g[l].reshape(1, d_model), ln1_b[l].reshape(1, d_model),
                       alpha=alpha, tm=1024, n_sub=4)
        x2 = _ffn(x1, w_gate_b, w_up_b, w_down_b,
                  ln2_g[l].reshape(1, d_model), ln2_b[l].reshape(1, d_model),
                  alpha=alpha, tm=1024, hc=256, n_sub=2)
    return x2.reshape(batch, seq, d_model)
```

```python
import functools

import jax
import jax.numpy as jnp
from jax import lax
from jax.experimental import pallas as pl
from jax.experimental.pallas import tpu as pltpu

HEAD_DIM = 64
MOBA_BLOCK = 256
MOBA_TOPK = 3
GMLP_CHUNK = 128
ROPE_THETA = 10000.0
ROPE_SPLIT = 64
LN_EPS = 1e-5
RMS_EPS = 1e-6
NEG_INF = -1e30
LANES = 128
ACC_ROWS = HEAD_DIM + 16
F32 = jnp.float32
BF16 = jnp.bfloat16

IN_PROJ_TILE, IN_PROJ_SUBTILES = 1024, 8
POST_TILE, POST_SUBTILES = 512, 2
FFN_CHUNK = 256
VMEM_LIMIT = 48 * 1024 * 1024
POST_VMEM_LIMIT = 56 * 1024 * 1024


def _erf_gelu(t):
    return 0.5 * t * (1.0 + lax.erf(t * 0.7071067811865476))


def _in_proj_kernel(x_ref, w32_ref, rope_hi_ref, rope_lo_ref, lng_ref, lnb_ref, *rest,
                    width, n_sub, n_cast):
    cast_in, rest = rest[:n_cast], rest[n_cast:]
    (q_ref, k_ref, v_ref, u_ref, z_ref), rest = rest[:5], rest[5:]
    cast_out, (w_ref,) = rest[:n_cast], rest[n_cast:]

    @pl.when(pl.program_id(0) == 0)
    def _():
        w_ref[...] = w32_ref[...].astype(BF16)

    lane = lax.broadcasted_iota(jnp.int32, (1, LANES), 1)
    lo = lane < HEAD_DIM
    sm = x_ref.shape[0] // n_sub
    n_seg = w_ref.shape[1] // width
    row_slices = [slice(s * sm, (s + 1) * sm) for s in range(n_sub)]

    first_half = (lane % HEAD_DIM) < HEAD_DIM // 2
    tables = {}

    def rope_tables(rows):
        if rows.start not in tables:
            cos_b, sin_b, cos_b_signed, sin_b_signed = (rope_lo_ref[i] for i in range(4))
            cos_rows, sin_rows = [], []
            for a in range(rows.start // ROPE_SPLIT, rows.stop // ROPE_SPLIT):
                cos_a, sin_a = rope_hi_ref[0, a:a + 1, :], rope_hi_ref[1, a:a + 1, :]
                cos_rows.append(cos_a * cos_b - sin_a * sin_b)
                sin_rows.append(sin_a * cos_b_signed + cos_a * sin_b_signed)
            tables[rows.start] = (jnp.concatenate(cos_rows, axis=0), jnp.concatenate(sin_rows, axis=0))
        return tables[rows.start]

    def rope(t, rows):
        cos, sin_signed = rope_tables(rows)
        rotated = jnp.where(first_half, pltpu.roll(t, 96, 1), pltpu.roll(t, 32, 1))
        return t * cos + rotated * sin_signed

    def group_norm(zs, sl):
        s_lo = jnp.sum(jnp.where(lo, zs, 0.0), axis=1, keepdims=True)
        s_hi = jnp.sum(jnp.where(lo, 0.0, zs), axis=1, keepdims=True)
        d = zs - jnp.where(lo, s_lo, s_hi) * (1.0 / HEAD_DIM)
        dd = d * d
        v_lo = jnp.sum(jnp.where(lo, dd, 0.0), axis=1, keepdims=True)
        v_hi = jnp.sum(jnp.where(lo, 0.0, dd), axis=1, keepdims=True)
        var = jnp.where(lo, v_lo, v_hi) * (1.0 / HEAD_DIM)
        return d * lax.rsqrt(var + LN_EPS) * lng_ref[:, sl] + lnb_ref[:, sl]

    def epilogue(seg, h, rows):
        if seg == 2:
            v_ref[rows, :] = h.astype(BF16)
        elif seg == 3:
            u_ref[rows, :] = _erf_gelu(h).astype(BF16)
        else:
            if seg == 4:
                h = _erf_gelu(h)
            for c in range(width // LANES):
                sl = slice(c * LANES, (c + 1) * LANES)
                if seg == 0:
                    q_ref[rows, sl] = rope(h[:, sl] * (HEAD_DIM ** -0.5), rows).astype(BF16)
                elif seg == 1:
                    k_ref[rows, sl] = rope(h[:, sl], rows).astype(BF16)
                else:
                    z_ref[rows, sl] = group_norm(h[:, sl], sl).astype(BF16)

    xs = [None] * n_sub
    hs = [[None] * n_seg for _ in range(n_sub)]
    for s in range(n_sub + 1):
        if s < n_sub:
            xs[s] = x_ref[row_slices[s], :].astype(BF16)
        for seg in range(n_seg):
            if s < n_sub:
                hs[s][seg] = jnp.dot(xs[s], w_ref[:, seg * width:(seg + 1) * width],
                                     preferred_element_type=F32)
            if s >= 1:
                epilogue(seg, hs[s - 1][seg], row_slices[s - 1])
        for src, dst in list(zip(cast_in, cast_out))[s::n_sub + 1]:
            dst[...] = src[...].astype(BF16)


def _in_proj(x2, w_in, rope_hi, rope_lo, lng, lnb, later_weights, *, seq, width, tm, n_sub):
    tokens, d_model = x2.shape
    n_steps = tokens // tm
    n_pos_tiles = seq // tm
    assert (tm // n_sub) % ROPE_SPLIT == 0 and (tm // ROPE_SPLIT) % 8 == 0
    out = jax.ShapeDtypeStruct((tokens, width), BF16)
    tile = pl.BlockSpec((tm, width), lambda t: (t, 0))
    hi_spec = pl.BlockSpec((2, tm // ROPE_SPLIT, LANES), lambda t: (0, t % n_pos_tiles, 0))
    lo_spec = pl.BlockSpec(rope_lo.shape, lambda t: (0, 0, 0))
    row = pl.BlockSpec((1, width), lambda t: (0, 0))
    slabs = []
    for w in later_weights:
        assert w.shape[0] % (16 * n_steps) == 0
        slabs.append(pl.BlockSpec((w.shape[0] // n_steps, w.shape[1]), lambda t: (t, 0)))
    res = pl.pallas_call(
        functools.partial(_in_proj_kernel, width=width, n_sub=n_sub, n_cast=len(later_weights)),
        grid=(n_steps,),
        in_specs=[pl.BlockSpec((tm, d_model), lambda t: (t, 0)),
                  pl.BlockSpec(w_in.shape, lambda t: (0, 0), pipeline_mode=pl.Buffered(1)),
                  hi_spec, lo_spec, row, row, *slabs],
        out_specs=[tile] * 5 + slabs,
        out_shape=[out] * 5 + [jax.ShapeDtypeStruct(w.shape, BF16) for w in later_weights],
        scratch_shapes=[pltpu.VMEM(w_in.shape, BF16)],
        compiler_params=pltpu.CompilerParams(dimension_semantics=("arbitrary",),
                                             vmem_limit_bytes=VMEM_LIMIT),
        name="in_proj",
    )(x2, w_in, rope_hi, rope_lo, lng, lnb, *later_weights)
    return res[:5], res[5:]


def _aligned(start, multiple):
    return start if isinstance(start, int) else pl.multiple_of(start, multiple)


def _attn_kernel(q_ref, k_ref, v_ref, o_ref, ka_ref, vt_ref, km_ref, qa_ref,
                 s0_ref, s1_ref, m0_ref, m1_ref, acc0_ref, acc1_ref, *, nb):
    blk = MOBA_BLOCK
    sub = blk // 8
    s_refs, m_refs = (s0_ref, s1_ref), (m0_ref, m1_ref)
    acc_refs = (acc0_ref, acc1_ref)
    assert nb % 4 == 0
    lane = lax.broadcasted_iota(jnp.int32, (1, LANES), 1)
    head_lanes = [(lane >= HEAD_DIM * h) & (lane < HEAD_DIM * (h + 1)) for h in range(2)]
    bias_off = [HEAD_DIM, 0]

    km_pair = jnp.sum(k_ref[...].astype(F32).reshape(nb, blk, LANES), axis=1) * (1.0 / blk)
    for h in range(2):
        km = jnp.where(head_lanes[h], km_pair, 0.0)
        hi = km.astype(BF16)
        km_ref[(2 * h) * nb:(2 * h + 1) * nb, :] = hi
        km_ref[(2 * h + 1) * nb:(2 * h + 2) * nb, :] = (km - hi.astype(F32)).astype(BF16)
    for j in range(nb):
        rows = slice(j * blk, (j + 1) * blk)
        kj = k_ref[rows, :]
        for h in range(2):
            onehot = jnp.where(lane == bias_off[h] + j, 1.0, 0.0).astype(BF16)
            ka_ref[h, rows, :] = jnp.where(head_lanes[h], kj, onehot)

    def build_values(j):
        ones_row = jnp.where(lax.broadcasted_iota(jnp.int32, (ACC_ROWS - HEAD_DIM, blk), 0) == 0, 1.0, 0.0)
        v_t = v_ref[j * blk:(j + 1) * blk, :].astype(F32).T
        for h in range(2):
            vt_ref[j, h] = jnp.concatenate(
                [v_t[HEAD_DIM * h:HEAD_DIM * (h + 1)], ones_row], axis=0).astype(BF16)

    row = lax.broadcasted_iota(jnp.int32, (nb, blk), 0)
    rowf = row.astype(F32)
    kk = lax.broadcasted_iota(jnp.int32, (blk, blk), 0)
    qq = lax.broadcasted_iota(jnp.int32, (blk, blk), 1)

    def select_blocks(r):
        q_t = q_ref[r * blk:(r + 1) * blk, :].astype(F32).T.astype(BF16)
        g_all = jnp.dot(km_ref[...], q_t, preferred_element_type=F32)
        for h in range(2):
            g = g_all[(2 * h) * nb:(2 * h + 1) * nb] + g_all[(2 * h + 1) * nb:(2 * h + 2) * nb]
            past = row < r
            g = jnp.where(past, g, NEG_INF)
            bias = jnp.where(row == r, 0.0, NEG_INF)
            for _ in range(MOBA_TOPK):
                top = jnp.max(g, axis=0, keepdims=True)
                first = jnp.min(jnp.where(g == top, rowf, float(nb)), axis=0, keepdims=True)
                pick = rowf == first
                bias = jnp.where(pick & past, 0.0, bias)
                g = jnp.where(pick, -jnp.inf, g)
            q_h = q_t[HEAD_DIM * h:HEAD_DIM * (h + 1)]
            bias = bias.astype(BF16)
            pad = jnp.zeros((HEAD_DIM - nb, blk), BF16)
            pieces = [q_h, bias, pad] if h == 0 else [bias, pad, q_h]
            qa_ref[r, h] = jnp.concatenate(pieces, axis=0)

    def score_tile(slot, t, r, rs, j, causal):
        m_ref, s_ref = m_refs[slot], s_refs[slot]
        k0 = _aligned(j * blk, blk)
        for h in range(2):
            s = jnp.dot(ka_ref[h, pl.ds(k0, blk), :], qa_ref[r, h],
                        preferred_element_type=F32)
            if causal is not None:
                s = s + causal
            s_ref[t, h] = s
            m_ref[rs, h] = jnp.maximum(m_ref[rs, h], jnp.max(s.reshape(sub, 8, blk), axis=0))

    def value_tile(slot, t, rs, j):
        m_ref, s_ref, acc_ref = m_refs[slot], s_refs[slot], acc_refs[slot]
        for h in range(2):
            p = jnp.exp(s_ref[t, h].reshape(sub, 8, blk) - m_ref[rs, h]).reshape(blk, blk)
            acc_ref[rs, h] += jnp.dot(vt_ref[j, h], p.astype(BF16), preferred_element_type=F32)

    def past_tile(u, pi):
        if isinstance(pi, int):
            return (0, u) if u < pi else (1, u - pi)
        first = u < pi
        return jnp.where(first, 0, 1), jnp.where(first, u, u - pi)

    def tile_of(t, pi):
        rows = (pi, nb - 1 - pi)
        return (t, rows[t]) if t < 2 else past_tile(t - 2, pi)

    def write_rows(pair, slot, rs):
        outs = []
        for h in range(2):
            a = acc_refs[slot][rs, h]
            outs.append(a[:HEAD_DIM] / a[HEAD_DIM:HEAD_DIM + 1])
        q0 = _aligned((pair, nb - 1 - pair)[rs] * blk, blk)
        o_ref[pl.ds(q0, blk), :] = jnp.concatenate(outs, axis=0).T.astype(BF16)

    def step(k, slot):
        live = lambda p: not isinstance(p, int) or 0 <= p < nb // 2
        score_pair, value_pair, write_pair = k, k - 1, k - 2
        do_score, do_value, do_write = (live(p) for p in (score_pair, value_pair, write_pair))
        if do_score:
            causal = jnp.where(kk <= qq, 0.0, NEG_INF)
            for rs in range(2):
                for h in range(2):
                    m_refs[slot][rs, h] = jnp.full((8, blk), NEG_INF, F32)
        if do_value:
            for rs in range(2):
                for h in range(2):
                    acc_refs[1 - slot][rs, h] = jnp.zeros((ACC_ROWS, blk), F32)
        write_at = {1: 0, nb // 2: 1}
        for t in range(nb + 1):
            if do_score:
                rs, j = tile_of(t, score_pair)
                r = score_pair + rs * (nb - 1 - 2 * score_pair)
                score_tile(slot, t, r, rs, j, causal if t < 2 else None)
            if do_value:
                rs, j = tile_of(t, value_pair)
                value_tile(1 - slot, t, rs, j)
            if do_write and t in write_at:
                write_rows(write_pair, slot, write_at[t])
        if do_score:
            m_ref = m_refs[slot]
            for rs in range(2):
                for h in range(2):
                    m_ref[rs, h] = jnp.broadcast_to(jnp.max(m_ref[rs, h], axis=0, keepdims=True),
                                                    (8, blk))

    n_pairs = nb // 2
    for j in range(nb):
        build_values(j)
        select_blocks(j)
    step(0, 0)
    step(1, 1)

    def two_steps(i2, _):
        step(2 * i2 + 2, 0)
        step(2 * i2 + 3, 1)
        return 0

    n_rolled = (n_pairs - 2) // 2
    lax.fori_loop(0, n_rolled, two_steps, 0)
    for k in range(2 + 2 * n_rolled, n_pairs + 2):
        step(k, k % 2)


def _attention(q, k, v, *, batch, seq):
    width = q.shape[-1]
    nb = seq // MOBA_BLOCK
    q3, k3, v3 = (t.reshape(batch, seq, width) for t in (q, k, v))
    spec = pl.BlockSpec((None, seq, LANES), lambda b, p: (b, 0, p))
    out = pl.pallas_call(
        functools.partial(_attn_kernel, nb=nb),
        grid=(batch, width // LANES),
        in_specs=[spec, spec, spec],
        out_specs=spec,
        out_shape=jax.ShapeDtypeStruct((batch, seq, width), BF16),
        scratch_shapes=[pltpu.VMEM((2, seq, LANES), BF16),
                        pltpu.VMEM((nb, 2, ACC_ROWS, MOBA_BLOCK), BF16),
                        pltpu.VMEM((4 * nb, LANES), BF16),
                        pltpu.VMEM((nb, 2, LANES, MOBA_BLOCK), BF16),
                        *[pltpu.VMEM((nb + 1, 2, MOBA_BLOCK, MOBA_BLOCK), F32)] * 2,
                        *[pltpu.VMEM((2, 2, 8, MOBA_BLOCK), F32)] * 2,
                        *[pltpu.VMEM((2, 2, ACC_ROWS, MOBA_BLOCK), F32)] * 2],
        compiler_params=pltpu.CompilerParams(dimension_semantics=("arbitrary", "arbitrary"),
                                             vmem_limit_bytes=VMEM_LIMIT),
        name="moba_attention",
    )(q3, k3, v3)
    return out.reshape(batch * seq, width)


def _layer_norm(y, g, b):
    mu = jnp.mean(y, axis=-1, keepdims=True)
    d = y - mu
    var = jnp.mean(d * d, axis=-1, keepdims=True)
    return d * lax.rsqrt(var + LN_EPS) * g + b


def _rms_norm(t, g):
    ms = jnp.mean(t * t, axis=-1, keepdims=True)
    return t * lax.rsqrt(ms + RMS_EPS) * g


def _mix_pieces(attn_ref, u_ref, z_ref, x_ref, ws_ref, bs_ref, ag_ref, gg_ref, w_ref, lg_ref, lb_ref,
                sg_ref, store, *, alpha, width, n_sub):
    c = GMLP_CHUNK
    sm = x_ref.shape[0] // n_sub
    n_pairs = ws_ref.shape[0] // 2
    ii = lax.broadcasted_iota(jnp.int32, (c, c), 0)
    jj = lax.broadcasted_iota(jnp.int32, (c, c), 1)
    lane = lax.broadcasted_iota(jnp.int32, (1, LANES), 1)
    lo = lane < HEAD_DIM
    w_cat = [jnp.concatenate([jnp.where(jj <= ii, ws_ref[2 * p + g], 0.0) for g in range(2)],
                             axis=1).astype(BF16) for p in range(n_pairs)]

    def z_stack(rows, sl):
        z = z_ref[rows, sl]
        zero = jnp.zeros_like(z)
        return jnp.concatenate([jnp.where(lo, z, zero), jnp.where(lo, zero, z)], axis=0)

    def gate_and_norm(s):
        for c2 in range(sm // (2 * c)):
            r0 = s * sm + c2 * 2 * c
            rows = [slice(r0, r0 + c), slice(r0 + c, r0 + 2 * c)]
            for p in range(n_pairs):
                sl = slice(p * LANES, (p + 1) * LANES)
                rhs = jnp.concatenate([z_stack(rows[0], sl), z_stack(rows[1], sl)], axis=1)
                mixed = jnp.dot(w_cat[p], rhs, preferred_element_type=F32)
                for k in range(2):
                    sg_ref[rows[k], sl] = (u_ref[rows[k], sl].astype(F32)
                                           * (mixed[:, k * LANES:(k + 1) * LANES] + bs_ref[:, sl]))
        rows = slice(s * sm, (s + 1) * sm)
        an = _rms_norm(attn_ref[rows, :].astype(F32), ag_ref[...]).astype(BF16)
        sn = _rms_norm(sg_ref[rows, :], gg_ref[...]).astype(BF16)
        return an, sn

    def project(normed, cols):
        an, sn = normed
        return (jnp.dot(an, w_ref[:width, cols], preferred_element_type=F32)
                + jnp.dot(sn, w_ref[width:, cols], preferred_element_type=F32))

    def residual_norm(s, mix):
        rows = slice(s * sm, (s + 1) * sm)
        store(rows, _layer_norm(alpha * x_ref[rows, :] + mix, lg_ref[...], lb_ref[...]))

    d_model = w_ref.shape[1]
    col_halves = [slice(0, d_model // 2), slice(d_model // 2, d_model)]
    normed, lefts, mixes = {}, {}, {}

    def do_norm(s):
        normed[s] = gate_and_norm(s)

    def do_left(s):
        lefts[s] = project(normed[s], col_halves[0])

    def do_right(s):
        mixes[s] = jnp.concatenate([lefts.pop(s), project(normed.pop(s), col_halves[1])], axis=1)

    def do_out(s):
        residual_norm(s, mixes.pop(s))

    pieces = []
    for s in range(n_sub + 2):
        if s < n_sub:
            pieces.append(functools.partial(do_norm, s))
        if 1 <= s <= n_sub:
            pieces.append(functools.partial(do_left, s - 1))
        if 2 <= s:
            pieces.append(functools.partial(do_out, s - 2))
        if 1 <= s <= n_sub:
            pieces.append(functools.partial(do_right, s - 1))
    return pieces


def _post_kernel(attn_ref, u_ref, z_ref, x_ref, attn0_ref, u0_ref, z0_ref, x0_ref, ws_ref, bs_ref,
                 ag_ref, gg_ref, wo_ref, l1g_ref, l1b_ref, wg_ref, wu_ref, wd_ref, l2g_ref, l2b_ref,
                 o_ref, x1_ref, sg_ref, acc_ref, *, alpha, width, hc, n_sub):
    t = pl.program_id(0)
    slot = t % 2
    mix_refs = (ws_ref, bs_ref, ag_ref, gg_ref, wo_ref, l1g_ref, l1b_ref, sg_ref)
    mix_kw = dict(alpha=alpha, width=width, n_sub=n_sub)

    @pl.when(t == 0)
    def _():
        def store_first(rows, value):
            x1_ref[0, rows, :] = value
        for piece in _mix_pieces(attn0_ref, u0_ref, z0_ref, x0_ref, *mix_refs, store_first, **mix_kw):
            piece()

    def store_next(rows, value):
        x1_ref[1 - slot, rows, :] = value

    pieces = _mix_pieces(attn_ref, u_ref, z_ref, x_ref, *mix_refs, store_next, **mix_kw)
    hidden = wg_ref.shape[1]
    n_chunks = hidden // hc
    assert hidden % hc == 0 and len(pieces) <= n_chunks
    xb = x1_ref[slot].astype(BF16)
    for c in range(n_chunks):
        sl = slice(c * hc, (c + 1) * hc)
        g = jnp.dot(xb, wg_ref[:, sl], preferred_element_type=F32)
        u = jnp.dot(xb, wu_ref[:, sl], preferred_element_type=F32)
        h = (g / (1.0 + jnp.exp(-g)) * u).astype(BF16)
        part = jnp.dot(h, wd_ref[sl, :], preferred_element_type=F32)
        if c == 0:
            acc_ref[...] = part
        else:
            acc_ref[...] += part
        if c < len(pieces):
            pieces[c]()
    o_ref[...] = _layer_norm(alpha * x1_ref[slot] + acc_ref[...], l2g_ref[...], l2b_ref[...])


def _post_attention(attn, u, z, x2, w_spatial, bias_full, ag, gg, w_out, l1g, l1b,
                    w_gate, w_up, w_down, l2g, l2b, *, alpha, tm, hc, n_sub):
    tokens, width = attn.shape
    d_model = x2.shape[1]
    n_tiles = tokens // tm
    assert (tm // n_sub) % (2 * GMLP_CHUNK) == 0
    nxt = lambda t: (jnp.minimum(t + 1, n_tiles - 1), 0)
    once = dict(pipeline_mode=pl.Buffered(1))
    half_n = pl.BlockSpec((tm, width), nxt)
    full_n = pl.BlockSpec((tm, d_model), nxt)
    half_0 = pl.BlockSpec((tm, width), lambda t: (0, 0), **once)
    full_0 = pl.BlockSpec((tm, d_model), lambda t: (0, 0), **once)
    const = lambda a: pl.BlockSpec(a.shape, lambda t: (0,) * a.ndim, **once)
    return pl.pallas_call(
        functools.partial(_post_kernel, alpha=alpha, width=width, hc=hc, n_sub=n_sub),
        grid=(n_tiles,),
        in_specs=[half_n, half_n, half_n, full_n, half_0, half_0, half_0, full_0,
                  const(w_spatial), const(bias_full), const(ag), const(gg), const(w_out),
                  const(l1g), const(l1b), const(w_gate), const(w_up), const(w_down),
                  const(l2g), const(l2b)],
        out_specs=pl.BlockSpec((tm, d_model), lambda t: (t, 0)),
        out_shape=jax.ShapeDtypeStruct((tokens, d_model), F32),
        scratch_shapes=[pltpu.VMEM((2, tm, d_model), F32),
                        pltpu.VMEM((tm, width), F32),
                        pltpu.VMEM((tm, d_model), F32)],
        compiler_params=pltpu.CompilerParams(dimension_semantics=("arbitrary",),
                                             vmem_limit_bytes=POST_VMEM_LIMIT),
        name="mixer_out_ffn",
    )(attn, u, z, x2, attn, u, z, x2, w_spatial, bias_full, ag, gg, w_out, l1g, l1b,
      w_gate, w_up, w_down, l2g, l2b)


def _rope_tables(seq):
    assert seq % ROPE_SPLIT == 0
    inv_freq = ROPE_THETA ** (-jnp.arange(0, HEAD_DIM, 2, dtype=F32) / HEAD_DIM)
    tile = lambda t: jnp.concatenate([t, t, t, t], axis=-1)

    def cos_sin(pos):
        ang = pos.astype(F32)[:, None] * inv_freq[None, :]
        return tile(jnp.cos(ang)), tile(jnp.sin(ang))

    cos_hi, sin_hi = cos_sin(jnp.arange(0, seq, ROPE_SPLIT, dtype=jnp.int32))
    cos_lo, sin_lo = cos_sin(jnp.arange(ROPE_SPLIT, dtype=jnp.int32))
    sign = jnp.where((jnp.arange(LANES) % HEAD_DIM) < HEAD_DIM // 2, -1.0, 1.0)[None, :]
    return jnp.stack([cos_hi, sin_hi]), jnp.stack([cos_lo, sin_lo, sign * cos_lo, sign * sin_lo])


def kernel(x, w_in, attn_out_g, gmlp_out_g, gmlp_ln_g, gmlp_ln_b, w_spatial, b_spatial, w_out,
           ln1_g, ln1_b, w_gate, w_up, w_down, ln2_g, ln2_b):
    batch, seq, d_model = x.shape
    depth = w_in.shape[0]
    width = attn_out_g.shape[-1]
    alpha = (2 * depth) ** 0.25
    assert w_in.shape[-1] == 5 * width and seq % MOBA_BLOCK == 0 and width % LANES == 0
    assert min(MOBA_TOPK, seq // MOBA_BLOCK - 1) == MOBA_TOPK
    rope_hi, rope_lo = _rope_tables(seq)
    x2 = x.reshape(batch * seq, d_model)
    for l in range(depth):
        (q, k, v, u, z), (w_out_b, w_gate_b, w_up_b, w_down_b) = _in_proj(
            x2, w_in[l], rope_hi, rope_lo,
            gmlp_ln_g[l].reshape(1, width), gmlp_ln_b[l].reshape(1, width),
            (w_out[l], w_gate[l], w_up[l], w_down[l]), seq=seq, width=width, tm=IN_PROJ_TILE, n_sub=IN_PROJ_SUBTILES)
        attn = _attention(q, k, v, batch=batch, seq=seq)
        bias_full = jnp.repeat(b_spatial[l].T, HEAD_DIM, axis=1)
        x2 = _post_attention(attn, u, z, x2, w_spatial[l], bias_full,
                             attn_out_g[l].reshape(1, width), gmlp_out_g[l].reshape(1, width),
                             w_out_b, ln1_g[l].reshape(1, d_model), ln1_b[l].reshape(1, d_model),
                             w_gate_b, w_up_b, w_down_b,
                             ln2_g[l].reshape(1, d_model), ln2_b[l].reshape(1, d_model),
                             alpha=alpha, tm=POST_TILE, hc=FFN_CHUNK, n_sub=POST_SUBTILES)
    return x2.reshape(batch, seq, d_model)
```

```python
import functools

import jax
import jax.numpy as jnp
from jax import lax
from jax.experimental import pallas as pl
from jax.experimental.pallas import tpu as pltpu

HEAD_DIM = 64
MOBA_BLOCK = 256
MOBA_TOPK = 3
GMLP_CHUNK = 128
ROPE_THETA = 10000.0
ROPE_SPLIT = 64
LN_EPS = 1e-5
RMS_EPS = 1e-6
NEG_INF = -1e30
LANES = 128
ACC_ROWS = HEAD_DIM + 16
F32 = jnp.float32
BF16 = jnp.bfloat16

IN_PROJ_TILE, IN_PROJ_SUBTILES = 1024, 8
POST_TILE, POST_SUBTILES = 512, 2
FFN_CHUNK = 256
ATTN_HEAD_PAIRS = 2
VMEM_LIMIT = 48 * 1024 * 1024
ATTN_VMEM_LIMIT = 52 * 1024 * 1024
POST_VMEM_LIMIT = 56 * 1024 * 1024


def _erf_gelu(t):
    return 0.5 * t * (1.0 + lax.erf(t * 0.7071067811865476))


def _in_proj_kernel(x_ref, w32_ref, rope_hi_ref, rope_lo_ref, lng_ref, lnb_ref, *rest,
                    width, n_sub, n_cast):
    cast_in, rest = rest[:n_cast], rest[n_cast:]
    (q_ref, k_ref, v_ref, u_ref, z_ref), rest = rest[:5], rest[5:]
    cast_out, (w_ref,) = rest[:n_cast], rest[n_cast:]

    @pl.when(pl.program_id(0) == 0)
    def _():
        w_ref[...] = w32_ref[...].astype(BF16)

    lane = lax.broadcasted_iota(jnp.int32, (1, LANES), 1)
    lo = lane < HEAD_DIM
    sm = x_ref.shape[0] // n_sub
    n_seg = w_ref.shape[1] // width
    row_slices = [slice(s * sm, (s + 1) * sm) for s in range(n_sub)]

    first_half = (lane % HEAD_DIM) < HEAD_DIM // 2
    tables = {}

    def rope_tables(rows):
        if rows.start not in tables:
            cos_b, sin_b, cos_b_signed, sin_b_signed = (rope_lo_ref[i] for i in range(4))
            cos_rows, sin_rows = [], []
            for a in range(rows.start // ROPE_SPLIT, rows.stop // ROPE_SPLIT):
                cos_a, sin_a = rope_hi_ref[0, a:a + 1, :], rope_hi_ref[1, a:a + 1, :]
                cos_rows.append(cos_a * cos_b - sin_a * sin_b)
                sin_rows.append(sin_a * cos_b_signed + cos_a * sin_b_signed)
            tables[rows.start] = (jnp.concatenate(cos_rows, axis=0), jnp.concatenate(sin_rows, axis=0))
        return tables[rows.start]

    def rope(t, rows):
        cos, sin_signed = rope_tables(rows)
        rotated = jnp.where(first_half, pltpu.roll(t, 96, 1), pltpu.roll(t, 32, 1))
        return t * cos + rotated * sin_signed

    def group_norm(zs, sl):
        s_lo = jnp.sum(jnp.where(lo, zs, 0.0), axis=1, keepdims=True)
        s_hi = jnp.sum(jnp.where(lo, 0.0, zs), axis=1, keepdims=True)
        d = zs - jnp.where(lo, s_lo, s_hi) * (1.0 / HEAD_DIM)
        dd = d * d
        v_lo = jnp.sum(jnp.where(lo, dd, 0.0), axis=1, keepdims=True)
        v_hi = jnp.sum(jnp.where(lo, 0.0, dd), axis=1, keepdims=True)
        var = jnp.where(lo, v_lo, v_hi) * (1.0 / HEAD_DIM)
        return d * lax.rsqrt(var + LN_EPS) * lng_ref[:, sl] + lnb_ref[:, sl]

    def epilogue(seg, h, rows):
        if seg == 2:
            v_ref[rows, :] = h.astype(BF16)
        elif seg == 3:
            u_ref[rows, :] = _erf_gelu(h).astype(BF16)
        else:
            if seg == 4:
                h = _erf_gelu(h)
            for c in range(width // LANES):
                sl = slice(c * LANES, (c + 1) * LANES)
                if seg == 0:
                    q_ref[rows, sl] = rope(h[:, sl] * (HEAD_DIM ** -0.5), rows).astype(BF16)
                elif seg == 1:
                    k_ref[rows, sl] = rope(h[:, sl], rows).astype(BF16)
                else:
                    z_ref[rows, sl] = group_norm(h[:, sl], sl).astype(BF16)

    xs = [None] * n_sub
    hs = [[None] * n_seg for _ in range(n_sub)]
    for s in range(n_sub + 1):
        if s < n_sub:
            xs[s] = x_ref[row_slices[s], :].astype(BF16)
        for seg in range(n_seg):
            if s < n_sub:
                hs[s][seg] = jnp.dot(xs[s], w_ref[:, seg * width:(seg + 1) * width],
                                     preferred_element_type=F32)
            if s >= 1:
                epilogue(seg, hs[s - 1][seg], row_slices[s - 1])
        for src, dst in list(zip(cast_in, cast_out))[s::n_sub + 1]:
            dst[...] = src[...].astype(BF16)


def _in_proj(x2, w_in, rope_hi, rope_lo, lng, lnb, later_weights, *, seq, width, tm, n_sub):
    tokens, d_model = x2.shape
    n_steps = tokens // tm
    n_pos_tiles = seq // tm
    assert (tm // n_sub) % ROPE_SPLIT == 0 and (tm // ROPE_SPLIT) % 8 == 0
    out = jax.ShapeDtypeStruct((tokens, width), BF16)
    tile = pl.BlockSpec((tm, width), lambda t: (t, 0))
    hi_spec = pl.BlockSpec((2, tm // ROPE_SPLIT, LANES), lambda t: (0, t % n_pos_tiles, 0))
    lo_spec = pl.BlockSpec(rope_lo.shape, lambda t: (0, 0, 0))
    row = pl.BlockSpec((1, width), lambda t: (0, 0))
    slabs = []
    for w in later_weights:
        assert w.shape[0] % (16 * n_steps) == 0
        slabs.append(pl.BlockSpec((w.shape[0] // n_steps, w.shape[1]), lambda t: (t, 0)))
    res = pl.pallas_call(
        functools.partial(_in_proj_kernel, width=width, n_sub=n_sub, n_cast=len(later_weights)),
        grid=(n_steps,),
        in_specs=[pl.BlockSpec((tm, d_model), lambda t: (t, 0)),
                  pl.BlockSpec(w_in.shape, lambda t: (0, 0), pipeline_mode=pl.Buffered(1)),
                  hi_spec, lo_spec, row, row, *slabs],
        out_specs=[tile] * 5 + slabs,
        out_shape=[out] * 5 + [jax.ShapeDtypeStruct(w.shape, BF16) for w in later_weights],
        scratch_shapes=[pltpu.VMEM(w_in.shape, BF16)],
        compiler_params=pltpu.CompilerParams(dimension_semantics=("arbitrary",),
                                             vmem_limit_bytes=VMEM_LIMIT),
        name="in_proj",
    )(x2, w_in, rope_hi, rope_lo, lng, lnb, *later_weights)
    return res[:5], res[5:]


def _aligned(start, multiple):
    return start if isinstance(start, int) else pl.multiple_of(start, multiple)


def _attn_kernel(q_ref, k_ref, v_ref, o_ref, ka_ref, vt_ref, km_ref, qa_ref,
                 s0_ref, s1_ref, m0_ref, m1_ref, acc0_ref, acc1_ref, *, nb, n_hp):
    blk = MOBA_BLOCK
    sub = blk // 8
    s_refs, m_refs = (s0_ref, s1_ref), (m0_ref, m1_ref)
    acc_refs = (acc0_ref, acc1_ref)
    assert nb % 4 == 0
    lane = lax.broadcasted_iota(jnp.int32, (1, LANES), 1)
    head_lanes = [(lane >= HEAD_DIM * h) & (lane < HEAD_DIM * (h + 1)) for h in range(2)]
    bias_off = [HEAD_DIM, 0]

    def build_keys(hp):
        lanes = slice(hp * LANES, (hp + 1) * LANES)
        km_pair = jnp.sum(k_ref[:, lanes].astype(F32).reshape(nb, blk, LANES), axis=1) * (1.0 / blk)
        for h in range(2):
            km = jnp.where(head_lanes[h], km_pair, 0.0)
            hi = km.astype(BF16)
            km_ref[hp, (2 * h) * nb:(2 * h + 1) * nb, :] = hi
            km_ref[hp, (2 * h + 1) * nb:(2 * h + 2) * nb, :] = (km - hi.astype(F32)).astype(BF16)
        for j in range(nb):
            rows = slice(j * blk, (j + 1) * blk)
            kj = k_ref[rows, lanes]
            for h in range(2):
                onehot = jnp.where(lane == bias_off[h] + j, 1.0, 0.0).astype(BF16)
                ka_ref[hp, h, rows, :] = jnp.where(head_lanes[h], kj, onehot)

    def build_values(hp, j):
        ones_row = jnp.where(lax.broadcasted_iota(jnp.int32, (ACC_ROWS - HEAD_DIM, blk), 0) == 0, 1.0, 0.0)
        v_t = v_ref[j * blk:(j + 1) * blk, hp * LANES:(hp + 1) * LANES].astype(F32).T
        for h in range(2):
            vt_ref[hp, j, h] = jnp.concatenate(
                [v_t[HEAD_DIM * h:HEAD_DIM * (h + 1)], ones_row], axis=0).astype(BF16)

    row = lax.broadcasted_iota(jnp.int32, (nb, blk), 0)
    rowf = row.astype(F32)
    kk = lax.broadcasted_iota(jnp.int32, (blk, blk), 0)
    qq = lax.broadcasted_iota(jnp.int32, (blk, blk), 1)

    def select_blocks(hp, r):
        q_t = q_ref[r * blk:(r + 1) * blk, hp * LANES:(hp + 1) * LANES].astype(F32).T.astype(BF16)
        g_all = jnp.dot(km_ref[hp], q_t, preferred_element_type=F32)
        for h in range(2):
            g = g_all[(2 * h) * nb:(2 * h + 1) * nb] + g_all[(2 * h + 1) * nb:(2 * h + 2) * nb]
            past = row < r
            g = jnp.where(past, g, NEG_INF)
            bias = jnp.where(row == r, 0.0, NEG_INF)
            for _ in range(MOBA_TOPK):
                top = jnp.max(g, axis=0, keepdims=True)
                first = jnp.min(jnp.where(g == top, rowf, float(nb)), axis=0, keepdims=True)
                pick = rowf == first
                bias = jnp.where(pick & past, 0.0, bias)
                g = jnp.where(pick, -jnp.inf, g)
            q_h = q_t[HEAD_DIM * h:HEAD_DIM * (h + 1)]
            bias = bias.astype(BF16)
            pad = jnp.zeros((HEAD_DIM - nb, blk), BF16)
            pieces = [q_h, bias, pad] if h == 0 else [bias, pad, q_h]
            qa_ref[hp, r, h] = jnp.concatenate(pieces, axis=0)

    def score_tile(slot, t, hp, r, rs, j, causal):
        m_ref, s_ref = m_refs[slot], s_refs[slot]
        k0 = _aligned(j * blk, blk)
        for h in range(2):
            s = jnp.dot(ka_ref[hp, h, pl.ds(k0, blk), :], qa_ref[hp, r, h],
                        preferred_element_type=F32)
            if causal is not None:
                s = s + causal
            s_ref[t, h] = s
            m_ref[rs, h] = jnp.maximum(m_ref[rs, h], jnp.max(s.reshape(sub, 8, blk), axis=0))

    def value_tile(slot, t, hp, rs, j):
        m_ref, s_ref, acc_ref = m_refs[slot], s_refs[slot], acc_refs[slot]
        for h in range(2):
            p = jnp.exp(s_ref[t, h].reshape(sub, 8, blk) - m_ref[rs, h]).reshape(blk, blk)
            acc_ref[rs, h] += jnp.dot(vt_ref[hp, j, h], p.astype(BF16), preferred_element_type=F32)

    def past_tile(u, pi):
        if isinstance(pi, int):
            return (0, u) if u < pi else (1, u - pi)
        first = u < pi
        return jnp.where(first, 0, 1), jnp.where(first, u, u - pi)

    def tile_of(t, pi):
        rows = (pi, nb - 1 - pi)
        return (t, rows[t]) if t < 2 else past_tile(t - 2, pi)

    n_pairs = nb // 2
    assert n_pairs & (n_pairs - 1) == 0

    def split(g):
        if isinstance(g, int):
            return divmod(g, n_pairs)
        return lax.shift_right_logical(g, n_pairs.bit_length() - 1), lax.bitwise_and(g, n_pairs - 1)

    def write_rows(g, slot, rs):
        hp, pair = split(g)
        outs = []
        for h in range(2):
            a = acc_refs[slot][rs, h]
            outs.append(a[:HEAD_DIM] / a[HEAD_DIM:HEAD_DIM + 1])
        q0 = _aligned((pair, nb - 1 - pair)[rs] * blk, blk)
        o_ref[hp, pl.ds(q0, blk), :] = jnp.concatenate(outs, axis=0).T.astype(BF16)

    def step(k, slot):
        live = lambda g: not isinstance(g, int) or 0 <= g < n_hp * n_pairs
        do_score, do_value, do_write = (live(g) for g in (k, k - 1, k - 2))
        (score_hp, score_pair), (value_hp, value_pair), write_pair = split(k), split(k - 1), k - 2
        if do_score:
            causal = jnp.where(kk <= qq, 0.0, NEG_INF)
            for rs in range(2):
                for h in range(2):
                    m_refs[slot][rs, h] = jnp.full((8, blk), NEG_INF, F32)
        if do_value:
            for rs in range(2):
                for h in range(2):
                    acc_refs[1 - slot][rs, h] = jnp.zeros((ACC_ROWS, blk), F32)
        write_at = {1: 0, nb // 2: 1}
        for t in range(nb + 1):
            if do_score:
                rs, j = tile_of(t, score_pair)
                r = score_pair + rs * (nb - 1 - 2 * score_pair)
                score_tile(slot, t, score_hp, r, rs, j, causal if t < 2 else None)
            if do_value:
                rs, j = tile_of(t, value_pair)
                value_tile(1 - slot, t, value_hp, rs, j)
            if do_write and t in write_at:
                write_rows(write_pair, slot, write_at[t])
        if do_score:
            m_ref = m_refs[slot]
            for rs in range(2):
                for h in range(2):
                    m_ref[rs, h] = jnp.broadcast_to(jnp.max(m_ref[rs, h], axis=0, keepdims=True),
                                                    (8, blk))

    for hp in range(n_hp):
        build_keys(hp)
        for j in range(nb):
            build_values(hp, j)
            select_blocks(hp, j)
    total = n_hp * n_pairs
    step(0, 0)
    step(1, 1)

    def two_steps(i2, _):
        step(2 * i2 + 2, 0)
        step(2 * i2 + 3, 1)
        return 0

    n_rolled = (total - 2) // 2
    lax.fori_loop(0, n_rolled, two_steps, 0)
    for k in range(2 + 2 * n_rolled, total + 2):
        step(k, k % 2)


def _attention(q, k, v, *, batch, seq, n_hp):
    width = q.shape[-1]
    nb = seq // MOBA_BLOCK
    q3, k3, v3 = (t.reshape(batch, seq, width) for t in (q, k, v))
    spec = pl.BlockSpec((None, seq, n_hp * LANES), lambda b, p: (b, 0, p))
    return pl.pallas_call(
        functools.partial(_attn_kernel, nb=nb, n_hp=n_hp),
        grid=(batch, width // (n_hp * LANES)),
        in_specs=[spec, spec, spec],
        out_specs=pl.BlockSpec((None, n_hp, seq, LANES), lambda b, p: (b, p, 0, 0)),
        out_shape=jax.ShapeDtypeStruct((batch, width // LANES, seq, LANES), BF16),
        scratch_shapes=[pltpu.VMEM((n_hp, 2, seq, LANES), BF16),
                        pltpu.VMEM((n_hp, nb, 2, ACC_ROWS, MOBA_BLOCK), BF16),
                        pltpu.VMEM((n_hp, 4 * nb, LANES), BF16),
                        pltpu.VMEM((n_hp, nb, 2, LANES, MOBA_BLOCK), BF16),
                        *[pltpu.VMEM((nb + 1, 2, MOBA_BLOCK, MOBA_BLOCK), F32)] * 2,
                        *[pltpu.VMEM((2, 2, 8, MOBA_BLOCK), F32)] * 2,
                        *[pltpu.VMEM((2, 2, ACC_ROWS, MOBA_BLOCK), F32)] * 2],
        compiler_params=pltpu.CompilerParams(dimension_semantics=("arbitrary", "arbitrary"),
                                             vmem_limit_bytes=ATTN_VMEM_LIMIT),
        name="moba_attention",
    )(q3, k3, v3)


def _layer_norm(y, g, b):
    mu = jnp.mean(y, axis=-1, keepdims=True)
    d = y - mu
    var = jnp.mean(d * d, axis=-1, keepdims=True)
    return d * lax.rsqrt(var + LN_EPS) * g + b


def _rms_norm(t, g):
    ms = jnp.mean(t * t, axis=-1, keepdims=True)
    return t * lax.rsqrt(ms + RMS_EPS) * g


def _mix_pieces(attn_ref, u_ref, z_ref, x_ref, ws_ref, bs_ref, ag_ref, gg_ref, w_ref, lg_ref, lb_ref,
                sg_ref, store, *, alpha, width, n_sub):
    c = GMLP_CHUNK
    sm = x_ref.shape[0] // n_sub
    n_pairs = ws_ref.shape[0] // 2
    ii = lax.broadcasted_iota(jnp.int32, (c, c), 0)
    jj = lax.broadcasted_iota(jnp.int32, (c, c), 1)
    lane = lax.broadcasted_iota(jnp.int32, (1, LANES), 1)
    lo = lane < HEAD_DIM
    w_cat = [jnp.concatenate([jnp.where(jj <= ii, ws_ref[2 * p + g], 0.0) for g in range(2)],
                             axis=1).astype(BF16) for p in range(n_pairs)]

    def z_stack(rows, sl):
        z = z_ref[rows, sl]
        zero = jnp.zeros_like(z)
        return jnp.concatenate([jnp.where(lo, z, zero), jnp.where(lo, zero, z)], axis=0)

    def gate_and_norm(s):
        for c2 in range(sm // (2 * c)):
            r0 = s * sm + c2 * 2 * c
            rows = [slice(r0, r0 + c), slice(r0 + c, r0 + 2 * c)]
            for p in range(n_pairs):
                sl = slice(p * LANES, (p + 1) * LANES)
                rhs = jnp.concatenate([z_stack(rows[0], sl), z_stack(rows[1], sl)], axis=1)
                mixed = jnp.dot(w_cat[p], rhs, preferred_element_type=F32)
                for k in range(2):
                    sg_ref[rows[k], sl] = (u_ref[rows[k], sl].astype(F32)
                                           * (mixed[:, k * LANES:(k + 1) * LANES] + bs_ref[:, sl]))
        rows = slice(s * sm, (s + 1) * sm)
        attn = jnp.concatenate([attn_ref[p, rows, :] for p in range(attn_ref.shape[0])], axis=1)
        an = _rms_norm(attn.astype(F32), ag_ref[...]).astype(BF16)
        sn = _rms_norm(sg_ref[rows, :], gg_ref[...]).astype(BF16)
        return an, sn

    def project(normed, cols):
        an, sn = normed
        return (jnp.dot(an, w_ref[:width, cols], preferred_element_type=F32)
                + jnp.dot(sn, w_ref[width:, cols], preferred_element_type=F32))

    def residual_norm(s, mix):
        rows = slice(s * sm, (s + 1) * sm)
        store(rows, _layer_norm(alpha * x_ref[rows, :] + mix, lg_ref[...], lb_ref[...]))

    d_model = w_ref.shape[1]
    col_halves = [slice(0, d_model // 2), slice(d_model // 2, d_model)]
    normed, lefts, mixes = {}, {}, {}

    def do_norm(s):
        normed[s] = gate_and_norm(s)

    def do_left(s):
        lefts[s] = project(normed[s], col_halves[0])

    def do_right(s):
        mixes[s] = jnp.concatenate([lefts.pop(s), project(normed.pop(s), col_halves[1])], axis=1)

    def do_out(s):
        residual_norm(s, mixes.pop(s))

    pieces = []
    for s in range(n_sub + 2):
        if s < n_sub:
            pieces.append(functools.partial(do_norm, s))
        if 1 <= s <= n_sub:
            pieces.append(functools.partial(do_left, s - 1))
        if 2 <= s:
            pieces.append(functools.partial(do_out, s - 2))
        if 1 <= s <= n_sub:
            pieces.append(functools.partial(do_right, s - 1))
    return pieces


def _post_kernel(attn_ref, u_ref, z_ref, x_ref, attn0_ref, u0_ref, z0_ref, x0_ref, ws_ref, bs_ref,
                 ag_ref, gg_ref, wo_ref, l1g_ref, l1b_ref, wg_ref, wu_ref, wd_ref, l2g_ref, l2b_ref,
                 o_ref, x1_ref, sg_ref, acc_ref, *, alpha, width, hc, n_sub):
    t = pl.program_id(0)
    slot = t % 2
    mix_refs = (ws_ref, bs_ref, ag_ref, gg_ref, wo_ref, l1g_ref, l1b_ref, sg_ref)
    mix_kw = dict(alpha=alpha, width=width, n_sub=n_sub)

    @pl.when(t == 0)
    def _():
        def store_first(rows, value):
            x1_ref[0, rows, :] = value
        for piece in _mix_pieces(attn0_ref, u0_ref, z0_ref, x0_ref, *mix_refs, store_first, **mix_kw):
            piece()

    def store_next(rows, value):
        x1_ref[1 - slot, rows, :] = value

    pieces = _mix_pieces(attn_ref, u_ref, z_ref, x_ref, *mix_refs, store_next, **mix_kw)
    hidden = wg_ref.shape[1]
    n_chunks = hidden // hc
    assert hidden % hc == 0 and len(pieces) <= n_chunks
    xb = x1_ref[slot].astype(BF16)
    for c in range(n_chunks):
        sl = slice(c * hc, (c + 1) * hc)
        g = jnp.dot(xb, wg_ref[:, sl], preferred_element_type=F32)
        u = jnp.dot(xb, wu_ref[:, sl], preferred_element_type=F32)
        h = (g / (1.0 + jnp.exp(-g)) * u).astype(BF16)
        part = jnp.dot(h, wd_ref[sl, :], preferred_element_type=F32)
        if c == 0:
            acc_ref[...] = part
        else:
            acc_ref[...] += part
        if c < len(pieces):
            pieces[c]()
    o_ref[...] = _layer_norm(alpha * x1_ref[slot] + acc_ref[...], l2g_ref[...], l2b_ref[...])


def _post_attention(attn, u, z, x2, w_spatial, bias_full, ag, gg, w_out, l1g, l1b,
                    w_gate, w_up, w_down, l2g, l2b, *, alpha, tm, hc, n_sub):
    tokens, width = u.shape
    d_model = x2.shape[1]
    n_tiles = tokens // tm
    n_hp, seq = attn.shape[1], attn.shape[2]
    tiles_per_seq = seq // tm
    assert (tm // n_sub) % (2 * GMLP_CHUNK) == 0 and seq % tm == 0
    nxt = lambda t: (jnp.minimum(t + 1, n_tiles - 1), 0)
    once = dict(pipeline_mode=pl.Buffered(1))

    def attn_nxt(t):
        tile = jnp.minimum(t + 1, n_tiles - 1)
        return (tile // tiles_per_seq, 0, tile % tiles_per_seq, 0)

    attn_n = pl.BlockSpec((None, n_hp, tm, LANES), attn_nxt)
    attn_0 = pl.BlockSpec((None, n_hp, tm, LANES), lambda t: (0, 0, 0, 0), **once)
    half_n = pl.BlockSpec((tm, width), nxt)
    full_n = pl.BlockSpec((tm, d_model), nxt)
    half_0 = pl.BlockSpec((tm, width), lambda t: (0, 0), **once)
    full_0 = pl.BlockSpec((tm, d_model), lambda t: (0, 0), **once)
    const = lambda a: pl.BlockSpec(a.shape, lambda t: (0,) * a.ndim, **once)
    return pl.pallas_call(
        functools.partial(_post_kernel, alpha=alpha, width=width, hc=hc, n_sub=n_sub),
        grid=(n_tiles,),
        in_specs=[attn_n, half_n, half_n, full_n, attn_0, half_0, half_0, full_0,
                  const(w_spatial), const(bias_full), const(ag), const(gg), const(w_out),
                  const(l1g), const(l1b), const(w_gate), const(w_up), const(w_down),
                  const(l2g), const(l2b)],
        out_specs=pl.BlockSpec((tm, d_model), lambda t: (t, 0)),
        out_shape=jax.ShapeDtypeStruct((tokens, d_model), F32),
        scratch_shapes=[pltpu.VMEM((2, tm, d_model), F32),
                        pltpu.VMEM((tm, width), F32),
                        pltpu.VMEM((tm, d_model), F32)],
        compiler_params=pltpu.CompilerParams(dimension_semantics=("arbitrary",),
                                             vmem_limit_bytes=POST_VMEM_LIMIT),
        name="mixer_out_ffn",
    )(attn, u, z, x2, attn, u, z, x2, w_spatial, bias_full, ag, gg, w_out, l1g, l1b,
      w_gate, w_up, w_down, l2g, l2b)


def _rope_tables(seq):
    assert seq % ROPE_SPLIT == 0
    inv_freq = ROPE_THETA ** (-jnp.arange(0, HEAD_DIM, 2, dtype=F32) / HEAD_DIM)
    tile = lambda t: jnp.concatenate([t, t, t, t], axis=-1)

    def cos_sin(pos):
        ang = pos.astype(F32)[:, None] * inv_freq[None, :]
        return tile(jnp.cos(ang)), tile(jnp.sin(ang))

    cos_hi, sin_hi = cos_sin(jnp.arange(0, seq, ROPE_SPLIT, dtype=jnp.int32))
    cos_lo, sin_lo = cos_sin(jnp.arange(ROPE_SPLIT, dtype=jnp.int32))
    sign = jnp.where((jnp.arange(LANES) % HEAD_DIM) < HEAD_DIM // 2, -1.0, 1.0)[None, :]
    return jnp.stack([cos_hi, sin_hi]), jnp.stack([cos_lo, sin_lo, sign * cos_lo, sign * sin_lo])


def kernel(x, w_in, attn_out_g, gmlp_out_g, gmlp_ln_g, gmlp_ln_b, w_spatial, b_spatial, w_out,
           ln1_g, ln1_b, w_gate, w_up, w_down, ln2_g, ln2_b):
    batch, seq, d_model = x.shape
    depth = w_in.shape[0]
    width = attn_out_g.shape[-1]
    alpha = (2 * depth) ** 0.25
    assert w_in.shape[-1] == 5 * width and seq % MOBA_BLOCK == 0 and width % LANES == 0
    assert min(MOBA_TOPK, seq // MOBA_BLOCK - 1) == MOBA_TOPK
    rope_hi, rope_lo = _rope_tables(seq)
    x2 = x.reshape(batch * seq, d_model)
    for l in range(depth):
        (q, k, v, u, z), (w_out_b, w_gate_b, w_up_b, w_down_b) = _in_proj(
            x2, w_in[l], rope_hi, rope_lo,
            gmlp_ln_g[l].reshape(1, width), gmlp_ln_b[l].reshape(1, width),
            (w_out[l], w_gate[l], w_up[l], w_down[l]), seq=seq, width=width, tm=IN_PROJ_TILE, n_sub=IN_PROJ_SUBTILES)
        attn = _attention(q, k, v, batch=batch, seq=seq, n_hp=ATTN_HEAD_PAIRS)
        bias_full = jnp.repeat(b_spatial[l].T, HEAD_DIM, axis=1)
        x2 = _post_attention(attn, u, z, x2, w_spatial[l], bias_full,
                             attn_out_g[l].reshape(1, width), gmlp_out_g[l].reshape(1, width),
                             w_out_b, ln1_g[l].reshape(1, d_model), ln1_b[l].reshape(1, d_model),
                             w_gate_b, w_up_b, w_down_b,
                             ln2_g[l].reshape(1, d_model), ln2_b[l].reshape(1, d_model),
                             alpha=alpha, tm=POST_TILE, hc=FFN_CHUNK, n_sub=POST_SUBTILES)
    return x2.reshape(batch, seq, d_model)
```

```python
import functools

import jax
import jax.numpy as jnp
from jax import lax
from jax.experimental import pallas as pl
from jax.experimental.pallas import tpu as pltpu

HEAD_DIM = 64
MOBA_BLOCK = 256
MOBA_TOPK = 3
GMLP_CHUNK = 128
ROPE_THETA = 10000.0
ROPE_SPLIT = 64
LN_EPS = 1e-5
RMS_EPS = 1e-6
NEG_INF = -1e30
LANES = 128
ACC_ROWS = HEAD_DIM + 16
F32 = jnp.float32
BF16 = jnp.bfloat16

IN_PROJ_TILE, IN_PROJ_SUBTILES = 1024, 8
POST_TILE, POST_SUBTILES = 512, 2
FFN_CHUNK = 256
VMEM_LIMIT = 48 * 1024 * 1024
POST_VMEM_LIMIT = 56 * 1024 * 1024


def _erf_gelu(t):
    return 0.5 * t * (1.0 + lax.erf(t * 0.7071067811865476))


def _in_proj_kernel(x_ref, w32_ref, rope_hi_ref, rope_lo_ref, lng_ref, lnb_ref, *rest,
                    width, n_sub, n_cast):
    cast_in, rest = rest[:n_cast], rest[n_cast:]
    (q_ref, k_ref, v_ref, u_ref, z_ref), rest = rest[:5], rest[5:]
    cast_out, (w_ref,) = rest[:n_cast], rest[n_cast:]

    @pl.when(pl.program_id(0) == 0)
    def _():
        w_ref[...] = w32_ref[...].astype(BF16)

    lane = lax.broadcasted_iota(jnp.int32, (1, LANES), 1)
    lo = lane < HEAD_DIM
    sm = x_ref.shape[0] // n_sub
    n_seg = w_ref.shape[1] // width
    row_slices = [slice(s * sm, (s + 1) * sm) for s in range(n_sub)]

    first_half = (lane % HEAD_DIM) < HEAD_DIM // 2
    tables = {}

    def rope_tables(rows):
        if rows.start not in tables:
            cos_b, sin_b, cos_b_signed, sin_b_signed = (rope_lo_ref[i] for i in range(4))
            cos_rows, sin_rows = [], []
            for a in range(rows.start // ROPE_SPLIT, rows.stop // ROPE_SPLIT):
                cos_a, sin_a = rope_hi_ref[0, a:a + 1, :], rope_hi_ref[1, a:a + 1, :]
                cos_rows.append(cos_a * cos_b - sin_a * sin_b)
                sin_rows.append(sin_a * cos_b_signed + cos_a * sin_b_signed)
            tables[rows.start] = (jnp.concatenate(cos_rows, axis=0), jnp.concatenate(sin_rows, axis=0))
        return tables[rows.start]

    def rope(t, rows):
        cos, sin_signed = rope_tables(rows)
        rotated = jnp.where(first_half, pltpu.roll(t, 96, 1), pltpu.roll(t, 32, 1))
        return t * cos + rotated * sin_signed

    def group_norm(zs, sl):
        s_lo = jnp.sum(jnp.where(lo, zs, 0.0), axis=1, keepdims=True)
        s_hi = jnp.sum(jnp.where(lo, 0.0, zs), axis=1, keepdims=True)
        d = zs - jnp.where(lo, s_lo, s_hi) * (1.0 / HEAD_DIM)
        dd = d * d
        v_lo = jnp.sum(jnp.where(lo, dd, 0.0), axis=1, keepdims=True)
        v_hi = jnp.sum(jnp.where(lo, 0.0, dd), axis=1, keepdims=True)
        var = jnp.where(lo, v_lo, v_hi) * (1.0 / HEAD_DIM)
        return d * lax.rsqrt(var + LN_EPS) * lng_ref[:, sl] + lnb_ref[:, sl]

    def epilogue(seg, h, rows):
        if seg == 2:
            v_ref[rows, :] = h.astype(BF16)
        elif seg == 3:
            u_ref[rows, :] = _erf_gelu(h).astype(BF16)
        else:
            if seg == 4:
                h = _erf_gelu(h)
            for c in range(width // LANES):
                sl = slice(c * LANES, (c + 1) * LANES)
                if seg == 0:
                    q_ref[rows, sl] = rope(h[:, sl] * (HEAD_DIM ** -0.5), rows).astype(BF16)
                elif seg == 1:
                    k_ref[rows, sl] = rope(h[:, sl], rows).astype(BF16)
                else:
                    z_ref[rows, sl] = group_norm(h[:, sl], sl).astype(BF16)

    order = (4, 3, 0, 1, 2)
    xs = [None] * n_sub
    hs = [[None] * n_seg for _ in range(n_sub)]
    last = n_sub - 1
    for s in range(n_sub):
        xs[s] = x_ref[row_slices[s], :].astype(BF16)
        done = None
        for seg in order:
            hs[s][seg] = jnp.dot(xs[s], w_ref[:, seg * width:(seg + 1) * width],
                                 preferred_element_type=F32)
            if s >= 1:
                epilogue(seg, hs[s - 1][seg], row_slices[s - 1])
            if s == last and done is not None:
                epilogue(done, hs[s][done], row_slices[s])
            done = seg
        for src, dst in list(zip(cast_in, cast_out))[s::n_sub]:
            dst[...] = src[...].astype(BF16)
    epilogue(order[-1], hs[last][order[-1]], row_slices[last])


def _in_proj(x2, w_in, rope_hi, rope_lo, lng, lnb, later_weights, *, seq, width, tm, n_sub):
    tokens, d_model = x2.shape
    n_steps = tokens // tm
    n_pos_tiles = seq // tm
    assert (tm // n_sub) % ROPE_SPLIT == 0 and (tm // ROPE_SPLIT) % 8 == 0
    out = jax.ShapeDtypeStruct((tokens, width), BF16)
    tile = pl.BlockSpec((tm, width), lambda t: (t, 0))
    hi_spec = pl.BlockSpec((2, tm // ROPE_SPLIT, LANES), lambda t: (0, t % n_pos_tiles, 0))
    lo_spec = pl.BlockSpec(rope_lo.shape, lambda t: (0, 0, 0))
    row = pl.BlockSpec((1, width), lambda t: (0, 0))
    slabs = []
    for w in later_weights:
        assert w.shape[0] % (16 * n_steps) == 0
        slabs.append(pl.BlockSpec((w.shape[0] // n_steps, w.shape[1]), lambda t: (t, 0)))
    res = pl.pallas_call(
        functools.partial(_in_proj_kernel, width=width, n_sub=n_sub, n_cast=len(later_weights)),
        grid=(n_steps,),
        in_specs=[pl.BlockSpec((tm, d_model), lambda t: (t, 0)),
                  pl.BlockSpec(w_in.shape, lambda t: (0, 0), pipeline_mode=pl.Buffered(1)),
                  hi_spec, lo_spec, row, row, *slabs],
        out_specs=[tile] * 5 + slabs,
        out_shape=[out] * 5 + [jax.ShapeDtypeStruct(w.shape, BF16) for w in later_weights],
        scratch_shapes=[pltpu.VMEM(w_in.shape, BF16)],
        compiler_params=pltpu.CompilerParams(dimension_semantics=("arbitrary",),
                                             vmem_limit_bytes=VMEM_LIMIT),
        name="in_proj",
    )(x2, w_in, rope_hi, rope_lo, lng, lnb, *later_weights)
    return res[:5], res[5:]


def _aligned(start, multiple):
    return start if isinstance(start, int) else pl.multiple_of(start, multiple)


def _attn_kernel(q_ref, k_ref, v_ref, o_ref, ka_ref, vt_ref, km_ref, qa_ref,
                 s0_ref, s1_ref, m0_ref, m1_ref, acc0_ref, acc1_ref, *, nb):
    blk = MOBA_BLOCK
    sub = blk // 8
    s_refs, m_refs = (s0_ref, s1_ref), (m0_ref, m1_ref)
    acc_refs = (acc0_ref, acc1_ref)
    assert nb % 4 == 0
    lane = lax.broadcasted_iota(jnp.int32, (1, LANES), 1)
    head_lanes = [(lane >= HEAD_DIM * h) & (lane < HEAD_DIM * (h + 1)) for h in range(2)]
    bias_off = [HEAD_DIM, 0]

    km_pair = jnp.sum(k_ref[...].astype(F32).reshape(nb, blk, LANES), axis=1) * (1.0 / blk)
    for h in range(2):
        km = jnp.where(head_lanes[h], km_pair, 0.0)
        hi = km.astype(BF16)
        km_ref[(2 * h) * nb:(2 * h + 1) * nb, :] = hi
        km_ref[(2 * h + 1) * nb:(2 * h + 2) * nb, :] = (km - hi.astype(F32)).astype(BF16)
    for j in range(nb):
        rows = slice(j * blk, (j + 1) * blk)
        kj = k_ref[rows, :]
        for h in range(2):
            onehot = jnp.where(lane == bias_off[h] + j, 1.0, 0.0).astype(BF16)
            ka_ref[h, rows, :] = jnp.where(head_lanes[h], kj, onehot)

    def build_values(j):
        ones_row = jnp.where(lax.broadcasted_iota(jnp.int32, (ACC_ROWS - HEAD_DIM, blk), 0) == 0, 1.0, 0.0)
        v_t = v_ref[j * blk:(j + 1) * blk, :].astype(F32).T
        for h in range(2):
            vt_ref[j, h] = jnp.concatenate(
                [v_t[HEAD_DIM * h:HEAD_DIM * (h + 1)], ones_row], axis=0).astype(BF16)

    row = lax.broadcasted_iota(jnp.int32, (nb, blk), 0)
    rowf = row.astype(F32)
    kk = lax.broadcasted_iota(jnp.int32, (blk, blk), 0)
    qq = lax.broadcasted_iota(jnp.int32, (blk, blk), 1)

    def select_blocks(r):
        q_t = q_ref[r * blk:(r + 1) * blk, :].astype(F32).T.astype(BF16)
        g_all = jnp.dot(km_ref[...], q_t, preferred_element_type=F32)
        for h in range(2):
            g = g_all[(2 * h) * nb:(2 * h + 1) * nb] + g_all[(2 * h + 1) * nb:(2 * h + 2) * nb]
            past = row < r
            g = jnp.where(past, g, NEG_INF)
            bias = jnp.where(row == r, 0.0, NEG_INF)
            for _ in range(MOBA_TOPK):
                top = jnp.max(g, axis=0, keepdims=True)
                first = jnp.min(jnp.where(g == top, rowf, float(nb)), axis=0, keepdims=True)
                pick = rowf == first
                bias = jnp.where(pick & past, 0.0, bias)
                g = jnp.where(pick, -jnp.inf, g)
            q_h = q_t[HEAD_DIM * h:HEAD_DIM * (h + 1)]
            bias = bias.astype(BF16)
            pad = jnp.zeros((HEAD_DIM - nb, blk), BF16)
            pieces = [q_h, bias, pad] if h == 0 else [bias, pad, q_h]
            qa_ref[r, h] = jnp.concatenate(pieces, axis=0)

    def score_tile(slot, t, r, rs, j, causal):
        m_ref, s_ref = m_refs[slot], s_refs[slot]
        k0 = _aligned(j * blk, blk)
        for h in range(2):
            s = jnp.dot(ka_ref[h, pl.ds(k0, blk), :], qa_ref[r, h],
                        preferred_element_type=F32)
            if causal is not None:
                s = s + causal
            s_ref[t, h] = s
            m_ref[rs, h] = jnp.maximum(m_ref[rs, h], jnp.max(s.reshape(sub, 8, blk), axis=0))

    def value_tile(slot, t, rs, j):
        m_ref, s_ref, acc_ref = m_refs[slot], s_refs[slot], acc_refs[slot]
        for h in range(2):
            p = jnp.exp(s_ref[t, h].reshape(sub, 8, blk) - m_ref[rs, h]).reshape(blk, blk)
            acc_ref[rs, h] += jnp.dot(vt_ref[j, h], p.astype(BF16), preferred_element_type=F32)

    def past_tile(u, pi):
        if isinstance(pi, int):
            return (0, u) if u < pi else (1, u - pi)
        first = u < pi
        return jnp.where(first, 0, 1), jnp.where(first, u, u - pi)

    def tile_of(t, pi):
        rows = (pi, nb - 1 - pi)
        return (t, rows[t]) if t < 2 else past_tile(t - 2, pi)

    def write_rows(pair, slot, rs):
        outs = []
        for h in range(2):
            a = acc_refs[slot][rs, h]
            outs.append(a[:HEAD_DIM] / a[HEAD_DIM:HEAD_DIM + 1])
        q0 = _aligned((pair, nb - 1 - pair)[rs] * blk, blk)
        o_ref[pl.ds(q0, blk), :] = jnp.concatenate(outs, axis=0).T.astype(BF16)

    def step(k, slot):
        live = lambda p: not isinstance(p, int) or 0 <= p < nb // 2
        score_pair, value_pair, write_pair = k, k - 1, k - 2
        do_score, do_value, do_write = (live(p) for p in (score_pair, value_pair, write_pair))
        if do_score:
            causal = jnp.where(kk <= qq, 0.0, NEG_INF)
            for rs in range(2):
                for h in range(2):
                    m_refs[slot][rs, h] = jnp.full((8, blk), NEG_INF, F32)
        if do_value:
            for rs in range(2):
                for h in range(2):
                    acc_refs[1 - slot][rs, h] = jnp.zeros((ACC_ROWS, blk), F32)
        write_at = {1: 0, nb // 2: 1}
        for t in range(nb + 1):
            if do_score:
                rs, j = tile_of(t, score_pair)
                r = score_pair + rs * (nb - 1 - 2 * score_pair)
                score_tile(slot, t, r, rs, j, causal if t < 2 else None)
            if do_value:
                rs, j = tile_of(t, value_pair)
                value_tile(1 - slot, t, rs, j)
            if do_write and t in write_at:
                write_rows(write_pair, slot, write_at[t])
        if do_score:
            m_ref = m_refs[slot]
            for rs in range(2):
                for h in range(2):
                    m_ref[rs, h] = jnp.broadcast_to(jnp.max(m_ref[rs, h], axis=0, keepdims=True),
                                                    (8, blk))

    n_pairs = nb // 2
    for j in range(nb):
        build_values(j)
        select_blocks(j)
    step(0, 0)
    step(1, 1)

    def two_steps(i2, _):
        step(2 * i2 + 2, 0)
        step(2 * i2 + 3, 1)
        return 0

    n_rolled = (n_pairs - 2) // 2
    lax.fori_loop(0, n_rolled, two_steps, 0)
    for k in range(2 + 2 * n_rolled, n_pairs + 2):
        step(k, k % 2)


def _attention(q, k, v, *, batch, seq):
    width = q.shape[-1]
    nb = seq // MOBA_BLOCK
    q3, k3, v3 = (t.reshape(batch, seq, width) for t in (q, k, v))
    spec = pl.BlockSpec((None, seq, LANES), lambda b, p: (b, 0, p))
    out = pl.pallas_call(
        functools.partial(_attn_kernel, nb=nb),
        grid=(batch, width // LANES),
        in_specs=[spec, spec, spec],
        out_specs=spec,
        out_shape=jax.ShapeDtypeStruct((batch, seq, width), BF16),
        scratch_shapes=[pltpu.VMEM((2, seq, LANES), BF16),
                        pltpu.VMEM((nb, 2, ACC_ROWS, MOBA_BLOCK), BF16),
                        pltpu.VMEM((4 * nb, LANES), BF16),
                        pltpu.VMEM((nb, 2, LANES, MOBA_BLOCK), BF16),
                        *[pltpu.VMEM((nb + 1, 2, MOBA_BLOCK, MOBA_BLOCK), F32)] * 2,
                        *[pltpu.VMEM((2, 2, 8, MOBA_BLOCK), F32)] * 2,
                        *[pltpu.VMEM((2, 2, ACC_ROWS, MOBA_BLOCK), F32)] * 2],
        compiler_params=pltpu.CompilerParams(dimension_semantics=("arbitrary", "arbitrary"),
                                             vmem_limit_bytes=VMEM_LIMIT),
        name="moba_attention",
    )(q3, k3, v3)
    return out.reshape(batch * seq, width)


def _layer_norm(y, g, b):
    mu = jnp.mean(y, axis=-1, keepdims=True)
    d = y - mu
    var = jnp.mean(d * d, axis=-1, keepdims=True)
    return d * lax.rsqrt(var + LN_EPS) * g + b


def _rms_norm(t, g):
    ms = jnp.mean(t * t, axis=-1, keepdims=True)
    return t * lax.rsqrt(ms + RMS_EPS) * g


def _mix_pieces(attn_ref, u_ref, z_ref, x_ref, ws_ref, bs_ref, ag_ref, gg_ref, w_ref, lg_ref, lb_ref,
                sg_ref, store, *, alpha, width, n_sub):
    c = GMLP_CHUNK
    sm = x_ref.shape[0] // n_sub
    n_pairs = ws_ref.shape[0] // 2
    ii = lax.broadcasted_iota(jnp.int32, (c, c), 0)
    jj = lax.broadcasted_iota(jnp.int32, (c, c), 1)
    lane = lax.broadcasted_iota(jnp.int32, (1, LANES), 1)
    lo = lane < HEAD_DIM
    w_cat = [jnp.concatenate([jnp.where(jj <= ii, ws_ref[2 * p + g], 0.0) for g in range(2)],
                             axis=1).astype(BF16) for p in range(n_pairs)]

    def z_stack(rows, sl):
        z = z_ref[rows, sl]
        zero = jnp.zeros_like(z)
        return jnp.concatenate([jnp.where(lo, z, zero), jnp.where(lo, zero, z)], axis=0)

    def gate_and_norm(s):
        for c2 in range(sm // (2 * c)):
            r0 = s * sm + c2 * 2 * c
            rows = [slice(r0, r0 + c), slice(r0 + c, r0 + 2 * c)]
            for p in range(n_pairs):
                sl = slice(p * LANES, (p + 1) * LANES)
                rhs = jnp.concatenate([z_stack(rows[0], sl), z_stack(rows[1], sl)], axis=1)
                mixed = jnp.dot(w_cat[p], rhs, preferred_element_type=F32)
                for k in range(2):
                    sg_ref[rows[k], sl] = (u_ref[rows[k], sl].astype(F32)
                                           * (mixed[:, k * LANES:(k + 1) * LANES] + bs_ref[:, sl]))
        rows = slice(s * sm, (s + 1) * sm)
        an = _rms_norm(attn_ref[rows, :].astype(F32), ag_ref[...]).astype(BF16)
        sn = _rms_norm(sg_ref[rows, :], gg_ref[...]).astype(BF16)
        return an, sn

    def project(normed, cols):
        an, sn = normed
        return (jnp.dot(an, w_ref[:width, cols], preferred_element_type=F32)
                + jnp.dot(sn, w_ref[width:, cols], preferred_element_type=F32))

    def residual_norm(s, mix):
        rows = slice(s * sm, (s + 1) * sm)
        store(rows, _layer_norm(alpha * x_ref[rows, :] + mix, lg_ref[...], lb_ref[...]))

    d_model = w_ref.shape[1]
    col_halves = [slice(0, d_model // 2), slice(d_model // 2, d_model)]
    normed, lefts, mixes = {}, {}, {}

    def do_norm(s):
        normed[s] = gate_and_norm(s)

    def do_left(s):
        lefts[s] = project(normed[s], col_halves[0])

    def do_right(s):
        mixes[s] = jnp.concatenate([lefts.pop(s), project(normed.pop(s), col_halves[1])], axis=1)

    def do_out(s):
        residual_norm(s, mixes.pop(s))

    pieces = []
    for s in range(n_sub + 2):
        if s < n_sub:
            pieces.append(functools.partial(do_norm, s))
        if 1 <= s <= n_sub:
            pieces.append(functools.partial(do_left, s - 1))
        if 2 <= s:
            pieces.append(functools.partial(do_out, s - 2))
        if 1 <= s <= n_sub:
            pieces.append(functools.partial(do_right, s - 1))
    return pieces


def _post_kernel(attn_ref, u_ref, z_ref, x_ref, attn0_ref, u0_ref, z0_ref, x0_ref, ws_ref, bs_ref,
                 ag_ref, gg_ref, wo_ref, l1g_ref, l1b_ref, wg_ref, wu_ref, wd_ref, l2g_ref, l2b_ref,
                 o_ref, x1_ref, sg_ref, acc_ref, y_ref, *, alpha, width, hc, n_sub, n_tiles):
    t = pl.program_id(0)
    cur = lax.rem(t, 2)
    tm = o_ref.shape[0]
    mix_refs = (ws_ref, bs_ref, ag_ref, gg_ref, wo_ref, l1g_ref, l1b_ref, sg_ref)
    mix_kw = dict(alpha=alpha, width=width, n_sub=n_sub)

    def final_norm(rows):
        o_ref[rows, :] = _layer_norm(y_ref[rows, :], l2g_ref[...], l2b_ref[...])

    norm_rows = [slice(i * tm // 4, (i + 1) * tm // 4) for i in range(4)]

    @pl.when(t == 0)
    def _():
        def store_first(rows, value):
            x1_ref[0, rows, :] = value
        for piece in _mix_pieces(attn0_ref, u0_ref, z0_ref, x0_ref, *mix_refs, store_first, **mix_kw):
            piece()
        y_ref[...] = jnp.zeros(y_ref.shape, F32)

    @pl.when(t < n_tiles)
    def _():
        def store_next(rows, value):
            x1_ref[1 - cur, rows, :] = value

        pieces = _mix_pieces(attn_ref, u_ref, z_ref, x_ref, *mix_refs, store_next, **mix_kw)
        hidden = wg_ref.shape[1]
        n_chunks = hidden // hc
        assert hidden % hc == 0 and len(pieces) + 1 <= n_chunks and len(norm_rows) <= n_chunks
        first_norm = n_chunks - 1 - len(norm_rows)
        xb = x1_ref[cur].astype(BF16)
        for c in range(n_chunks):
            sl = slice(c * hc, (c + 1) * hc)
            g = jnp.dot(xb, wg_ref[:, sl], preferred_element_type=F32)
            u = jnp.dot(xb, wu_ref[:, sl], preferred_element_type=F32)
            h = (g / (1.0 + jnp.exp(-g)) * u).astype(BF16)
            part = jnp.dot(h, wd_ref[sl, :], preferred_element_type=F32)
            if c == 0:
                acc_ref[...] = part
            elif c < n_chunks - 1:
                acc_ref[...] += part
            else:
                y_ref[...] = alpha * x1_ref[cur] + (acc_ref[...] + part)
            if c < len(pieces):
                pieces[c]()
            if 0 <= c - first_norm < len(norm_rows):
                final_norm(norm_rows[c - first_norm])

    @pl.when(t == n_tiles)
    def _():
        for rows in norm_rows:
            final_norm(rows)


def _post_attention(attn, u, z, x2, w_spatial, bias_full, ag, gg, w_out, l1g, l1b,
                    w_gate, w_up, w_down, l2g, l2b, *, alpha, tm, hc, n_sub):
    tokens, width = attn.shape
    d_model = x2.shape[1]
    n_tiles = tokens // tm
    assert (tm // n_sub) % (2 * GMLP_CHUNK) == 0
    nxt = lambda t: (jnp.minimum(t + 1, n_tiles - 1), 0)
    once = dict(pipeline_mode=pl.Buffered(1))
    half_n = pl.BlockSpec((tm, width), nxt)
    full_n = pl.BlockSpec((tm, d_model), nxt)
    half_0 = pl.BlockSpec((tm, width), lambda t: (0, 0), **once)
    full_0 = pl.BlockSpec((tm, d_model), lambda t: (0, 0), **once)
    const = lambda a: pl.BlockSpec(a.shape, lambda t: (0,) * a.ndim, **once)
    return pl.pallas_call(
        functools.partial(_post_kernel, alpha=alpha, width=width, hc=hc, n_sub=n_sub, n_tiles=n_tiles),
        grid=(n_tiles + 1,),
        in_specs=[half_n, half_n, half_n, full_n, half_0, half_0, half_0, full_0,
                  const(w_spatial), const(bias_full), const(ag), const(gg), const(w_out),
                  const(l1g), const(l1b), const(w_gate), const(w_up), const(w_down),
                  const(l2g), const(l2b)],
        out_specs=pl.BlockSpec((tm, d_model), lambda t: (jnp.maximum(t - 1, 0), 0)),
        out_shape=jax.ShapeDtypeStruct((tokens, d_model), F32),
        scratch_shapes=[pltpu.VMEM((2, tm, d_model), F32),
                        pltpu.VMEM((tm, width), F32),
                        pltpu.VMEM((tm, d_model), F32),
                        pltpu.VMEM((tm, d_model), F32)],
        compiler_params=pltpu.CompilerParams(dimension_semantics=("arbitrary",),
                                             vmem_limit_bytes=POST_VMEM_LIMIT),
        name="mixer_out_ffn",
    )(attn, u, z, x2, attn, u, z, x2, w_spatial, bias_full, ag, gg, w_out, l1g, l1b,
      w_gate, w_up, w_down, l2g, l2b)


def _rope_tables(seq):
    assert seq % ROPE_SPLIT == 0
    inv_freq = ROPE_THETA ** (-jnp.arange(0, HEAD_DIM, 2, dtype=F32) / HEAD_DIM)
    tile = lambda t: jnp.concatenate([t, t, t, t], axis=-1)

    def cos_sin(pos):
        ang = pos.astype(F32)[:, None] * inv_freq[None, :]
        return tile(jnp.cos(ang)), tile(jnp.sin(ang))

    cos_hi, sin_hi = cos_sin(jnp.arange(0, seq, ROPE_SPLIT, dtype=jnp.int32))
    cos_lo, sin_lo = cos_sin(jnp.arange(ROPE_SPLIT, dtype=jnp.int32))
    sign = jnp.where((jnp.arange(LANES) % HEAD_DIM) < HEAD_DIM // 2, -1.0, 1.0)[None, :]
    return jnp.stack([cos_hi, sin_hi]), jnp.stack([cos_lo, sin_lo, sign * cos_lo, sign * sin_lo])


def kernel(x, w_in, attn_out_g, gmlp_out_g, gmlp_ln_g, gmlp_ln_b, w_spatial, b_spatial, w_out,
           ln1_g, ln1_b, w_gate, w_up, w_down, ln2_g, ln2_b):
    batch, seq, d_model = x.shape
    depth = w_in.shape[0]
    width = attn_out_g.shape[-1]
    alpha = (2 * depth) ** 0.25
    assert w_in.shape[-1] == 5 * width and seq % MOBA_BLOCK == 0 and width % LANES == 0
    assert min(MOBA_TOPK, seq // MOBA_BLOCK - 1) == MOBA_TOPK
    rope_hi, rope_lo = _rope_tables(seq)
    x2 = x.reshape(batch * seq, d_model)
    for l in range(depth):
        (q, k, v, u, z), (w_out_b, w_gate_b, w_up_b, w_down_b) = _in_proj(
            x2, w_in[l], rope_hi, rope_lo,
            gmlp_ln_g[l].reshape(1, width), gmlp_ln_b[l].reshape(1, width),
            (w_out[l], w_gate[l], w_up[l], w_down[l]), seq=seq, width=width, tm=IN_PROJ_TILE, n_sub=IN_PROJ_SUBTILES)
        attn = _attention(q, k, v, batch=batch, seq=seq)
        bias_full = jnp.repeat(b_spatial[l].T, HEAD_DIM, axis=1)
        x2 = _post_attention(attn, u, z, x2, w_spatial[l], bias_full,
                             attn_out_g[l].reshape(1, width), gmlp_out_g[l].reshape(1, width),
                             w_out_b, ln1_g[l].reshape(1, d_model), ln1_b[l].reshape(1, d_model),
                             w_gate_b, w_up_b, w_down_b,
                             ln2_g[l].reshape(1, d_model), ln2_b[l].reshape(1, d_model),
                             alpha=alpha, tm=POST_TILE, hc=FFN_CHUNK, n_sub=POST_SUBTILES)
    return x2.reshape(batch, seq, d_model)
```

```python
import functools

import jax
import jax.numpy as jnp
from jax import lax
from jax.experimental import pallas as pl
from jax.experimental.pallas import tpu as pltpu

HEAD_DIM = 64
MOBA_BLOCK = 256
MOBA_TOPK = 3
GMLP_CHUNK = 128
ROPE_THETA = 10000.0
ROPE_SPLIT = 64
LN_EPS = 1e-5
RMS_EPS = 1e-6
NEG_INF = -1e30
LANES = 128
ACC_ROWS = HEAD_DIM + 16
F32 = jnp.float32
BF16 = jnp.bfloat16

IN_PROJ_TILE, IN_PROJ_SUBTILES = 1024, 8
POST_TILE, POST_SUBTILES = 512, 2
FFN_CHUNK = 256
VMEM_LIMIT = 48 * 1024 * 1024
POST_VMEM_LIMIT = 56 * 1024 * 1024


def _erf_gelu(t):
    return 0.5 * t * (1.0 + lax.erf(t * 0.7071067811865476))


def _in_proj_kernel(x_ref, w32_ref, rope_hi_ref, rope_lo_ref, lng_ref, lnb_ref, *rest,
                    width, n_sub, n_cast):
    cast_in, rest = rest[:n_cast], rest[n_cast:]
    (q_ref, k_ref, v_ref, u_ref, z_ref), rest = rest[:5], rest[5:]
    cast_out, (w_ref,) = rest[:n_cast], rest[n_cast:]

    @pl.when(pl.program_id(0) == 0)
    def _():
        w_ref[...] = w32_ref[...].astype(BF16)

    lane = lax.broadcasted_iota(jnp.int32, (1, LANES), 1)
    lo = lane < HEAD_DIM
    sm = x_ref.shape[0] // n_sub
    n_seg = w_ref.shape[1] // width
    row_slices = [slice(s * sm, (s + 1) * sm) for s in range(n_sub)]

    first_half = (lane % HEAD_DIM) < HEAD_DIM // 2
    tables = {}

    def rope_tables(rows):
        if rows.start not in tables:
            cos_b, sin_b, cos_b_signed, sin_b_signed = (rope_lo_ref[i] for i in range(4))
            cos_rows, sin_rows = [], []
            for a in range(rows.start // ROPE_SPLIT, rows.stop // ROPE_SPLIT):
                cos_a, sin_a = rope_hi_ref[0, a:a + 1, :], rope_hi_ref[1, a:a + 1, :]
                cos_rows.append(cos_a * cos_b - sin_a * sin_b)
                sin_rows.append(sin_a * cos_b_signed + cos_a * sin_b_signed)
            tables[rows.start] = (jnp.concatenate(cos_rows, axis=0), jnp.concatenate(sin_rows, axis=0))
        return tables[rows.start]

    def rope(t, rows):
        cos, sin_signed = rope_tables(rows)
        rotated = jnp.where(first_half, pltpu.roll(t, 96, 1), pltpu.roll(t, 32, 1))
        return t * cos + rotated * sin_signed

    def group_norm(zs, sl):
        s_lo = jnp.sum(jnp.where(lo, zs, 0.0), axis=1, keepdims=True)
        s_hi = jnp.sum(jnp.where(lo, 0.0, zs), axis=1, keepdims=True)
        d = zs - jnp.where(lo, s_lo, s_hi) * (1.0 / HEAD_DIM)
        dd = d * d
        v_lo = jnp.sum(jnp.where(lo, dd, 0.0), axis=1, keepdims=True)
        v_hi = jnp.sum(jnp.where(lo, 0.0, dd), axis=1, keepdims=True)
        var = jnp.where(lo, v_lo, v_hi) * (1.0 / HEAD_DIM)
        return d * lax.rsqrt(var + LN_EPS) * lng_ref[:, sl] + lnb_ref[:, sl]

    def epilogue(seg, h, rows):
        if seg == 2:
            v_ref[rows, :] = h.astype(BF16)
        elif seg == 3:
            u_ref[rows, :] = _erf_gelu(h).astype(BF16)
        else:
            if seg == 4:
                h = _erf_gelu(h)
            for c in range(width // LANES):
                sl = slice(c * LANES, (c + 1) * LANES)
                if seg == 0:
                    q_ref[rows, sl] = rope(h[:, sl] * (HEAD_DIM ** -0.5), rows).astype(BF16)
                elif seg == 1:
                    k_ref[rows, sl] = rope(h[:, sl], rows).astype(BF16)
                else:
                    z_ref[rows, sl] = group_norm(h[:, sl], sl).astype(BF16)

    order = (4, 3, 0, 1, 2)
    xs = [None] * n_sub
    hs = [[None] * n_seg for _ in range(n_sub)]
    last = n_sub - 1
    for s in range(n_sub):
        xs[s] = x_ref[row_slices[s], :].astype(BF16)
        done = None
        for seg in order:
            hs[s][seg] = jnp.dot(xs[s], w_ref[:, seg * width:(seg + 1) * width],
                                 preferred_element_type=F32)
            if s >= 1:
                epilogue(seg, hs[s - 1][seg], row_slices[s - 1])
            if s == last and done is not None:
                epilogue(done, hs[s][done], row_slices[s])
            done = seg
        for src, dst in list(zip(cast_in, cast_out))[s::n_sub]:
            dst[...] = src[...].astype(BF16)
    epilogue(order[-1], hs[last][order[-1]], row_slices[last])


def _in_proj(x2, w_in, rope_hi, rope_lo, lng, lnb, later_weights, *, seq, width, tm, n_sub):
    tokens, d_model = x2.shape
    n_steps = tokens // tm
    n_pos_tiles = seq // tm
    assert (tm // n_sub) % ROPE_SPLIT == 0 and (tm // ROPE_SPLIT) % 8 == 0
    out = jax.ShapeDtypeStruct((tokens, width), BF16)
    tile = pl.BlockSpec((tm, width), lambda t: (t, 0))
    hi_spec = pl.BlockSpec((2, tm // ROPE_SPLIT, LANES), lambda t: (0, t % n_pos_tiles, 0))
    lo_spec = pl.BlockSpec(rope_lo.shape, lambda t: (0, 0, 0))
    row = pl.BlockSpec((1, width), lambda t: (0, 0))
    slabs = []
    for w in later_weights:
        assert w.shape[0] % (16 * n_steps) == 0
        slabs.append(pl.BlockSpec((w.shape[0] // n_steps, w.shape[1]), lambda t: (t, 0)))
    res = pl.pallas_call(
        functools.partial(_in_proj_kernel, width=width, n_sub=n_sub, n_cast=len(later_weights)),
        grid=(n_steps,),
        in_specs=[pl.BlockSpec((tm, d_model), lambda t: (t, 0)),
                  pl.BlockSpec(w_in.shape, lambda t: (0, 0), pipeline_mode=pl.Buffered(1)),
                  hi_spec, lo_spec, row, row, *slabs],
        out_specs=[tile] * 5 + slabs,
        out_shape=[out] * 5 + [jax.ShapeDtypeStruct(w.shape, BF16) for w in later_weights],
        scratch_shapes=[pltpu.VMEM(w_in.shape, BF16)],
        compiler_params=pltpu.CompilerParams(dimension_semantics=("arbitrary",),
                                             vmem_limit_bytes=VMEM_LIMIT),
        name="in_proj",
    )(x2, w_in, rope_hi, rope_lo, lng, lnb, *later_weights)
    return res[:5], res[5:]


def _aligned(start, multiple):
    return start if isinstance(start, int) else pl.multiple_of(start, multiple)


def _attn_kernel(q_ref, k_ref, v_ref, o_ref, ka_ref, vt_ref, km_ref, qa_ref,
                 s0_ref, s1_ref, m0_ref, m1_ref, acc0_ref, acc1_ref, *, nb):
    blk = MOBA_BLOCK
    sub = blk // 8
    s_refs, m_refs = (s0_ref, s1_ref), (m0_ref, m1_ref)
    acc_refs = (acc0_ref, acc1_ref)
    assert nb % 4 == 0
    lane = lax.broadcasted_iota(jnp.int32, (1, LANES), 1)
    head_lanes = [(lane >= HEAD_DIM * h) & (lane < HEAD_DIM * (h + 1)) for h in range(2)]
    bias_off = [HEAD_DIM, 0]

    km_pair = jnp.sum(k_ref[...].astype(F32).reshape(nb, blk, LANES), axis=1) * (1.0 / blk)
    for h in range(2):
        km = jnp.where(head_lanes[h], km_pair, 0.0)
        hi = km.astype(BF16)
        km_ref[(2 * h) * nb:(2 * h + 1) * nb, :] = hi
        km_ref[(2 * h + 1) * nb:(2 * h + 2) * nb, :] = (km - hi.astype(F32)).astype(BF16)
    for j in range(nb):
        rows = slice(j * blk, (j + 1) * blk)
        kj = k_ref[rows, :]
        for h in range(2):
            onehot = jnp.where(lane == bias_off[h] + j, 1.0, 0.0).astype(BF16)
            ka_ref[h, rows, :] = jnp.where(head_lanes[h], kj, onehot)

    def build_values(j):
        ones_row = jnp.where(lax.broadcasted_iota(jnp.int32, (ACC_ROWS - HEAD_DIM, blk), 0) == 0, 1.0, 0.0)
        v_t = v_ref[j * blk:(j + 1) * blk, :].astype(F32).T
        for h in range(2):
            vt_ref[j, h] = jnp.concatenate(
                [v_t[HEAD_DIM * h:HEAD_DIM * (h + 1)], ones_row], axis=0).astype(BF16)

    row = lax.broadcasted_iota(jnp.int32, (nb, blk), 0)
    rowf = row.astype(F32)
    kk = lax.broadcasted_iota(jnp.int32, (blk, blk), 0)
    qq = lax.broadcasted_iota(jnp.int32, (blk, blk), 1)

    def select_blocks(r):
        q_t = q_ref[r * blk:(r + 1) * blk, :].astype(F32).T.astype(BF16)
        g_all = jnp.dot(km_ref[...], q_t, preferred_element_type=F32)
        for h in range(2):
            g = g_all[(2 * h) * nb:(2 * h + 1) * nb] + g_all[(2 * h + 1) * nb:(2 * h + 2) * nb]
            past = row < r
            g = jnp.where(past, g, NEG_INF)
            bias = jnp.where(row == r, 0.0, NEG_INF)
            for _ in range(MOBA_TOPK):
                top = jnp.max(g, axis=0, keepdims=True)
                first = jnp.min(jnp.where(g == top, rowf, float(nb)), axis=0, keepdims=True)
                pick = rowf == first
                bias = jnp.where(pick & past, 0.0, bias)
                g = jnp.where(pick, -jnp.inf, g)
            q_h = q_t[HEAD_DIM * h:HEAD_DIM * (h + 1)]
            bias = bias.astype(BF16)
            pad = jnp.zeros((HEAD_DIM - nb, blk), BF16)
            pieces = [q_h, bias, pad] if h == 0 else [bias, pad, q_h]
            qa_ref[r, h] = jnp.concatenate(pieces, axis=0)

    def score_tile(slot, t, r, rs, j, causal):
        m_ref, s_ref = m_refs[slot], s_refs[slot]
        k0 = _aligned(j * blk, blk)
        for h in range(2):
            s = jnp.dot(ka_ref[h, pl.ds(k0, blk), :], qa_ref[r, h],
                        preferred_element_type=F32)
            if causal is not None:
                s = s + causal
            s_ref[t, h] = s
            m_ref[rs, h] = jnp.maximum(m_ref[rs, h], jnp.max(s.reshape(sub, 8, blk), axis=0))

    def value_tile(slot, t, rs, j):
        m_ref, s_ref, acc_ref = m_refs[slot], s_refs[slot], acc_refs[slot]
        for h in range(2):
            p = jnp.exp(s_ref[t, h].reshape(sub, 8, blk) - m_ref[rs, h]).reshape(blk, blk)
            acc_ref[rs, h] += jnp.dot(vt_ref[j, h], p.astype(BF16), preferred_element_type=F32)

    def past_tile(u, pi):
        if isinstance(pi, int):
            return (0, u) if u < pi else (1, u - pi)
        first = u < pi
        return jnp.where(first, 0, 1), jnp.where(first, u, u - pi)

    def tile_of(t, pi):
        rows = (pi, nb - 1 - pi)
        return (t, rows[t]) if t < 2 else past_tile(t - 2, pi)

    def write_rows(pair, slot, rs):
        outs = []
        for h in range(2):
            a = acc_refs[slot][rs, h]
            outs.append(a[:HEAD_DIM] / a[HEAD_DIM:HEAD_DIM + 1])
        q0 = _aligned((pair, nb - 1 - pair)[rs] * blk, blk)
        o_ref[pl.ds(q0, blk), :] = jnp.concatenate(outs, axis=0).T.astype(BF16)

    def step(k, slot):
        live = lambda p: not isinstance(p, int) or 0 <= p < nb // 2
        score_pair, value_pair, write_pair = k, k - 1, k - 2
        do_score, do_value, do_write = (live(p) for p in (score_pair, value_pair, write_pair))
        if do_score:
            causal = jnp.where(kk <= qq, 0.0, NEG_INF)
            for rs in range(2):
                for h in range(2):
                    m_refs[slot][rs, h] = jnp.full((8, blk), NEG_INF, F32)
        if do_value:
            for rs in range(2):
                for h in range(2):
                    acc_refs[1 - slot][rs, h] = jnp.zeros((ACC_ROWS, blk), F32)
        write_at = {1: 0, nb // 2: 1}
        for t in range(nb + 1):
            if do_score:
                rs, j = tile_of(t, score_pair)
                r = score_pair + rs * (nb - 1 - 2 * score_pair)
                score_tile(slot, t, r, rs, j, causal if t < 2 else None)
            if do_value:
                rs, j = tile_of(t, value_pair)
                value_tile(1 - slot, t, rs, j)
            if do_write and t in write_at:
                write_rows(write_pair, slot, write_at[t])
        if do_score:
            m_ref = m_refs[slot]
            for rs in range(2):
                for h in range(2):
                    m_ref[rs, h] = jnp.broadcast_to(jnp.max(m_ref[rs, h], axis=0, keepdims=True),
                                                    (8, blk))

    n_pairs = nb // 2
    for j in range(nb):
        build_values(j)
        select_blocks(j)
    step(0, 0)
    step(1, 1)

    def two_steps(i2, _):
        step(2 * i2 + 2, 0)
        step(2 * i2 + 3, 1)
        return 0

    n_rolled = (n_pairs - 2) // 2
    lax.fori_loop(0, n_rolled, two_steps, 0)
    for k in range(2 + 2 * n_rolled, n_pairs + 2):
        step(k, k % 2)


def _attention(q, k, v, *, batch, seq):
    width = q.shape[-1]
    nb = seq // MOBA_BLOCK
    q3, k3, v3 = (t.reshape(batch, seq, width) for t in (q, k, v))
    spec = pl.BlockSpec((None, seq, LANES), lambda b, p: (b, 0, p))
    out = pl.pallas_call(
        functools.partial(_attn_kernel, nb=nb),
        grid=(batch, width // LANES),
        in_specs=[spec, spec, spec],
        out_specs=spec,
        out_shape=jax.ShapeDtypeStruct((batch, seq, width), BF16),
        scratch_shapes=[pltpu.VMEM((2, seq, LANES), BF16),
                        pltpu.VMEM((nb, 2, ACC_ROWS, MOBA_BLOCK), BF16),
                        pltpu.VMEM((4 * nb, LANES), BF16),
                        pltpu.VMEM((nb, 2, LANES, MOBA_BLOCK), BF16),
                        *[pltpu.VMEM((nb + 1, 2, MOBA_BLOCK, MOBA_BLOCK), F32)] * 2,
                        *[pltpu.VMEM((2, 2, 8, MOBA_BLOCK), F32)] * 2,
                        *[pltpu.VMEM((2, 2, ACC_ROWS, MOBA_BLOCK), F32)] * 2],
        compiler_params=pltpu.CompilerParams(dimension_semantics=("arbitrary", "arbitrary"),
                                             vmem_limit_bytes=VMEM_LIMIT),
        name="moba_attention",
    )(q3, k3, v3)
    return out.reshape(batch * seq, width)


def _layer_norm(y, g, b):
    mu = jnp.mean(y, axis=-1, keepdims=True)
    d = y - mu
    var = jnp.mean(d * d, axis=-1, keepdims=True)
    return d * lax.rsqrt(var + LN_EPS) * g + b


def _rms_norm(t, g):
    ms = jnp.mean(t * t, axis=-1, keepdims=True)
    return t * lax.rsqrt(ms + RMS_EPS) * g


def _mix_pieces(attn_ref, u_ref, z_ref, x_ref, ws_ref, bs_ref, ag_ref, gg_ref, w_ref, lg_ref, lb_ref,
                sg_ref, store, *, alpha, width, n_sub):
    c = GMLP_CHUNK
    sm = x_ref.shape[0] // n_sub
    n_pairs = ws_ref.shape[0] // 2
    ii = lax.broadcasted_iota(jnp.int32, (c, c), 0)
    jj = lax.broadcasted_iota(jnp.int32, (c, c), 1)
    lane = lax.broadcasted_iota(jnp.int32, (1, LANES), 1)
    lo = lane < HEAD_DIM
    w_cat = [jnp.concatenate([jnp.where(jj <= ii, ws_ref[2 * p + g], 0.0) for g in range(2)],
                             axis=1).astype(BF16) for p in range(n_pairs)]

    def z_stack(rows, sl):
        z = z_ref[rows, sl]
        zero = jnp.zeros_like(z)
        return jnp.concatenate([jnp.where(lo, z, zero), jnp.where(lo, zero, z)], axis=0)

    def gate_and_norm(s):
        for c2 in range(sm // (2 * c)):
            r0 = s * sm + c2 * 2 * c
            rows = [slice(r0, r0 + c), slice(r0 + c, r0 + 2 * c)]
            for p in range(n_pairs):
                sl = slice(p * LANES, (p + 1) * LANES)
                rhs = jnp.concatenate([z_stack(rows[0], sl), z_stack(rows[1], sl)], axis=1)
                mixed = jnp.dot(w_cat[p], rhs, preferred_element_type=F32)
                for k in range(2):
                    sg_ref[rows[k], sl] = (u_ref[rows[k], sl].astype(F32)
                                           * (mixed[:, k * LANES:(k + 1) * LANES] + bs_ref[:, sl]))
        rows = slice(s * sm, (s + 1) * sm)
        an = _rms_norm(attn_ref[rows, :].astype(F32), ag_ref[...]).astype(BF16)
        sn = _rms_norm(sg_ref[rows, :], gg_ref[...]).astype(BF16)
        return an, sn

    def project(normed, cols):
        an, sn = normed
        return (jnp.dot(an, w_ref[:width, cols], preferred_element_type=F32)
                + jnp.dot(sn, w_ref[width:, cols], preferred_element_type=F32))

    def residual_norm(s, mix):
        rows = slice(s * sm, (s + 1) * sm)
        store(rows, _layer_norm(alpha * x_ref[rows, :] + mix, lg_ref[...], lb_ref[...]))

    d_model = w_ref.shape[1]
    col_halves = [slice(0, d_model // 2), slice(d_model // 2, d_model)]
    normed, lefts, mixes = {}, {}, {}

    def do_norm(s):
        normed[s] = gate_and_norm(s)

    def do_left(s):
        lefts[s] = project(normed[s], col_halves[0])

    def do_right(s):
        mixes[s] = jnp.concatenate([lefts.pop(s), project(normed.pop(s), col_halves[1])], axis=1)

    def do_out(s):
        residual_norm(s, mixes.pop(s))

    pieces = []
    for s in range(n_sub + 2):
        if s < n_sub:
            pieces.append(functools.partial(do_norm, s))
        if 1 <= s <= n_sub:
            pieces.append(functools.partial(do_left, s - 1))
        if 2 <= s:
            pieces.append(functools.partial(do_out, s - 2))
        if 1 <= s <= n_sub:
            pieces.append(functools.partial(do_right, s - 1))
    return pieces


def _post_kernel(attn_ref, u_ref, z_ref, x_ref, attn0_ref, u0_ref, z0_ref, x0_ref, ws_ref, bs_ref,
                 ag_ref, gg_ref, wo_ref, l1g_ref, l1b_ref, wg_ref, wu_ref, wd_ref, l2g_ref, l2b_ref,
                 o_ref, x1_ref, sg_ref, acc_ref, *, alpha, width, hc, n_sub):
    t = pl.program_id(0)
    slot = t % 2
    mix_refs = (ws_ref, bs_ref, ag_ref, gg_ref, wo_ref, l1g_ref, l1b_ref, sg_ref)
    mix_kw = dict(alpha=alpha, width=width, n_sub=n_sub)

    @pl.when(t == 0)
    def _():
        def store_first(rows, value):
            x1_ref[0, rows, :] = value
        for piece in _mix_pieces(attn0_ref, u0_ref, z0_ref, x0_ref, *mix_refs, store_first, **mix_kw):
            piece()

    def store_next(rows, value):
        x1_ref[1 - slot, rows, :] = value

    pieces = _mix_pieces(attn_ref, u_ref, z_ref, x_ref, *mix_refs, store_next, **mix_kw)
    hidden = wg_ref.shape[1]
    n_chunks = hidden // hc
    assert hidden % hc == 0 and len(pieces) <= n_chunks
    xb = x1_ref[slot].astype(BF16)
    for c in range(n_chunks):
        sl = slice(c * hc, (c + 1) * hc)
        g = jnp.dot(xb, wg_ref[:, sl], preferred_element_type=F32)
        u = jnp.dot(xb, wu_ref[:, sl], preferred_element_type=F32)
        h = (g / (1.0 + jnp.exp(-g)) * u).astype(BF16)
        part = jnp.dot(h, wd_ref[sl, :], preferred_element_type=F32)
        if c == 0:
            acc_ref[...] = part
        else:
            acc_ref[...] += part
        if c < len(pieces):
            pieces[c]()
    o_ref[...] = _layer_norm(alpha * x1_ref[slot] + acc_ref[...], l2g_ref[...], l2b_ref[...])


def _post_attention(attn, u, z, x2, w_spatial, bias_full, ag, gg, w_out, l1g, l1b,
                    w_gate, w_up, w_down, l2g, l2b, *, alpha, tm, hc, n_sub):
    tokens, width = attn.shape
    d_model = x2.shape[1]
    n_tiles = tokens // tm
    assert (tm // n_sub) % (2 * GMLP_CHUNK) == 0
    nxt = lambda t: (jnp.minimum(t + 1, n_tiles - 1), 0)
    once = dict(pipeline_mode=pl.Buffered(1))
    half_n = pl.BlockSpec((tm, width), nxt)
    full_n = pl.BlockSpec((tm, d_model), nxt)
    half_0 = pl.BlockSpec((tm, width), lambda t: (0, 0), **once)
    full_0 = pl.BlockSpec((tm, d_model), lambda t: (0, 0), **once)
    const = lambda a: pl.BlockSpec(a.shape, lambda t: (0,) * a.ndim, **once)
    return pl.pallas_call(
        functools.partial(_post_kernel, alpha=alpha, width=width, hc=hc, n_sub=n_sub),
        grid=(n_tiles,),
        in_specs=[half_n, half_n, half_n, full_n, half_0, half_0, half_0, full_0,
                  const(w_spatial), const(bias_full), const(ag), const(gg), const(w_out),
                  const(l1g), const(l1b), const(w_gate), const(w_up), const(w_down),
                  const(l2g), const(l2b)],
        out_specs=pl.BlockSpec((tm, d_model), lambda t: (t, 0)),
        out_shape=jax.ShapeDtypeStruct((tokens, d_model), F32),
        scratch_shapes=[pltpu.VMEM((2, tm, d_model), F32),
                        pltpu.VMEM((tm, width), F32),
                        pltpu.VMEM((tm, d_model), F32)],
        compiler_params=pltpu.CompilerParams(dimension_semantics=("arbitrary",),
                                             vmem_limit_bytes=POST_VMEM_LIMIT),
        name="mixer_out_ffn",
    )(attn, u, z, x2, attn, u, z, x2, w_spatial, bias_full, ag, gg, w_out, l1g, l1b,
      w_gate, w_up, w_down, l2g, l2b)


def _rope_tables(seq):
    assert seq % ROPE_SPLIT == 0
    inv_freq = ROPE_THETA ** (-jnp.arange(0, HEAD_DIM, 2, dtype=F32) / HEAD_DIM)
    tile = lambda t: jnp.concatenate([t, t, t, t], axis=-1)

    def cos_sin(pos):
        ang = pos.astype(F32)[:, None] * inv_freq[None, :]
        return tile(jnp.cos(ang)), tile(jnp.sin(ang))

    cos_hi, sin_hi = cos_sin(jnp.arange(0, seq, ROPE_SPLIT, dtype=jnp.int32))
    cos_lo, sin_lo = cos_sin(jnp.arange(ROPE_SPLIT, dtype=jnp.int32))
    sign = jnp.where((jnp.arange(LANES) % HEAD_DIM) < HEAD_DIM // 2, -1.0, 1.0)[None, :]
    return jnp.stack([cos_hi, sin_hi]), jnp.stack([cos_lo, sin_lo, sign * cos_lo, sign * sin_lo])


def kernel(x, w_in, attn_out_g, gmlp_out_g, gmlp_ln_g, gmlp_ln_b, w_spatial, b_spatial, w_out,
           ln1_g, ln1_b, w_gate, w_up, w_down, ln2_g, ln2_b):
    batch, seq, d_model = x.shape
    depth = w_in.shape[0]
    width = attn_out_g.shape[-1]
    alpha = (2 * depth) ** 0.25
    assert w_in.shape[-1] == 5 * width and seq % MOBA_BLOCK == 0 and width % LANES == 0
    assert min(MOBA_TOPK, seq // MOBA_BLOCK - 1) == MOBA_TOPK
    rope_hi, rope_lo = _rope_tables(seq)
    x2 = x.reshape(batch * seq, d_model)
    for l in range(depth):
        (q, k, v, u, z), (w_out_b, w_gate_b, w_up_b, w_down_b) = _in_proj(
            x2, w_in[l], rope_hi, rope_lo,
            gmlp_ln_g[l].reshape(1, width), gmlp_ln_b[l].reshape(1, width),
            (w_out[l], w_gate[l], w_up[l], w_down[l]), seq=seq, width=width, tm=IN_PROJ_TILE, n_sub=IN_PROJ_SUBTILES)
        attn = _attention(q, k, v, batch=batch, seq=seq)
        bias_full = jnp.repeat(b_spatial[l].T, HEAD_DIM, axis=1)
        x2 = _post_attention(attn, u, z, x2, w_spatial[l], bias_full,
                             attn_out_g[l].reshape(1, width), gmlp_out_g[l].reshape(1, width),
                             w_out_b, ln1_g[l].reshape(1, d_model), ln1_b[l].reshape(1, d_model),
                             w_gate_b, w_up_b, w_down_b,
                             ln2_g[l].reshape(1, d_model), ln2_b[l].reshape(1, d_model),
                             alpha=alpha, tm=POST_TILE, hc=FFN_CHUNK, n_sub=POST_SUBTILES)
    return x2.reshape(batch, seq, d_model)
```

```python
import functools

import jax
import jax.numpy as jnp
from jax import lax
from jax.experimental import pallas as pl
from jax.experimental.pallas import tpu as pltpu

HEAD_DIM = 64
MOBA_BLOCK = 256
MOBA_TOPK = 3
GMLP_CHUNK = 128
ROPE_THETA = 10000.0
ROPE_SPLIT = 64
LN_EPS = 1e-5
RMS_EPS = 1e-6
NEG_INF = -1e30
LANES = 128
ACC_ROWS = HEAD_DIM + 16
F32 = jnp.float32
BF16 = jnp.bfloat16

IN_PROJ_TILE, IN_PROJ_SUBTILES = 1024, 8
POST_TILE, POST_SUBTILES = 512, 2
FFN_CHUNK = 256
VMEM_LIMIT = 48 * 1024 * 1024
POST_VMEM_LIMIT = 56 * 1024 * 1024


def _erf_gelu(t):
    return 0.5 * t * (1.0 + lax.erf(t * 0.7071067811865476))


def _in_proj_kernel(x_ref, w32_ref, rope_hi_ref, rope_lo_ref, lng_ref, lnb_ref, *rest,
                    width, n_sub, n_cast, n_pos_tiles):
    cast_in, rest = rest[:n_cast], rest[n_cast:]
    (q_ref, ka_ref, v_ref, u_ref, z_ref, kmean_ref), rest = rest[:6], rest[6:]
    cast_out, (w_ref,) = rest[:n_cast], rest[n_cast:]
    first_block = lax.rem(pl.program_id(0), n_pos_tiles) * (x_ref.shape[0] // MOBA_BLOCK)

    @pl.when(pl.program_id(0) == 0)
    def _():
        w_ref[...] = w32_ref[...].astype(BF16)

    lane = lax.broadcasted_iota(jnp.int32, (1, LANES), 1)
    lo = lane < HEAD_DIM
    sm = x_ref.shape[0] // n_sub
    n_seg = w_ref.shape[1] // width
    row_slices = [slice(s * sm, (s + 1) * sm) for s in range(n_sub)]

    first_half = (lane % HEAD_DIM) < HEAD_DIM // 2
    tables = {}

    def rope_tables(rows):
        if rows.start not in tables:
            cos_b, sin_b, cos_b_signed, sin_b_signed = (rope_lo_ref[i] for i in range(4))
            cos_rows, sin_rows = [], []
            for a in range(rows.start // ROPE_SPLIT, rows.stop // ROPE_SPLIT):
                cos_a, sin_a = rope_hi_ref[0, a:a + 1, :], rope_hi_ref[1, a:a + 1, :]
                cos_rows.append(cos_a * cos_b - sin_a * sin_b)
                sin_rows.append(sin_a * cos_b_signed + cos_a * sin_b_signed)
            tables[rows.start] = (jnp.concatenate(cos_rows, axis=0), jnp.concatenate(sin_rows, axis=0))
        return tables[rows.start]

    def rope(t, rows):
        cos, sin_signed = rope_tables(rows)
        rotated = jnp.where(first_half, pltpu.roll(t, 96, 1), pltpu.roll(t, 32, 1))
        return t * cos + rotated * sin_signed

    def group_norm(zs, sl):
        s_lo = jnp.sum(jnp.where(lo, zs, 0.0), axis=1, keepdims=True)
        s_hi = jnp.sum(jnp.where(lo, 0.0, zs), axis=1, keepdims=True)
        d = zs - jnp.where(lo, s_lo, s_hi) * (1.0 / HEAD_DIM)
        dd = d * d
        v_lo = jnp.sum(jnp.where(lo, dd, 0.0), axis=1, keepdims=True)
        v_hi = jnp.sum(jnp.where(lo, 0.0, dd), axis=1, keepdims=True)
        var = jnp.where(lo, v_lo, v_hi) * (1.0 / HEAD_DIM)
        return d * lax.rsqrt(var + LN_EPS) * lng_ref[:, sl] + lnb_ref[:, sl]

    k_sums = {}

    def emit_keys(k_rot, rows, c):
        local = rows.start // MOBA_BLOCK
        kb = k_rot.astype(BF16)
        for hd in range(2):
            in_head = (lane >= HEAD_DIM * hd) & (lane < HEAD_DIM * (hd + 1))
            onehot = jnp.where(lane == HEAD_DIM * (1 - hd) + first_block + local, 1.0, 0.0)
            slab = slice((2 * c + hd) * LANES, (2 * c + hd + 1) * LANES)
            ka_ref[rows, slab] = jnp.where(in_head, kb, onehot.astype(BF16))
        part = jnp.sum(k_rot, axis=0, keepdims=True) * (1.0 / MOBA_BLOCK)
        key = (local, c)
        k_sums[key] = part if rows.start % MOBA_BLOCK == 0 else k_sums[key] + part
        if rows.stop % MOBA_BLOCK == 0:
            kmean_ref[local, :, c * LANES:(c + 1) * LANES] = jnp.broadcast_to(k_sums.pop(key), (8, LANES))

    def epilogue(seg, h, rows):
        if seg == 2:
            v_ref[rows, :] = h.astype(BF16)
        elif seg == 3:
            u_ref[rows, :] = _erf_gelu(h).astype(BF16)
        else:
            if seg == 4:
                h = _erf_gelu(h)
            for c in range(width // LANES):
                sl = slice(c * LANES, (c + 1) * LANES)
                if seg == 0:
                    q_ref[rows, sl] = rope(h[:, sl] * (HEAD_DIM ** -0.5), rows).astype(BF16)
                elif seg == 1:
                    emit_keys(rope(h[:, sl], rows), rows, c)
                else:
                    z_ref[rows, sl] = group_norm(h[:, sl], sl).astype(BF16)

    order = (4, 3, 0, 1, 2)
    xs = [None] * n_sub
    hs = [[None] * n_seg for _ in range(n_sub)]
    last = n_sub - 1
    for s in range(n_sub):
        xs[s] = x_ref[row_slices[s], :].astype(BF16)
        done = None
        for seg in order:
            hs[s][seg] = jnp.dot(xs[s], w_ref[:, seg * width:(seg + 1) * width],
                                 preferred_element_type=F32)
            if s >= 1:
                epilogue(seg, hs[s - 1][seg], row_slices[s - 1])
            if s == last and done is not None:
                epilogue(done, hs[s][done], row_slices[s])
            done = seg
        for src, dst in list(zip(cast_in, cast_out))[s::n_sub]:
            dst[...] = src[...].astype(BF16)
    epilogue(order[-1], hs[last][order[-1]], row_slices[last])


def _in_proj(x2, w_in, rope_hi, rope_lo, lng, lnb, later_weights, *, seq, width, tm, n_sub):
    tokens, d_model = x2.shape
    n_steps = tokens // tm
    n_pos_tiles = seq // tm
    assert (tm // n_sub) % ROPE_SPLIT == 0 and (tm // ROPE_SPLIT) % 8 == 0
    assert tm % MOBA_BLOCK == 0 and MOBA_BLOCK % (tm // n_sub) == 0
    out = jax.ShapeDtypeStruct((tokens, width), BF16)
    tile = pl.BlockSpec((tm, width), lambda t: (t, 0))
    keys_out = jax.ShapeDtypeStruct((tokens, 2 * width), BF16)
    keys_tile = pl.BlockSpec((tm, 2 * width), lambda t: (t, 0))
    mean_out = jax.ShapeDtypeStruct((tokens // MOBA_BLOCK, 8, width), F32)
    mean_tile = pl.BlockSpec((tm // MOBA_BLOCK, 8, width), lambda t: (t, 0, 0))
    hi_spec = pl.BlockSpec((2, tm // ROPE_SPLIT, LANES), lambda t: (0, t % n_pos_tiles, 0))
    lo_spec = pl.BlockSpec(rope_lo.shape, lambda t: (0, 0, 0))
    row = pl.BlockSpec((1, width), lambda t: (0, 0))
    slabs = []
    for w in later_weights:
        assert w.shape[0] % (16 * n_steps) == 0
        slabs.append(pl.BlockSpec((w.shape[0] // n_steps, w.shape[1]), lambda t: (t, 0)))
    res = pl.pallas_call(
        functools.partial(_in_proj_kernel, width=width, n_sub=n_sub, n_cast=len(later_weights),
                          n_pos_tiles=n_pos_tiles),
        grid=(n_steps,),
        in_specs=[pl.BlockSpec((tm, d_model), lambda t: (t, 0)),
                  pl.BlockSpec(w_in.shape, lambda t: (0, 0), pipeline_mode=pl.Buffered(1)),
                  hi_spec, lo_spec, row, row, *slabs],
        out_specs=[tile, keys_tile, tile, tile, tile, mean_tile] + slabs,
        out_shape=[out, keys_out, out, out, out, mean_out]
        + [jax.ShapeDtypeStruct(w.shape, BF16) for w in later_weights],
        scratch_shapes=[pltpu.VMEM(w_in.shape, BF16)],
        compiler_params=pltpu.CompilerParams(dimension_semantics=("arbitrary",),
                                             vmem_limit_bytes=VMEM_LIMIT),
        name="in_proj",
    )(x2, w_in, rope_hi, rope_lo, lng, lnb, *later_weights)
    return res[:6], res[6:]


def _aligned(start, multiple):
    return start if isinstance(start, int) else pl.multiple_of(start, multiple)


def _attn_kernel(q_ref, ka0_ref, ka1_ref, v_ref, kmean_ref, o_ref, vt_ref, km_ref, qa_ref,
                 s0_ref, s1_ref, m0_ref, m1_ref, acc0_ref, acc1_ref, *, nb):
    blk = MOBA_BLOCK
    sub = blk // 8
    s_refs, m_refs = (s0_ref, s1_ref), (m0_ref, m1_ref)
    acc_refs = (acc0_ref, acc1_ref)
    ka_refs = (ka0_ref, ka1_ref)
    assert nb % 4 == 0
    lane = lax.broadcasted_iota(jnp.int32, (1, LANES), 1)
    head_lanes = [(lane >= HEAD_DIM * h) & (lane < HEAD_DIM * (h + 1)) for h in range(2)]

    km_pair = jnp.max(kmean_ref[...], axis=1)
    for h in range(2):
        km = jnp.where(head_lanes[h], km_pair, 0.0)
        hi = km.astype(BF16)
        km_ref[(2 * h) * nb:(2 * h + 1) * nb, :] = hi
        km_ref[(2 * h + 1) * nb:(2 * h + 2) * nb, :] = (km - hi.astype(F32)).astype(BF16)

    def build_values(j):
        ones_row = jnp.where(lax.broadcasted_iota(jnp.int32, (ACC_ROWS - HEAD_DIM, blk), 0) == 0, 1.0, 0.0)
        v_t = v_ref[j * blk:(j + 1) * blk, :].astype(F32).T
        for h in range(2):
            vt_ref[j, h] = jnp.concatenate(
                [v_t[HEAD_DIM * h:HEAD_DIM * (h + 1)], ones_row], axis=0).astype(BF16)

    row = lax.broadcasted_iota(jnp.int32, (nb, blk), 0)
    rowf = row.astype(F32)
    kk = lax.broadcasted_iota(jnp.int32, (blk, blk), 0)
    qq = lax.broadcasted_iota(jnp.int32, (blk, blk), 1)

    def select_blocks(r):
        q_t = q_ref[r * blk:(r + 1) * blk, :].astype(F32).T.astype(BF16)
        few_past = r <= MOBA_TOPK
        if not few_past:
            g_all = jnp.dot(km_ref[...], q_t, preferred_element_type=F32)
        for h in range(2):
            past = row < r
            bias = jnp.where((row <= r) if few_past else (row == r), 0.0, NEG_INF)
            if not few_past:
                g = g_all[(2 * h) * nb:(2 * h + 1) * nb] + g_all[(2 * h + 1) * nb:(2 * h + 2) * nb]
                g = jnp.where(past, g, NEG_INF)
            for _ in range(0 if few_past else MOBA_TOPK):
                top = jnp.max(g, axis=0, keepdims=True)
                first = jnp.min(jnp.where(g == top, rowf, float(nb)), axis=0, keepdims=True)
                pick = rowf == first
                bias = jnp.where(pick & past, 0.0, bias)
                g = jnp.where(pick, -jnp.inf, g)
            q_h = q_t[HEAD_DIM * h:HEAD_DIM * (h + 1)]
            bias = bias.astype(BF16)
            pad = jnp.zeros((HEAD_DIM - nb, blk), BF16)
            pieces = [q_h, bias, pad] if h == 0 else [bias, pad, q_h]
            qa_ref[r, h] = jnp.concatenate(pieces, axis=0)

    def score_tile(slot, t, r, rs, j, causal):
        m_ref, s_ref = m_refs[slot], s_refs[slot]
        k0 = _aligned(j * blk, blk)
        for h in range(2):
            s = jnp.dot(ka_refs[h][pl.ds(k0, blk), :], qa_ref[r, h],
                        preferred_element_type=F32)
            if causal is not None:
                s = s + causal
            s_ref[t, h] = s
            m_ref[rs, h] = jnp.maximum(m_ref[rs, h], jnp.max(s.reshape(sub, 8, blk), axis=0))

    def value_tile(slot, t, rs, j):
        m_ref, s_ref, acc_ref = m_refs[slot], s_refs[slot], acc_refs[slot]
        for h in range(2):
            p = jnp.exp(s_ref[t, h].reshape(sub, 8, blk) - m_ref[rs, h]).reshape(blk, blk)
            acc_ref[rs, h] += jnp.dot(vt_ref[j, h], p.astype(BF16), preferred_element_type=F32)

    def past_tile(u, pi):
        if isinstance(pi, int):
            return (0, u) if u < pi else (1, u - pi)
        first = u < pi
        return jnp.where(first, 0, 1), jnp.where(first, u, u - pi)

    def tile_of(t, pi):
        rows = (pi, nb - 1 - pi)
        return (t, rows[t]) if t < 2 else past_tile(t - 2, pi)

    def write_rows(pair, slot, rs):
        outs = []
        for h in range(2):
            a = acc_refs[slot][rs, h]
            outs.append(a[:HEAD_DIM] / a[HEAD_DIM:HEAD_DIM + 1])
        q0 = _aligned((pair, nb - 1 - pair)[rs] * blk, blk)
        o_ref[pl.ds(q0, blk), :] = jnp.concatenate(outs, axis=0).T.astype(BF16)

    def step(k, slot):
        live = lambda p: not isinstance(p, int) or 0 <= p < nb // 2
        score_pair, value_pair, write_pair = k, k - 1, k - 2
        do_score, do_value, do_write = (live(p) for p in (score_pair, value_pair, write_pair))
        if do_score:
            causal = jnp.where(kk <= qq, 0.0, NEG_INF)
            for rs in range(2):
                for h in range(2):
                    m_refs[slot][rs, h] = jnp.full((8, blk), NEG_INF, F32)
        if do_value:
            for rs in range(2):
                for h in range(2):
                    acc_refs[1 - slot][rs, h] = jnp.zeros((ACC_ROWS, blk), F32)
        write_at = {1: 0, nb // 2: 1}
        for t in range(nb + 1):
            if do_score:
                rs, j = tile_of(t, score_pair)
                r = score_pair + rs * (nb - 1 - 2 * score_pair)
                score_tile(slot, t, r, rs, j, causal if t < 2 else None)
            if do_value:
                rs, j = tile_of(t, value_pair)
                value_tile(1 - slot, t, rs, j)
            if do_write and t in write_at:
                write_rows(write_pair, slot, write_at[t])
        if do_score:
            m_ref = m_refs[slot]
            for rs in range(2):
                for h in range(2):
                    m_ref[rs, h] = jnp.broadcast_to(jnp.max(m_ref[rs, h], axis=0, keepdims=True),
                                                    (8, blk))

    n_pairs = nb // 2
    for j in range(nb):
        build_values(j)
        select_blocks(j)
    step(0, 0)
    step(1, 1)

    def two_steps(i2, _):
        step(2 * i2 + 2, 0)
        step(2 * i2 + 3, 1)
        return 0

    n_rolled = (n_pairs - 2) // 2
    lax.fori_loop(0, n_rolled, two_steps, 0)
    for k in range(2 + 2 * n_rolled, n_pairs + 2):
        step(k, k % 2)


def _attention(q, keys, v, k_mean, *, batch, seq):
    width = q.shape[-1]
    nb = seq // MOBA_BLOCK
    q3, v3 = (t.reshape(batch, seq, width) for t in (q, v))
    keys3 = keys.reshape(batch, seq, 2 * width)
    spec = pl.BlockSpec((None, seq, LANES), lambda b, p: (b, 0, p))
    head_specs = [pl.BlockSpec((None, seq, LANES), lambda b, p, h=h: (b, 0, 2 * p + h)) for h in range(2)]
    mean_spec = pl.BlockSpec((nb, 8, LANES), lambda b, p: (b, 0, p))
    out = pl.pallas_call(
        functools.partial(_attn_kernel, nb=nb),
        grid=(batch, width // LANES),
        in_specs=[spec, *head_specs, spec, mean_spec],
        out_specs=spec,
        out_shape=jax.ShapeDtypeStruct((batch, seq, width), BF16),
        scratch_shapes=[pltpu.VMEM((nb, 2, ACC_ROWS, MOBA_BLOCK), BF16),
                        pltpu.VMEM((4 * nb, LANES), BF16),
                        pltpu.VMEM((nb, 2, LANES, MOBA_BLOCK), BF16),
                        *[pltpu.VMEM((nb + 1, 2, MOBA_BLOCK, MOBA_BLOCK), F32)] * 2,
                        *[pltpu.VMEM((2, 2, 8, MOBA_BLOCK), F32)] * 2,
                        *[pltpu.VMEM((2, 2, ACC_ROWS, MOBA_BLOCK), F32)] * 2],
        compiler_params=pltpu.CompilerParams(dimension_semantics=("arbitrary", "arbitrary"),
                                             vmem_limit_bytes=VMEM_LIMIT),
        name="moba_attention",
    )(q3, keys3, keys3, v3, k_mean)
    return out.reshape(batch * seq, width)


def _layer_norm(y, g, b):
    mu = jnp.mean(y, axis=-1, keepdims=True)
    d = y - mu
    var = jnp.mean(d * d, axis=-1, keepdims=True)
    return d * lax.rsqrt(var + LN_EPS) * g + b


def _rms_norm(t, g):
    ms = jnp.mean(t * t, axis=-1, keepdims=True)
    return t * lax.rsqrt(ms + RMS_EPS) * g


def _mix_pieces(attn_ref, u_ref, z_ref, x_ref, ws_ref, bs_ref, ag_ref, gg_ref, w_ref, lg_ref, lb_ref,
                sg_ref, store, *, alpha, width, n_sub):
    c = GMLP_CHUNK
    sm = x_ref.shape[0] // n_sub
    n_pairs = ws_ref.shape[0] // 2
    ii = lax.broadcasted_iota(jnp.int32, (c, c), 0)
    jj = lax.broadcasted_iota(jnp.int32, (c, c), 1)
    lane = lax.broadcasted_iota(jnp.int32, (1, LANES), 1)
    lo = lane < HEAD_DIM
    w_cat = [jnp.concatenate([jnp.where(jj <= ii, ws_ref[2 * p + g], 0.0) for g in range(2)],
                             axis=1).astype(BF16) for p in range(n_pairs)]

    def z_stack(rows, sl):
        z = z_ref[rows, sl]
        zero = jnp.zeros_like(z)
        return jnp.concatenate([jnp.where(lo, z, zero), jnp.where(lo, zero, z)], axis=0)

    def gate_and_norm(s):
        for c2 in range(sm // (2 * c)):
            r0 = s * sm + c2 * 2 * c
            rows = [slice(r0, r0 + c), slice(r0 + c, r0 + 2 * c)]
            for p in range(n_pairs):
                sl = slice(p * LANES, (p + 1) * LANES)
                rhs = jnp.concatenate([z_stack(rows[0], sl), z_stack(rows[1], sl)], axis=1)
                mixed = jnp.dot(w_cat[p], rhs, preferred_element_type=F32)
                for k in range(2):
                    sg_ref[rows[k], sl] = (u_ref[rows[k], sl].astype(F32)
                                           * (mixed[:, k * LANES:(k + 1) * LANES] + bs_ref[:, sl]))
        rows = slice(s * sm, (s + 1) * sm)
        an = _rms_norm(attn_ref[rows, :].astype(F32), ag_ref[...]).astype(BF16)
        sn = _rms_norm(sg_ref[rows, :], gg_ref[...]).astype(BF16)
        return an, sn

    def project(normed, cols):
        an, sn = normed
        return (jnp.dot(an, w_ref[:width, cols], preferred_element_type=F32)
                + jnp.dot(sn, w_ref[width:, cols], preferred_element_type=F32))

    def residual_norm(s, mix):
        rows = slice(s * sm, (s + 1) * sm)
        store(rows, _layer_norm(alpha * x_ref[rows, :] + mix, lg_ref[...], lb_ref[...]))

    d_model = w_ref.shape[1]
    col_halves = [slice(0, d_model // 2), slice(d_model // 2, d_model)]
    normed, lefts, mixes = {}, {}, {}

    def do_norm(s):
        normed[s] = gate_and_norm(s)

    def do_left(s):
        lefts[s] = project(normed[s], col_halves[0])

    def do_right(s):
        mixes[s] = jnp.concatenate([lefts.pop(s), project(normed.pop(s), col_halves[1])], axis=1)

    def do_out(s):
        residual_norm(s, mixes.pop(s))

    pieces = []
    for s in range(n_sub + 2):
        if s < n_sub:
            pieces.append(functools.partial(do_norm, s))
        if 1 <= s <= n_sub:
            pieces.append(functools.partial(do_left, s - 1))
        if 2 <= s:
            pieces.append(functools.partial(do_out, s - 2))
        if 1 <= s <= n_sub:
            pieces.append(functools.partial(do_right, s - 1))
    return pieces


def _post_kernel(attn_ref, u_ref, z_ref, x_ref, attn0_ref, u0_ref, z0_ref, x0_ref, ws_ref, bs_ref,
                 ag_ref, gg_ref, wo_ref, l1g_ref, l1b_ref, wg_ref, wu_ref, wd_ref, l2g_ref, l2b_ref,
                 o_ref, x1_ref, sg_ref, acc_ref, *, alpha, width, hc, n_sub):
    t = pl.program_id(0)
    slot = t % 2
    mix_refs = (ws_ref, bs_ref, ag_ref, gg_ref, wo_ref, l1g_ref, l1b_ref, sg_ref)
    mix_kw = dict(alpha=alpha, width=width, n_sub=n_sub)

    @pl.when(t == 0)
    def _():
        def store_first(rows, value):
            x1_ref[0, rows, :] = value
        for piece in _mix_pieces(attn0_ref, u0_ref, z0_ref, x0_ref, *mix_refs, store_first, **mix_kw):
            piece()

    def store_next(rows, value):
        x1_ref[1 - slot, rows, :] = value

    pieces = _mix_pieces(attn_ref, u_ref, z_ref, x_ref, *mix_refs, store_next, **mix_kw)
    hidden = wg_ref.shape[1]
    n_chunks = hidden // hc
    assert hidden % hc == 0 and len(pieces) <= n_chunks
    xb = x1_ref[slot].astype(BF16)
    for c in range(n_chunks):
        sl = slice(c * hc, (c + 1) * hc)
        g = jnp.dot(xb, wg_ref[:, sl], preferred_element_type=F32)
        u = jnp.dot(xb, wu_ref[:, sl], preferred_element_type=F32)
        h = (g / (1.0 + jnp.exp(-g)) * u).astype(BF16)
        part = jnp.dot(h, wd_ref[sl, :], preferred_element_type=F32)
        if c == 0:
            acc_ref[...] = part
        else:
            acc_ref[...] += part
        if c < len(pieces):
            pieces[c]()
    o_ref[...] = _layer_norm(alpha * x1_ref[slot] + acc_ref[...], l2g_ref[...], l2b_ref[...])


def _post_attention(attn, u, z, x2, w_spatial, bias_full, ag, gg, w_out, l1g, l1b,
                    w_gate, w_up, w_down, l2g, l2b, *, alpha, tm, hc, n_sub):
    tokens, width = attn.shape
    d_model = x2.shape[1]
    n_tiles = tokens // tm
    assert (tm // n_sub) % (2 * GMLP_CHUNK) == 0
    nxt = lambda t: (jnp.minimum(t + 1, n_tiles - 1), 0)
    once = dict(pipeline_mode=pl.Buffered(1))
    half_n = pl.BlockSpec((tm, width), nxt)
    full_n = pl.BlockSpec((tm, d_model), nxt)
    half_0 = pl.BlockSpec((tm, width), lambda t: (0, 0), **once)
    full_0 = pl.BlockSpec((tm, d_model), lambda t: (0, 0), **once)
    const = lambda a: pl.BlockSpec(a.shape, lambda t: (0,) * a.ndim, **once)
    return pl.pallas_call(
        functools.partial(_post_kernel, alpha=alpha, width=width, hc=hc, n_sub=n_sub),
        grid=(n_tiles,),
        in_specs=[half_n, half_n, half_n, full_n, half_0, half_0, half_0, full_0,
                  const(w_spatial), const(bias_full), const(ag), const(gg), const(w_out),
                  const(l1g), const(l1b), const(w_gate), const(w_up), const(w_down),
                  const(l2g), const(l2b)],
        out_specs=pl.BlockSpec((tm, d_model), lambda t: (t, 0)),
        out_shape=jax.ShapeDtypeStruct((tokens, d_model), F32),
        scratch_shapes=[pltpu.VMEM((2, tm, d_model), F32),
                        pltpu.VMEM((tm, width), F32),
                        pltpu.VMEM((tm, d_model), F32)],
        compiler_params=pltpu.CompilerParams(dimension_semantics=("arbitrary",),
                                             vmem_limit_bytes=POST_VMEM_LIMIT),
        name="mixer_out_ffn",
    )(attn, u, z, x2, attn, u, z, x2, w_spatial, bias_full, ag, gg, w_out, l1g, l1b,
      w_gate, w_up, w_down, l2g, l2b)


def _rope_tables(seq):
    assert seq % ROPE_SPLIT == 0
    inv_freq = ROPE_THETA ** (-jnp.arange(0, HEAD_DIM, 2, dtype=F32) / HEAD_DIM)
    tile = lambda t: jnp.concatenate([t, t, t, t], axis=-1)

    def cos_sin(pos):
        ang = pos.astype(F32)[:, None] * inv_freq[None, :]
        return tile(jnp.cos(ang)), tile(jnp.sin(ang))

    cos_hi, sin_hi = cos_sin(jnp.arange(0, seq, ROPE_SPLIT, dtype=jnp.int32))
    cos_lo, sin_lo = cos_sin(jnp.arange(ROPE_SPLIT, dtype=jnp.int32))
    sign = jnp.where((jnp.arange(LANES) % HEAD_DIM) < HEAD_DIM // 2, -1.0, 1.0)[None, :]
    return jnp.stack([cos_hi, sin_hi]), jnp.stack([cos_lo, sin_lo, sign * cos_lo, sign * sin_lo])


def kernel(x, w_in, attn_out_g, gmlp_out_g, gmlp_ln_g, gmlp_ln_b, w_spatial, b_spatial, w_out,
           ln1_g, ln1_b, w_gate, w_up, w_down, ln2_g, ln2_b):
    batch, seq, d_model = x.shape
    depth = w_in.shape[0]
    width = attn_out_g.shape[-1]
    alpha = (2 * depth) ** 0.25
    assert w_in.shape[-1] == 5 * width and seq % MOBA_BLOCK == 0 and width % LANES == 0
    assert min(MOBA_TOPK, seq // MOBA_BLOCK - 1) == MOBA_TOPK
    rope_hi, rope_lo = _rope_tables(seq)
    x2 = x.reshape(batch * seq, d_model)
    for l in range(depth):
        (q, keys, v, u, z, k_mean), (w_out_b, w_gate_b, w_up_b, w_down_b) = _in_proj(
            x2, w_in[l], rope_hi, rope_lo,
            gmlp_ln_g[l].reshape(1, width), gmlp_ln_b[l].reshape(1, width),
            (w_out[l], w_gate[l], w_up[l], w_down[l]), seq=seq, width=width, tm=IN_PROJ_TILE, n_sub=IN_PROJ_SUBTILES)
        attn = _attention(q, keys, v, k_mean, batch=batch, seq=seq)
        bias_full = jnp.repeat(b_spatial[l].T, HEAD_DIM, axis=1)
        x2 = _post_attention(attn, u, z, x2, w_spatial[l], bias_full,
                             attn_out_g[l].reshape(1, width), gmlp_out_g[l].reshape(1, width),
                             w_out_b, ln1_g[l].reshape(1, d_model), ln1_b[l].reshape(1, d_model),
                             w_gate_b, w_up_b, w_down_b,
                             ln2_g[l].reshape(1, d_model), ln2_b[l].reshape(1, d_model),
                             alpha=alpha, tm=POST_TILE, hc=FFN_CHUNK, n_sub=POST_SUBTILES)
    return x2.reshape(batch, seq, d_model)
```

```python
import functools

import jax
import jax.numpy as jnp
from jax import lax
from jax.experimental import pallas as pl
from jax.experimental.pallas import tpu as pltpu

HEAD_DIM = 64
MOBA_BLOCK = 256
MOBA_TOPK = 3
GMLP_CHUNK = 128
ROPE_THETA = 10000.0
ROPE_SPLIT = 64
LN_EPS = 1e-5
RMS_EPS = 1e-6
NEG_INF = -1e30
LANES = 128
ACC_ROWS = HEAD_DIM + 16
F32 = jnp.float32
BF16 = jnp.bfloat16

IN_PROJ_TILE, IN_PROJ_SUBTILES = 1024, 8
POST_TILE, POST_SUBTILES = 512, 2
FFN_CHUNK = 256
VMEM_LIMIT = 48 * 1024 * 1024
POST_VMEM_LIMIT = 56 * 1024 * 1024


def _erf_gelu(t):
    return 0.5 * t * (1.0 + lax.erf(t * 0.7071067811865476))


def _in_proj_kernel(x_ref, w32_ref, rope_hi_ref, rope_lo_ref, lng_ref, lnb_ref, *rest,
                    width, n_sub, n_cast, n_pos_tiles):
    cast_in, rest = rest[:n_cast], rest[n_cast:]
    (qa_ref, ka_ref, v_ref, u_ref, z_ref), rest = rest[:5], rest[5:]
    cast_out, (w_ref, kmean_ref, qs_ref) = rest[:n_cast], rest[n_cast:]
    first_block = lax.rem(pl.program_id(0), n_pos_tiles) * (x_ref.shape[0] // MOBA_BLOCK)
    nb = kmean_ref.shape[0]

    @pl.when(pl.program_id(0) == 0)
    def _():
        w_ref[...] = w32_ref[...].astype(BF16)
        kmean_ref[...] = jnp.zeros(kmean_ref.shape, F32)

    lane = lax.broadcasted_iota(jnp.int32, (1, LANES), 1)
    lo = lane < HEAD_DIM
    sm = x_ref.shape[0] // n_sub
    n_seg = w_ref.shape[1] // width
    row_slices = [slice(s * sm, (s + 1) * sm) for s in range(n_sub)]

    first_half = (lane % HEAD_DIM) < HEAD_DIM // 2
    tables = {}

    def rope_tables(rows):
        if rows.start not in tables:
            cos_b, sin_b, cos_b_signed, sin_b_signed = (rope_lo_ref[i] for i in range(4))
            cos_rows, sin_rows = [], []
            for a in range(rows.start // ROPE_SPLIT, rows.stop // ROPE_SPLIT):
                cos_a, sin_a = rope_hi_ref[0, a:a + 1, :], rope_hi_ref[1, a:a + 1, :]
                cos_rows.append(cos_a * cos_b - sin_a * sin_b)
                sin_rows.append(sin_a * cos_b_signed + cos_a * sin_b_signed)
            tables[rows.start] = (jnp.concatenate(cos_rows, axis=0), jnp.concatenate(sin_rows, axis=0))
        return tables[rows.start]

    def rope(t, rows):
        cos, sin_signed = rope_tables(rows)
        rotated = jnp.where(first_half, pltpu.roll(t, 96, 1), pltpu.roll(t, 32, 1))
        return t * cos + rotated * sin_signed

    def group_norm(zs, sl):
        s_lo = jnp.sum(jnp.where(lo, zs, 0.0), axis=1, keepdims=True)
        s_hi = jnp.sum(jnp.where(lo, 0.0, zs), axis=1, keepdims=True)
        d = zs - jnp.where(lo, s_lo, s_hi) * (1.0 / HEAD_DIM)
        dd = d * d
        v_lo = jnp.sum(jnp.where(lo, dd, 0.0), axis=1, keepdims=True)
        v_hi = jnp.sum(jnp.where(lo, 0.0, dd), axis=1, keepdims=True)
        var = jnp.where(lo, v_lo, v_hi) * (1.0 / HEAD_DIM)
        return d * lax.rsqrt(var + LN_EPS) * lng_ref[:, sl] + lnb_ref[:, sl]

    k_sums, pending = {}, []
    head_lanes = [(lane >= HEAD_DIM * hd) & (lane < HEAD_DIM * (hd + 1)) for hd in range(2)]
    blk_row = lax.broadcasted_iota(jnp.int32, (nb, MOBA_BLOCK), 0)
    blk_rowf = blk_row.astype(F32)

    def emit_keys(k_rot, rows, c):
        local = rows.start // MOBA_BLOCK
        kb = k_rot.astype(BF16)
        for hd in range(2):
            onehot = jnp.where(lane == HEAD_DIM * (1 - hd) + first_block + local, 1.0, 0.0)
            slab = slice((2 * c + hd) * LANES, (2 * c + hd + 1) * LANES)
            ka_ref[rows, slab] = jnp.where(head_lanes[hd], kb, onehot.astype(BF16))
        part = jnp.sum(k_rot, axis=0, keepdims=True) * (1.0 / MOBA_BLOCK)
        key = (local, c)
        k_sums[key] = part if rows.start % MOBA_BLOCK == 0 else k_sums[key] + part
        if rows.stop % MOBA_BLOCK == 0:
            sl = slice(c * LANES, (c + 1) * LANES)
            kmean_ref[:, sl] = jnp.where(blk_row[:, :LANES] == first_block + local, k_sums.pop(key),
                                         kmean_ref[:, sl])

    def emit_queries(q_rot, rows, c):
        qs_ref[rows, c * LANES:(c + 1) * LANES] = q_rot
        if rows.stop % MOBA_BLOCK == 0:
            pending.append(functools.partial(select_blocks, rows.start // MOBA_BLOCK, c))

    def select_blocks(local, c):
        block_rows = slice(local * MOBA_BLOCK, (local + 1) * MOBA_BLOCK)
        q_t = qs_ref[block_rows, c * LANES:(c + 1) * LANES].T.astype(BF16)
        r = first_block + local
        km_pair = kmean_ref[:, c * LANES:(c + 1) * LANES]
        stack = []
        for hd in range(2):
            km = jnp.where(head_lanes[hd], km_pair, 0.0)
            hi = km.astype(BF16)
            stack += [hi, (km - hi.astype(F32)).astype(BF16)]
        g_all = jnp.dot(jnp.concatenate(stack, axis=0), q_t, preferred_element_type=F32)
        past = blk_row < r
        for hd in range(2):
            g = g_all[(2 * hd) * nb:(2 * hd + 1) * nb] + g_all[(2 * hd + 1) * nb:(2 * hd + 2) * nb]
            g = jnp.where(past, g, NEG_INF)
            bias = jnp.where(blk_row == r, 0.0, NEG_INF)
            for _ in range(MOBA_TOPK):
                top = jnp.max(g, axis=0, keepdims=True)
                first = jnp.min(jnp.where(g == top, blk_rowf, float(nb)), axis=0, keepdims=True)
                pick = blk_rowf == first
                bias = jnp.where(pick & past, 0.0, bias)
                g = jnp.where(pick, -jnp.inf, g)
            q_h = q_t[HEAD_DIM * hd:HEAD_DIM * (hd + 1)]
            pad = jnp.zeros((HEAD_DIM - nb, MOBA_BLOCK), BF16)
            pieces = [q_h, bias.astype(BF16), pad] if hd == 0 else [bias.astype(BF16), pad, q_h]
            qa_ref[local, c, hd] = jnp.concatenate(pieces, axis=0)

    def epilogue(seg, h, rows):
        if seg == 2:
            v_ref[rows, :] = h.astype(BF16)
        elif seg == 3:
            u_ref[rows, :] = _erf_gelu(h).astype(BF16)
        else:
            if seg == 4:
                h = _erf_gelu(h)
            for c in range(width // LANES):
                sl = slice(c * LANES, (c + 1) * LANES)
                if seg == 0:
                    emit_queries(rope(h[:, sl] * (HEAD_DIM ** -0.5), rows), rows, c)
                elif seg == 1:
                    emit_keys(rope(h[:, sl], rows), rows, c)
                else:
                    z_ref[rows, sl] = group_norm(h[:, sl], sl).astype(BF16)

    order = (0, 4, 3, 1, 2)
    xs = [None] * n_sub
    hs = [[None] * n_seg for _ in range(n_sub)]
    last = n_sub - 1
    for s in range(n_sub):
        xs[s] = x_ref[row_slices[s], :].astype(BF16)
        done = None
        for seg in order:
            hs[s][seg] = jnp.dot(xs[s], w_ref[:, seg * width:(seg + 1) * width],
                                 preferred_element_type=F32)
            if s >= 1:
                epilogue(seg, hs[s - 1][seg], row_slices[s - 1])
            if s == last and done is not None:
                epilogue(done, hs[s][done], row_slices[s])
            if pending:
                pending.pop(0)()
            done = seg
        for src, dst in list(zip(cast_in, cast_out))[s::n_sub]:
            dst[...] = src[...].astype(BF16)
    epilogue(order[-1], hs[last][order[-1]], row_slices[last])
    assert not pending


def _in_proj(x2, w_in, rope_hi, rope_lo, lng, lnb, later_weights, *, seq, width, tm, n_sub):
    tokens, d_model = x2.shape
    n_steps = tokens // tm
    n_pos_tiles = seq // tm
    assert (tm // n_sub) % ROPE_SPLIT == 0 and (tm // ROPE_SPLIT) % 8 == 0
    assert tm % MOBA_BLOCK == 0 and MOBA_BLOCK % (tm // n_sub) == 0
    out = jax.ShapeDtypeStruct((tokens, width), BF16)
    tile = pl.BlockSpec((tm, width), lambda t: (t, 0))
    keys_out = jax.ShapeDtypeStruct((tokens, 2 * width), BF16)
    keys_tile = pl.BlockSpec((tm, 2 * width), lambda t: (t, 0))
    assert seq // MOBA_BLOCK <= HEAD_DIM
    qa_dims = (width // LANES, 2, LANES, MOBA_BLOCK)
    qa_out = jax.ShapeDtypeStruct((tokens // MOBA_BLOCK, *qa_dims), BF16)
    qa_tile = pl.BlockSpec((tm // MOBA_BLOCK, *qa_dims), lambda t: (t, 0, 0, 0, 0))
    hi_spec = pl.BlockSpec((2, tm // ROPE_SPLIT, LANES), lambda t: (0, t % n_pos_tiles, 0))
    lo_spec = pl.BlockSpec(rope_lo.shape, lambda t: (0, 0, 0))
    row = pl.BlockSpec((1, width), lambda t: (0, 0))
    slabs = []
    for w in later_weights:
        assert w.shape[0] % (16 * n_steps) == 0
        slabs.append(pl.BlockSpec((w.shape[0] // n_steps, w.shape[1]), lambda t: (t, 0)))
    res = pl.pallas_call(
        functools.partial(_in_proj_kernel, width=width, n_sub=n_sub, n_cast=len(later_weights),
                          n_pos_tiles=n_pos_tiles),
        grid=(n_steps,),
        in_specs=[pl.BlockSpec((tm, d_model), lambda t: (t, 0)),
                  pl.BlockSpec(w_in.shape, lambda t: (0, 0), pipeline_mode=pl.Buffered(1)),
                  hi_spec, lo_spec, row, row, *slabs],
        out_specs=[qa_tile, keys_tile, tile, tile, tile] + slabs,
        out_shape=[qa_out, keys_out, out, out, out]
        + [jax.ShapeDtypeStruct(w.shape, BF16) for w in later_weights],
        scratch_shapes=[pltpu.VMEM(w_in.shape, BF16),
                        pltpu.VMEM((seq // MOBA_BLOCK, width), F32),
                        pltpu.VMEM((tm, width), F32)],
        compiler_params=pltpu.CompilerParams(dimension_semantics=("arbitrary",),
                                             vmem_limit_bytes=VMEM_LIMIT),
        name="in_proj",
    )(x2, w_in, rope_hi, rope_lo, lng, lnb, *later_weights)
    return res[:5], res[5:]


def _aligned(start, multiple):
    return start if isinstance(start, int) else pl.multiple_of(start, multiple)


def _attn_kernel(qa_ref, ka0_ref, ka1_ref, v_ref, o_ref, vt_ref,
                 s0_ref, s1_ref, m0_ref, m1_ref, acc0_ref, acc1_ref, *, nb):
    blk = MOBA_BLOCK
    sub = blk // 8
    s_refs, m_refs = (s0_ref, s1_ref), (m0_ref, m1_ref)
    acc_refs = (acc0_ref, acc1_ref)
    ka_refs = (ka0_ref, ka1_ref)
    assert nb % 4 == 0

    def build_values(j):
        ones_row = jnp.where(lax.broadcasted_iota(jnp.int32, (ACC_ROWS - HEAD_DIM, blk), 0) == 0, 1.0, 0.0)
        v_t = v_ref[j * blk:(j + 1) * blk, :].astype(F32).T
        for h in range(2):
            vt_ref[j, h] = jnp.concatenate(
                [v_t[HEAD_DIM * h:HEAD_DIM * (h + 1)], ones_row], axis=0).astype(BF16)

    kk = lax.broadcasted_iota(jnp.int32, (blk, blk), 0)
    qq = lax.broadcasted_iota(jnp.int32, (blk, blk), 1)

    def score_tile(slot, t, r, rs, j, causal):
        m_ref, s_ref = m_refs[slot], s_refs[slot]
        k0 = _aligned(j * blk, blk)
        for h in range(2):
            s = jnp.dot(ka_refs[h][pl.ds(k0, blk), :], qa_ref[r, h],
                        preferred_element_type=F32)
            if causal is not None:
                s = s + causal
            s_ref[t, h] = s
            m_ref[rs, h] = jnp.maximum(m_ref[rs, h], jnp.max(s.reshape(sub, 8, blk), axis=0))

    def value_tile(slot, t, rs, j):
        m_ref, s_ref, acc_ref = m_refs[slot], s_refs[slot], acc_refs[slot]
        for h in range(2):
            p = jnp.exp(s_ref[t, h].reshape(sub, 8, blk) - m_ref[rs, h]).reshape(blk, blk)
            acc_ref[rs, h] += jnp.dot(vt_ref[j, h], p.astype(BF16), preferred_element_type=F32)

    def past_tile(u, pi):
        if isinstance(pi, int):
            return (0, u) if u < pi else (1, u - pi)
        first = u < pi
        return jnp.where(first, 0, 1), jnp.where(first, u, u - pi)

    def tile_of(t, pi):
        rows = (pi, nb - 1 - pi)
        return (t, rows[t]) if t < 2 else past_tile(t - 2, pi)

    def write_rows(pair, slot, rs):
        outs = []
        for h in range(2):
            a = acc_refs[slot][rs, h]
            outs.append(a[:HEAD_DIM] / a[HEAD_DIM:HEAD_DIM + 1])
        q0 = _aligned((pair, nb - 1 - pair)[rs] * blk, blk)
        o_ref[pl.ds(q0, blk), :] = jnp.concatenate(outs, axis=0).T.astype(BF16)

    def step(k, slot):
        live = lambda p: not isinstance(p, int) or 0 <= p < nb // 2
        score_pair, value_pair, write_pair = k, k - 1, k - 2
        do_score, do_value, do_write = (live(p) for p in (score_pair, value_pair, write_pair))
        if do_score:
            causal = jnp.where(kk <= qq, 0.0, NEG_INF)
            for rs in range(2):
                for h in range(2):
                    m_refs[slot][rs, h] = jnp.full((8, blk), NEG_INF, F32)
        if do_value:
            for rs in range(2):
                for h in range(2):
                    acc_refs[1 - slot][rs, h] = jnp.zeros((ACC_ROWS, blk), F32)
        write_at = {1: 0, nb // 2: 1}
        for t in range(nb + 1):
            if do_score:
                rs, j = tile_of(t, score_pair)
                r = score_pair + rs * (nb - 1 - 2 * score_pair)
                score_tile(slot, t, r, rs, j, causal if t < 2 else None)
            if do_value:
                rs, j = tile_of(t, value_pair)
                value_tile(1 - slot, t, rs, j)
            if do_write and t in write_at:
                write_rows(write_pair, slot, write_at[t])
        if do_score:
            m_ref = m_refs[slot]
            for rs in range(2):
                for h in range(2):
                    m_ref[rs, h] = jnp.broadcast_to(jnp.max(m_ref[rs, h], axis=0, keepdims=True),
                                                    (8, blk))

    n_pairs = nb // 2
    for j in range(nb):
        build_values(j)
    step(0, 0)
    step(1, 1)

    def two_steps(i2, _):
        step(2 * i2 + 2, 0)
        step(2 * i2 + 3, 1)
        return 0

    n_rolled = (n_pairs - 2) // 2
    lax.fori_loop(0, n_rolled, two_steps, 0)
    for k in range(2 + 2 * n_rolled, n_pairs + 2):
        step(k, k % 2)


def _attention(qa, keys, v, *, batch, seq):
    width = v.shape[-1]
    nb = seq // MOBA_BLOCK
    v3 = v.reshape(batch, seq, width)
    keys3 = keys.reshape(batch, seq, 2 * width)
    spec = pl.BlockSpec((None, seq, LANES), lambda b, p: (b, 0, p))
    head_specs = [pl.BlockSpec((None, seq, LANES), lambda b, p, h=h: (b, 0, 2 * p + h)) for h in range(2)]
    qa_spec = pl.BlockSpec((nb, None, 2, LANES, MOBA_BLOCK), lambda b, p: (b, p, 0, 0, 0))
    out = pl.pallas_call(
        functools.partial(_attn_kernel, nb=nb),
        grid=(batch, width // LANES),
        in_specs=[qa_spec, *head_specs, spec],
        out_specs=spec,
        out_shape=jax.ShapeDtypeStruct((batch, seq, width), BF16),
        scratch_shapes=[pltpu.VMEM((nb, 2, ACC_ROWS, MOBA_BLOCK), BF16),
                        *[pltpu.VMEM((nb + 1, 2, MOBA_BLOCK, MOBA_BLOCK), F32)] * 2,
                        *[pltpu.VMEM((2, 2, 8, MOBA_BLOCK), F32)] * 2,
                        *[pltpu.VMEM((2, 2, ACC_ROWS, MOBA_BLOCK), F32)] * 2],
        compiler_params=pltpu.CompilerParams(dimension_semantics=("arbitrary", "arbitrary"),
                                             vmem_limit_bytes=VMEM_LIMIT),
        name="moba_attention",
    )(qa, keys3, keys3, v3)
    return out.reshape(batch * seq, width)


def _layer_norm(y, g, b):
    mu = jnp.mean(y, axis=-1, keepdims=True)
    d = y - mu
    var = jnp.mean(d * d, axis=-1, keepdims=True)
    return d * lax.rsqrt(var + LN_EPS) * g + b


def _rms_norm(t, g):
    ms = jnp.mean(t * t, axis=-1, keepdims=True)
    return t * lax.rsqrt(ms + RMS_EPS) * g


def _mix_pieces(attn_ref, u_ref, z_ref, x_ref, ws_ref, bs_ref, ag_ref, gg_ref, w_ref, lg_ref, lb_ref,
                sg_ref, store, *, alpha, width, n_sub):
    c = GMLP_CHUNK
    sm = x_ref.shape[0] // n_sub
    n_pairs = ws_ref.shape[0] // 2
    ii = lax.broadcasted_iota(jnp.int32, (c, c), 0)
    jj = lax.broadcasted_iota(jnp.int32, (c, c), 1)
    lane = lax.broadcasted_iota(jnp.int32, (1, LANES), 1)
    lo = lane < HEAD_DIM
    w_cat = [jnp.concatenate([jnp.where(jj <= ii, ws_ref[2 * p + g], 0.0) for g in range(2)],
                             axis=1).astype(BF16) for p in range(n_pairs)]

    def z_stack(rows, sl):
        z = z_ref[rows, sl]
        zero = jnp.zeros_like(z)
        return jnp.concatenate([jnp.where(lo, z, zero), jnp.where(lo, zero, z)], axis=0)

    def gate_and_norm(s):
        for c2 in range(sm // (2 * c)):
            r0 = s * sm + c2 * 2 * c
            rows = [slice(r0, r0 + c), slice(r0 + c, r0 + 2 * c)]
            for p in range(n_pairs):
                sl = slice(p * LANES, (p + 1) * LANES)
                rhs = jnp.concatenate([z_stack(rows[0], sl), z_stack(rows[1], sl)], axis=1)
                mixed = jnp.dot(w_cat[p], rhs, preferred_element_type=F32)
                for k in range(2):
                    sg_ref[rows[k], sl] = (u_ref[rows[k], sl].astype(F32)
                                           * (mixed[:, k * LANES:(k + 1) * LANES] + bs_ref[:, sl]))
        rows = slice(s * sm, (s + 1) * sm)
        an = _rms_norm(attn_ref[rows, :].astype(F32), ag_ref[...]).astype(BF16)
        sn = _rms_norm(sg_ref[rows, :], gg_ref[...]).astype(BF16)
        return an, sn

    def project(normed, cols):
        an, sn = normed
        return (jnp.dot(an, w_ref[:width, cols], preferred_element_type=F32)
                + jnp.dot(sn, w_ref[width:, cols], preferred_element_type=F32))

    def residual_norm(s, mix):
        rows = slice(s * sm, (s + 1) * sm)
        store(rows, _layer_norm(alpha * x_ref[rows, :] + mix, lg_ref[...], lb_ref[...]))

    d_model = w_ref.shape[1]
    col_halves = [slice(0, d_model // 2), slice(d_model // 2, d_model)]
    normed, lefts, mixes = {}, {}, {}

    def do_norm(s):
        normed[s] = gate_and_norm(s)

    def do_left(s):
        lefts[s] = project(normed[s], col_halves[0])

    def do_right(s):
        mixes[s] = jnp.concatenate([lefts.pop(s), project(normed.pop(s), col_halves[1])], axis=1)

    def do_out(s):
        residual_norm(s, mixes.pop(s))

    pieces = []
    for s in range(n_sub + 2):
        if s < n_sub:
            pieces.append(functools.partial(do_norm, s))
        if 1 <= s <= n_sub:
            pieces.append(functools.partial(do_left, s - 1))
        if 2 <= s:
            pieces.append(functools.partial(do_out, s - 2))
        if 1 <= s <= n_sub:
            pieces.append(functools.partial(do_right, s - 1))
    return pieces


def _post_kernel(attn_ref, u_ref, z_ref, x_ref, attn0_ref, u0_ref, z0_ref, x0_ref, ws_ref, bs_ref,
                 ag_ref, gg_ref, wo_ref, l1g_ref, l1b_ref, wg_ref, wu_ref, wd_ref, l2g_ref, l2b_ref,
                 o_ref, x1_ref, sg_ref, acc_ref, *, alpha, width, hc, n_sub):
    t = pl.program_id(0)
    slot = t % 2
    mix_refs = (ws_ref, bs_ref, ag_ref, gg_ref, wo_ref, l1g_ref, l1b_ref, sg_ref)
    mix_kw = dict(alpha=alpha, width=width, n_sub=n_sub)

    @pl.when(t == 0)
    def _():
        def store_first(rows, value):
            x1_ref[0, rows, :] = value
        for piece in _mix_pieces(attn0_ref, u0_ref, z0_ref, x0_ref, *mix_refs, store_first, **mix_kw):
            piece()

    def store_next(rows, value):
        x1_ref[1 - slot, rows, :] = value

    pieces = _mix_pieces(attn_ref, u_ref, z_ref, x_ref, *mix_refs, store_next, **mix_kw)
    hidden = wg_ref.shape[1]
    n_chunks = hidden // hc
    assert hidden % hc == 0 and len(pieces) <= n_chunks
    xb = x1_ref[slot].astype(BF16)
    for c in range(n_chunks):
        sl = slice(c * hc, (c + 1) * hc)
        g = jnp.dot(xb, wg_ref[:, sl], preferred_element_type=F32)
        u = jnp.dot(xb, wu_ref[:, sl], preferred_element_type=F32)
        h = (g / (1.0 + jnp.exp(-g)) * u).astype(BF16)
        part = jnp.dot(h, wd_ref[sl, :], preferred_element_type=F32)
        if c == 0:
            acc_ref[...] = part
        else:
            acc_ref[...] += part
        if c < len(pieces):
            pieces[c]()
    o_ref[...] = _layer_norm(alpha * x1_ref[slot] + acc_ref[...], l2g_ref[...], l2b_ref[...])


def _post_attention(attn, u, z, x2, w_spatial, bias_full, ag, gg, w_out, l1g, l1b,
                    w_gate, w_up, w_down, l2g, l2b, *, alpha, tm, hc, n_sub):
    tokens, width = attn.shape
    d_model = x2.shape[1]
    n_tiles = tokens // tm
    assert (tm // n_sub) % (2 * GMLP_CHUNK) == 0
    nxt = lambda t: (jnp.minimum(t + 1, n_tiles - 1), 0)
    once = dict(pipeline_mode=pl.Buffered(1))
    half_n = pl.BlockSpec((tm, width), nxt)
    full_n = pl.BlockSpec((tm, d_model), nxt)
    half_0 = pl.BlockSpec((tm, width), lambda t: (0, 0), **once)
    full_0 = pl.BlockSpec((tm, d_model), lambda t: (0, 0), **once)
    const = lambda a: pl.BlockSpec(a.shape, lambda t: (0,) * a.ndim, **once)
    return pl.pallas_call(
        functools.partial(_post_kernel, alpha=alpha, width=width, hc=hc, n_sub=n_sub),
        grid=(n_tiles,),
        in_specs=[half_n, half_n, half_n, full_n, half_0, half_0, half_0, full_0,
                  const(w_spatial), const(bias_full), const(ag), const(gg), const(w_out),
                  const(l1g), const(l1b), const(w_gate), const(w_up), const(w_down),
                  const(l2g), const(l2b)],
        out_specs=pl.BlockSpec((tm, d_model), lambda t: (t, 0)),
        out_shape=jax.ShapeDtypeStruct((tokens, d_model), F32),
        scratch_shapes=[pltpu.VMEM((2, tm, d_model), F32),
                        pltpu.VMEM((tm, width), F32),
                        pltpu.VMEM((tm, d_model), F32)],
        compiler_params=pltpu.CompilerParams(dimension_semantics=("arbitrary",),
                                             vmem_limit_bytes=POST_VMEM_LIMIT),
        name="mixer_out_ffn",
    )(attn, u, z, x2, attn, u, z, x2, w_spatial, bias_full, ag, gg, w_out, l1g, l1b,
      w_gate, w_up, w_down, l2g, l2b)


def _rope_tables(seq):
    assert seq % ROPE_SPLIT == 0
    inv_freq = ROPE_THETA ** (-jnp.arange(0, HEAD_DIM, 2, dtype=F32) / HEAD_DIM)
    tile = lambda t: jnp.concatenate([t, t, t, t], axis=-1)

    def cos_sin(pos):
        ang = pos.astype(F32)[:, None] * inv_freq[None, :]
        return tile(jnp.cos(ang)), tile(jnp.sin(ang))

    cos_hi, sin_hi = cos_sin(jnp.arange(0, seq, ROPE_SPLIT, dtype=jnp.int32))
    cos_lo, sin_lo = cos_sin(jnp.arange(ROPE_SPLIT, dtype=jnp.int32))
    sign = jnp.where((jnp.arange(LANES) % HEAD_DIM) < HEAD_DIM // 2, -1.0, 1.0)[None, :]
    return jnp.stack([cos_hi, sin_hi]), jnp.stack([cos_lo, sin_lo, sign * cos_lo, sign * sin_lo])


def kernel(x, w_in, attn_out_g, gmlp_out_g, gmlp_ln_g, gmlp_ln_b, w_spatial, b_spatial, w_out,
           ln1_g, ln1_b, w_gate, w_up, w_down, ln2_g, ln2_b):
    batch, seq, d_model = x.shape
    depth = w_in.shape[0]
    width = attn_out_g.shape[-1]
    alpha = (2 * depth) ** 0.25
    assert w_in.shape[-1] == 5 * width and seq % MOBA_BLOCK == 0 and width % LANES == 0
    assert min(MOBA_TOPK, seq // MOBA_BLOCK - 1) == MOBA_TOPK
    rope_hi, rope_lo = _rope_tables(seq)
    x2 = x.reshape(batch * seq, d_model)
    for l in range(depth):
        (qa, keys, v, u, z), (w_out_b, w_gate_b, w_up_b, w_down_b) = _in_proj(
            x2, w_in[l], rope_hi, rope_lo,
            gmlp_ln_g[l].reshape(1, width), gmlp_ln_b[l].reshape(1, width),
            (w_out[l], w_gate[l], w_up[l], w_down[l]), seq=seq, width=width, tm=IN_PROJ_TILE, n_sub=IN_PROJ_SUBTILES)
        attn = _attention(qa, keys, v, batch=batch, seq=seq)
        bias_full = jnp.repeat(b_spatial[l].T, HEAD_DIM, axis=1)
        x2 = _post_attention(attn, u, z, x2, w_spatial[l], bias_full,
                             attn_out_g[l].reshape(1, width), gmlp_out_g[l].reshape(1, width),
                             w_out_b, ln1_g[l].reshape(1, d_model), ln1_b[l].reshape(1, d_model),
                             w_gate_b, w_up_b, w_down_b,
                             ln2_g[l].reshape(1, d_model), ln2_b[l].reshape(1, d_model),
                             alpha=alpha, tm=POST_TILE, hc=FFN_CHUNK, n_sub=POST_SUBTILES)
    return x2.reshape(batch, seq, d_model)
```

```python
import functools

import jax
import jax.numpy as jnp
from jax import lax
from jax.experimental import pallas as pl
from jax.experimental.pallas import tpu as pltpu

HEAD_DIM = 64
MOBA_BLOCK = 256
MOBA_TOPK = 3
GMLP_CHUNK = 128
ROPE_THETA = 10000.0
ROPE_SPLIT = 64
LN_EPS = 1e-5
RMS_EPS = 1e-6
NEG_INF = -1e30
LANES = 128
ACC_ROWS = HEAD_DIM + 16
F32 = jnp.float32
BF16 = jnp.bfloat16

IN_PROJ_TILE, IN_PROJ_SUBTILES = 1024, 8
POST_TILE, POST_SUBTILES = 512, 2
FFN_CHUNK = 256
VMEM_LIMIT = 48 * 1024 * 1024
POST_VMEM_LIMIT = 56 * 1024 * 1024


def _erf_gelu(t):
    return 0.5 * t * (1.0 + lax.erf(t * 0.7071067811865476))


def _in_proj_kernel(x_ref, w32_ref, rope_hi_ref, rope_lo_ref, lng_ref, lnb_ref, *rest,
                    width, n_sub, n_cast, n_pos_tiles):
    cast_in, rest = rest[:n_cast], rest[n_cast:]
    (q_ref, ka_ref, v_ref, u_ref, z_ref, kmean_ref), rest = rest[:6], rest[6:]
    cast_out, (w_ref,) = rest[:n_cast], rest[n_cast:]
    first_block = lax.rem(pl.program_id(0), n_pos_tiles) * (x_ref.shape[0] // MOBA_BLOCK)

    @pl.when(pl.program_id(0) == 0)
    def _():
        w_ref[...] = w32_ref[...].astype(BF16)

    lane = lax.broadcasted_iota(jnp.int32, (1, LANES), 1)
    lo = lane < HEAD_DIM
    sm = x_ref.shape[0] // n_sub
    n_seg = w_ref.shape[1] // width
    row_slices = [slice(s * sm, (s + 1) * sm) for s in range(n_sub)]

    first_half = (lane % HEAD_DIM) < HEAD_DIM // 2
    tables = {}

    def rope_tables(rows):
        if rows.start not in tables:
            cos_b, sin_b, cos_b_signed, sin_b_signed = (rope_lo_ref[i] for i in range(4))
            cos_rows, sin_rows = [], []
            for a in range(rows.start // ROPE_SPLIT, rows.stop // ROPE_SPLIT):
                cos_a, sin_a = rope_hi_ref[0, a:a + 1, :], rope_hi_ref[1, a:a + 1, :]
                cos_rows.append(cos_a * cos_b - sin_a * sin_b)
                sin_rows.append(sin_a * cos_b_signed + cos_a * sin_b_signed)
            tables[rows.start] = (jnp.concatenate(cos_rows, axis=0), jnp.concatenate(sin_rows, axis=0))
        return tables[rows.start]

    def rope(t, rows):
        cos, sin_signed = rope_tables(rows)
        rotated = jnp.where(first_half, pltpu.roll(t, 96, 1), pltpu.roll(t, 32, 1))
        return t * cos + rotated * sin_signed

    def group_norm(zs, sl):
        s_lo = jnp.sum(jnp.where(lo, zs, 0.0), axis=1, keepdims=True)
        s_hi = jnp.sum(jnp.where(lo, 0.0, zs), axis=1, keepdims=True)
        d = zs - jnp.where(lo, s_lo, s_hi) * (1.0 / HEAD_DIM)
        dd = d * d
        v_lo = jnp.sum(jnp.where(lo, dd, 0.0), axis=1, keepdims=True)
        v_hi = jnp.sum(jnp.where(lo, 0.0, dd), axis=1, keepdims=True)
        var = jnp.where(lo, v_lo, v_hi) * (1.0 / HEAD_DIM)
        return d * lax.rsqrt(var + LN_EPS) * lng_ref[:, sl] + lnb_ref[:, sl]

    k_sums = {}

    def emit_keys(k_rot, rows, c):
        local = rows.start // MOBA_BLOCK
        kb = k_rot.astype(BF16)
        for hd in range(2):
            in_head = (lane >= HEAD_DIM * hd) & (lane < HEAD_DIM * (hd + 1))
            onehot = jnp.where(lane == HEAD_DIM * (1 - hd) + first_block + local, 1.0, 0.0)
            slab = slice((2 * c + hd) * LANES, (2 * c + hd + 1) * LANES)
            ka_ref[rows, slab] = jnp.where(in_head, kb, onehot.astype(BF16))
        part = jnp.sum(k_rot, axis=0, keepdims=True) * (1.0 / MOBA_BLOCK)
        key = (local, c)
        k_sums[key] = part if rows.start % MOBA_BLOCK == 0 else k_sums[key] + part
        if rows.stop % MOBA_BLOCK == 0:
            kmean_ref[local, :, c * LANES:(c + 1) * LANES] = jnp.broadcast_to(k_sums.pop(key), (8, LANES))

    def epilogue(seg, h, rows):
        if seg == 2:
            v_ref[rows, :] = h.astype(BF16)
        elif seg == 3:
            u_ref[rows, :] = _erf_gelu(h).astype(BF16)
        else:
            if seg == 4:
                h = _erf_gelu(h)
            for c in range(width // LANES):
                sl = slice(c * LANES, (c + 1) * LANES)
                if seg == 0:
                    q_ref[rows, sl] = rope(h[:, sl] * (HEAD_DIM ** -0.5), rows).astype(BF16)
                elif seg == 1:
                    emit_keys(rope(h[:, sl], rows), rows, c)
                else:
                    z_ref[rows, sl] = group_norm(h[:, sl], sl).astype(BF16)

    order = (4, 3, 0, 1, 2)
    xs = [None] * n_sub
    hs = [[None] * n_seg for _ in range(n_sub)]
    last = n_sub - 1
    for s in range(n_sub):
        xs[s] = x_ref[row_slices[s], :].astype(BF16)
        done = None
        for seg in order:
            hs[s][seg] = jnp.dot(xs[s], w_ref[:, seg * width:(seg + 1) * width],
                                 preferred_element_type=F32)
            if s >= 1:
                epilogue(seg, hs[s - 1][seg], row_slices[s - 1])
            if s == last and done is not None:
                epilogue(done, hs[s][done], row_slices[s])
            done = seg
        for src, dst in list(zip(cast_in, cast_out))[s::n_sub]:
            dst[...] = src[...].astype(BF16)
    epilogue(order[-1], hs[last][order[-1]], row_slices[last])


def _in_proj(x2, w_in, rope_hi, rope_lo, lng, lnb, later_weights, *, seq, width, tm, n_sub):
    tokens, d_model = x2.shape
    n_steps = tokens // tm
    n_pos_tiles = seq // tm
    assert (tm // n_sub) % ROPE_SPLIT == 0 and (tm // ROPE_SPLIT) % 8 == 0
    assert tm % MOBA_BLOCK == 0 and MOBA_BLOCK % (tm // n_sub) == 0
    out = jax.ShapeDtypeStruct((tokens, width), BF16)
    tile = pl.BlockSpec((tm, width), lambda t: (t, 0))
    keys_out = jax.ShapeDtypeStruct((tokens, 2 * width), BF16)
    keys_tile = pl.BlockSpec((tm, 2 * width), lambda t: (t, 0))
    mean_out = jax.ShapeDtypeStruct((tokens // MOBA_BLOCK, 8, width), F32)
    mean_tile = pl.BlockSpec((tm // MOBA_BLOCK, 8, width), lambda t: (t, 0, 0))
    hi_spec = pl.BlockSpec((2, tm // ROPE_SPLIT, LANES), lambda t: (0, t % n_pos_tiles, 0))
    lo_spec = pl.BlockSpec(rope_lo.shape, lambda t: (0, 0, 0))
    row = pl.BlockSpec((1, width), lambda t: (0, 0))
    slabs = []
    for w in later_weights:
        assert w.shape[0] % (16 * n_steps) == 0
        slabs.append(pl.BlockSpec((w.shape[0] // n_steps, w.shape[1]), lambda t: (t, 0)))
    res = pl.pallas_call(
        functools.partial(_in_proj_kernel, width=width, n_sub=n_sub, n_cast=len(later_weights),
                          n_pos_tiles=n_pos_tiles),
        grid=(n_steps,),
        in_specs=[pl.BlockSpec((tm, d_model), lambda t: (t, 0)),
                  pl.BlockSpec(w_in.shape, lambda t: (0, 0), pipeline_mode=pl.Buffered(1)),
                  hi_spec, lo_spec, row, row, *slabs],
        out_specs=[tile, keys_tile, tile, tile, tile, mean_tile] + slabs,
        out_shape=[out, keys_out, out, out, out, mean_out]
        + [jax.ShapeDtypeStruct(w.shape, BF16) for w in later_weights],
        scratch_shapes=[pltpu.VMEM(w_in.shape, BF16)],
        compiler_params=pltpu.CompilerParams(dimension_semantics=("arbitrary",),
                                             vmem_limit_bytes=VMEM_LIMIT),
        name="in_proj",
    )(x2, w_in, rope_hi, rope_lo, lng, lnb, *later_weights)
    return res[:6], res[6:]


def _aligned(start, multiple):
    return start if isinstance(start, int) else pl.multiple_of(start, multiple)


def _attn_kernel(q_ref, ka0_ref, ka1_ref, v_ref, kmean_ref, o_ref, vt_ref, km_ref, qa_ref,
                 s0_ref, s1_ref, m0_ref, m1_ref, acc0_ref, acc1_ref, *, nb):
    blk = MOBA_BLOCK
    sub = blk // 8
    s_refs, m_refs = (s0_ref, s1_ref), (m0_ref, m1_ref)
    acc_refs = (acc0_ref, acc1_ref)
    ka_refs = (ka0_ref, ka1_ref)
    assert nb % 4 == 0
    lane = lax.broadcasted_iota(jnp.int32, (1, LANES), 1)
    head_lanes = [(lane >= HEAD_DIM * h) & (lane < HEAD_DIM * (h + 1)) for h in range(2)]

    km_pair = jnp.max(kmean_ref[...], axis=1)
    for h in range(2):
        km = jnp.where(head_lanes[h], km_pair, 0.0)
        hi = km.astype(BF16)
        km_ref[(2 * h) * nb:(2 * h + 1) * nb, :] = hi
        km_ref[(2 * h + 1) * nb:(2 * h + 2) * nb, :] = (km - hi.astype(F32)).astype(BF16)

    def build_values(j):
        ones_row = jnp.where(lax.broadcasted_iota(jnp.int32, (ACC_ROWS - HEAD_DIM, blk), 0) == 0, 1.0, 0.0)
        v_t = v_ref[j * blk:(j + 1) * blk, :].astype(F32).T
        for h in range(2):
            vt_ref[j, h] = jnp.concatenate(
                [v_t[HEAD_DIM * h:HEAD_DIM * (h + 1)], ones_row], axis=0).astype(BF16)

    row = lax.broadcasted_iota(jnp.int32, (nb, blk), 0)
    rowf = row.astype(F32)
    kk = lax.broadcasted_iota(jnp.int32, (blk, blk), 0)
    qq = lax.broadcasted_iota(jnp.int32, (blk, blk), 1)

    def select_blocks(r):
        q_t = q_ref[r * blk:(r + 1) * blk, :].astype(F32).T.astype(BF16)
        few_past = r <= MOBA_TOPK
        if not few_past:
            g_all = jnp.dot(km_ref[...], q_t, preferred_element_type=F32)
        for h in range(2):
            past = row < r
            bias = jnp.where((row <= r) if few_past else (row == r), 0.0, NEG_INF)
            if not few_past:
                g = g_all[(2 * h) * nb:(2 * h + 1) * nb] + g_all[(2 * h + 1) * nb:(2 * h + 2) * nb]
                g = jnp.where(past, g, NEG_INF)
            for _ in range(0 if few_past else MOBA_TOPK):
                top = jnp.max(g, axis=0, keepdims=True)
                first = jnp.min(jnp.where(g == top, rowf, float(nb)), axis=0, keepdims=True)
                pick = rowf == first
                bias = jnp.where(pick & past, 0.0, bias)
                g = jnp.where(pick, -jnp.inf, g)
            q_h = q_t[HEAD_DIM * h:HEAD_DIM * (h + 1)]
            bias = bias.astype(BF16)
            pad = jnp.zeros((HEAD_DIM - nb, blk), BF16)
            pieces = [q_h, bias, pad] if h == 0 else [bias, pad, q_h]
            qa_ref[r, h] = jnp.concatenate(pieces, axis=0)

    def score_tile(slot, t, r, rs, j, causal):
        m_ref, s_ref = m_refs[slot], s_refs[slot]
        k0 = _aligned(j * blk, blk)
        for h in range(2):
            s = jnp.dot(ka_refs[h][pl.ds(k0, blk), :], qa_ref[r, h],
                        preferred_element_type=F32)
            if causal is not None:
                s = s + causal
            s_ref[t, h] = s
            m_ref[rs, h] = jnp.maximum(m_ref[rs, h], jnp.max(s.reshape(sub, 8, blk), axis=0))

    def value_tile(slot, t, rs, j):
        m_ref, s_ref, acc_ref = m_refs[slot], s_refs[slot], acc_refs[slot]
        for h in range(2):
            p = jnp.exp(s_ref[t, h].reshape(sub, 8, blk) - m_ref[rs, h]).reshape(blk, blk)
            acc_ref[rs, h] += jnp.dot(vt_ref[j, h], p.astype(BF16), preferred_element_type=F32)

    def past_tile(u, pi):
        if isinstance(pi, int):
            return (0, u) if u < pi else (1, u - pi)
        first = u < pi
        return jnp.where(first, 0, 1), jnp.where(first, u, u - pi)

    def tile_of(t, pi):
        rows = (pi, nb - 1 - pi)
        return (t, rows[t]) if t < 2 else past_tile(t - 2, pi)

    def write_rows(pair, slot, rs):
        outs = []
        for h in range(2):
            a = acc_refs[slot][rs, h]
            outs.append(a[:HEAD_DIM] / a[HEAD_DIM:HEAD_DIM + 1])
        q0 = _aligned((pair, nb - 1 - pair)[rs] * blk, blk)
        o_ref[pl.ds(q0, blk), :] = jnp.concatenate(outs, axis=0).T.astype(BF16)

    def step(k, slot, extra=()):
        live = lambda p: not isinstance(p, int) or 0 <= p < nb // 2
        score_pair, value_pair, write_pair = k, k - 1, k - 2
        do_score, do_value, do_write = (live(p) for p in (score_pair, value_pair, write_pair))
        if do_score:
            causal = jnp.where(kk <= qq, 0.0, NEG_INF)
            for rs in range(2):
                for h in range(2):
                    m_refs[slot][rs, h] = jnp.full((8, blk), NEG_INF, F32)
        if do_value:
            for rs in range(2):
                for h in range(2):
                    acc_refs[1 - slot][rs, h] = jnp.zeros((ACC_ROWS, blk), F32)
        write_at = {1: 0, nb // 2: 1}
        for t in range(nb + 1):
            if do_score:
                rs, j = tile_of(t, score_pair)
                r = score_pair + rs * (nb - 1 - 2 * score_pair)
                score_tile(slot, t, r, rs, j, causal if t < 2 else None)
            if do_value:
                rs, j = tile_of(t, value_pair)
                value_tile(1 - slot, t, rs, j)
            if do_write and t in write_at:
                write_rows(write_pair, slot, write_at[t])
            for thunk in extra[2 * t:2 * t + 2]:
                thunk()
        if do_score:
            m_ref = m_refs[slot]
            for rs in range(2):
                for h in range(2):
                    m_ref[rs, h] = jnp.broadcast_to(jnp.max(m_ref[rs, h], axis=0, keepdims=True),
                                                    (8, blk))

    n_pairs = nb // 2
    select_blocks(0)
    select_blocks(nb - 1)
    prep = []
    for j in range(nb):
        prep.append(functools.partial(build_values, j))
        if 1 <= j < nb - 1:
            prep.append(functools.partial(select_blocks, j))
    assert len(prep) <= 2 * (nb + 1)
    step(0, 0, prep)
    step(1, 1)

    def two_steps(i2, _):
        step(2 * i2 + 2, 0)
        step(2 * i2 + 3, 1)
        return 0

    n_rolled = (n_pairs - 2) // 2
    lax.fori_loop(0, n_rolled, two_steps, 0)
    for k in range(2 + 2 * n_rolled, n_pairs + 2):
        step(k, k % 2)


def _attention(q, keys, v, k_mean, *, batch, seq):
    width = q.shape[-1]
    nb = seq // MOBA_BLOCK
    q3, v3 = (t.reshape(batch, seq, width) for t in (q, v))
    keys3 = keys.reshape(batch, seq, 2 * width)
    spec = pl.BlockSpec((None, seq, LANES), lambda b, p: (b, 0, p))
    head_specs = [pl.BlockSpec((None, seq, LANES), lambda b, p, h=h: (b, 0, 2 * p + h)) for h in range(2)]
    mean_spec = pl.BlockSpec((nb, 8, LANES), lambda b, p: (b, 0, p))
    out = pl.pallas_call(
        functools.partial(_attn_kernel, nb=nb),
        grid=(batch, width // LANES),
        in_specs=[spec, *head_specs, spec, mean_spec],
        out_specs=spec,
        out_shape=jax.ShapeDtypeStruct((batch, seq, width), BF16),
        scratch_shapes=[pltpu.VMEM((nb, 2, ACC_ROWS, MOBA_BLOCK), BF16),
                        pltpu.VMEM((4 * nb, LANES), BF16),
                        pltpu.VMEM((nb, 2, LANES, MOBA_BLOCK), BF16),
                        *[pltpu.VMEM((nb + 1, 2, MOBA_BLOCK, MOBA_BLOCK), F32)] * 2,
                        *[pltpu.VMEM((2, 2, 8, MOBA_BLOCK), F32)] * 2,
                        *[pltpu.VMEM((2, 2, ACC_ROWS, MOBA_BLOCK), F32)] * 2],
        compiler_params=pltpu.CompilerParams(dimension_semantics=("arbitrary", "arbitrary"),
                                             vmem_limit_bytes=VMEM_LIMIT),
        name="moba_attention",
    )(q3, keys3, keys3, v3, k_mean)
    return out.reshape(batch * seq, width)


def _layer_norm(y, g, b):
    mu = jnp.mean(y, axis=-1, keepdims=True)
    d = y - mu
    var = jnp.mean(d * d, axis=-1, keepdims=True)
    return d * lax.rsqrt(var + LN_EPS) * g + b


def _rms_norm(t, g):
    ms = jnp.mean(t * t, axis=-1, keepdims=True)
    return t * lax.rsqrt(ms + RMS_EPS) * g


def _mix_pieces(attn_ref, u_ref, z_ref, x_ref, ws_ref, bs_ref, ag_ref, gg_ref, w_ref, lg_ref, lb_ref,
                sg_ref, store, *, alpha, width, n_sub):
    c = GMLP_CHUNK
    sm = x_ref.shape[0] // n_sub
    n_pairs = ws_ref.shape[0] // 2
    ii = lax.broadcasted_iota(jnp.int32, (c, c), 0)
    jj = lax.broadcasted_iota(jnp.int32, (c, c), 1)
    lane = lax.broadcasted_iota(jnp.int32, (1, LANES), 1)
    lo = lane < HEAD_DIM
    w_cat = [jnp.concatenate([jnp.where(jj <= ii, ws_ref[2 * p + g], 0.0) for g in range(2)],
                             axis=1).astype(BF16) for p in range(n_pairs)]

    def z_stack(rows, sl):
        z = z_ref[rows, sl]
        zero = jnp.zeros_like(z)
        return jnp.concatenate([jnp.where(lo, z, zero), jnp.where(lo, zero, z)], axis=0)

    def gate_and_norm(s):
        for c2 in range(sm // (2 * c)):
            r0 = s * sm + c2 * 2 * c
            rows = [slice(r0, r0 + c), slice(r0 + c, r0 + 2 * c)]
            for p in range(n_pairs):
                sl = slice(p * LANES, (p + 1) * LANES)
                rhs = jnp.concatenate([z_stack(rows[0], sl), z_stack(rows[1], sl)], axis=1)
                mixed = jnp.dot(w_cat[p], rhs, preferred_element_type=F32)
                for k in range(2):
                    sg_ref[rows[k], sl] = (u_ref[rows[k], sl].astype(F32)
                                           * (mixed[:, k * LANES:(k + 1) * LANES] + bs_ref[:, sl]))
        rows = slice(s * sm, (s + 1) * sm)
        an = _rms_norm(attn_ref[rows, :].astype(F32), ag_ref[...]).astype(BF16)
        sn = _rms_norm(sg_ref[rows, :], gg_ref[...]).astype(BF16)
        return an, sn

    def project(normed, cols):
        an, sn = normed
        return (jnp.dot(an, w_ref[:width, cols], preferred_element_type=F32)
                + jnp.dot(sn, w_ref[width:, cols], preferred_element_type=F32))

    def residual_norm(s, mix):
        rows = slice(s * sm, (s + 1) * sm)
        store(rows, _layer_norm(alpha * x_ref[rows, :] + mix, lg_ref[...], lb_ref[...]))

    d_model = w_ref.shape[1]
    col_halves = [slice(0, d_model // 2), slice(d_model // 2, d_model)]
    normed, lefts, mixes = {}, {}, {}

    def do_norm(s):
        normed[s] = gate_and_norm(s)

    def do_left(s):
        lefts[s] = project(normed[s], col_halves[0])

    def do_right(s):
        mixes[s] = jnp.concatenate([lefts.pop(s), project(normed.pop(s), col_halves[1])], axis=1)

    def do_out(s):
        residual_norm(s, mixes.pop(s))

    pieces = []
    for s in range(n_sub + 2):
        if s < n_sub:
            pieces.append(functools.partial(do_norm, s))
        if 1 <= s <= n_sub:
            pieces.append(functools.partial(do_left, s - 1))
        if 2 <= s:
            pieces.append(functools.partial(do_out, s - 2))
        if 1 <= s <= n_sub:
            pieces.append(functools.partial(do_right, s - 1))
    return pieces


def _post_kernel(attn_ref, u_ref, z_ref, x_ref, attn0_ref, u0_ref, z0_ref, x0_ref, ws_ref, bs_ref,
                 ag_ref, gg_ref, wo_ref, l1g_ref, l1b_ref, wg_ref, wu_ref, wd_ref, l2g_ref, l2b_ref,
                 o_ref, x1_ref, sg_ref, acc_ref, *, alpha, width, hc, n_sub):
    t = pl.program_id(0)
    slot = t % 2
    mix_refs = (ws_ref, bs_ref, ag_ref, gg_ref, wo_ref, l1g_ref, l1b_ref, sg_ref)
    mix_kw = dict(alpha=alpha, width=width, n_sub=n_sub)

    @pl.when(t == 0)
    def _():
        def store_first(rows, value):
            x1_ref[0, rows, :] = value
        for piece in _mix_pieces(attn0_ref, u0_ref, z0_ref, x0_ref, *mix_refs, store_first, **mix_kw):
            piece()

    def store_next(rows, value):
        x1_ref[1 - slot, rows, :] = value

    pieces = _mix_pieces(attn_ref, u_ref, z_ref, x_ref, *mix_refs, store_next, **mix_kw)
    hidden = wg_ref.shape[1]
    n_chunks = hidden // hc
    assert hidden % hc == 0 and len(pieces) <= n_chunks
    xb = x1_ref[slot].astype(BF16)
    for c in range(n_chunks):
        sl = slice(c * hc, (c + 1) * hc)
        g = jnp.dot(xb, wg_ref[:, sl], preferred_element_type=F32)
        u = jnp.dot(xb, wu_ref[:, sl], preferred_element_type=F32)
        h = (g / (1.0 + jnp.exp(-g)) * u).astype(BF16)
        part = jnp.dot(h, wd_ref[sl, :], preferred_element_type=F32)
        if c == 0:
            acc_ref[...] = part
        else:
            acc_ref[...] += part
        if c < len(pieces):
            pieces[c]()
    o_ref[...] = _layer_norm(alpha * x1_ref[slot] + acc_ref[...], l2g_ref[...], l2b_ref[...])


def _post_attention(attn, u, z, x2, w_spatial, bias_full, ag, gg, w_out, l1g, l1b,
                    w_gate, w_up, w_down, l2g, l2b, *, alpha, tm, hc, n_sub):
    tokens, width = attn.shape
    d_model = x2.shape[1]
    n_tiles = tokens // tm
    assert (tm // n_sub) % (2 * GMLP_CHUNK) == 0
    nxt = lambda t: (jnp.minimum(t + 1, n_tiles - 1), 0)
    once = dict(pipeline_mode=pl.Buffered(1))
    half_n = pl.BlockSpec((tm, width), nxt)
    full_n = pl.BlockSpec((tm, d_model), nxt)
    half_0 = pl.BlockSpec((tm, width), lambda t: (0, 0), **once)
    full_0 = pl.BlockSpec((tm, d_model), lambda t: (0, 0), **once)
    const = lambda a: pl.BlockSpec(a.shape, lambda t: (0,) * a.ndim, **once)
    return pl.pallas_call(
        functools.partial(_post_kernel, alpha=alpha, width=width, hc=hc, n_sub=n_sub),
        grid=(n_tiles,),
        in_specs=[half_n, half_n, half_n, full_n, half_0, half_0, half_0, full_0,
                  const(w_spatial), const(bias_full), const(ag), const(gg), const(w_out),
                  const(l1g), const(l1b), const(w_gate), const(w_up), const(w_down),
                  const(l2g), const(l2b)],
        out_specs=pl.BlockSpec((tm, d_model), lambda t: (t, 0)),
        out_shape=jax.ShapeDtypeStruct((tokens, d_model), F32),
        scratch_shapes=[pltpu.VMEM((2, tm, d_model), F32),
                        pltpu.VMEM((tm, width), F32),
                        pltpu.VMEM((tm, d_model), F32)],
        compiler_params=pltpu.CompilerParams(dimension_semantics=("arbitrary",),
                                             vmem_limit_bytes=POST_VMEM_LIMIT),
        name="mixer_out_ffn",
    )(attn, u, z, x2, attn, u, z, x2, w_spatial, bias_full, ag, gg, w_out, l1g, l1b,
      w_gate, w_up, w_down, l2g, l2b)


def _rope_tables(seq):
    assert seq % ROPE_SPLIT == 0
    inv_freq = ROPE_THETA ** (-jnp.arange(0, HEAD_DIM, 2, dtype=F32) / HEAD_DIM)
    tile = lambda t: jnp.concatenate([t, t, t, t], axis=-1)

    def cos_sin(pos):
        ang = pos.astype(F32)[:, None] * inv_freq[None, :]
        return tile(jnp.cos(ang)), tile(jnp.sin(ang))

    cos_hi, sin_hi = cos_sin(jnp.arange(0, seq, ROPE_SPLIT, dtype=jnp.int32))
    cos_lo, sin_lo = cos_sin(jnp.arange(ROPE_SPLIT, dtype=jnp.int32))
    sign = jnp.where((jnp.arange(LANES) % HEAD_DIM) < HEAD_DIM // 2, -1.0, 1.0)[None, :]
    return jnp.stack([cos_hi, sin_hi]), jnp.stack([cos_lo, sin_lo, sign * cos_lo, sign * sin_lo])


def kernel(x, w_in, attn_out_g, gmlp_out_g, gmlp_ln_g, gmlp_ln_b, w_spatial, b_spatial, w_out,
           ln1_g, ln1_b, w_gate, w_up, w_down, ln2_g, ln2_b):
    batch, seq, d_model = x.shape
    depth = w_in.shape[0]
    width = attn_out_g.shape[-1]
    alpha = (2 * depth) ** 0.25
    assert w_in.shape[-1] == 5 * width and seq % MOBA_BLOCK == 0 and width % LANES == 0
    assert min(MOBA_TOPK, seq // MOBA_BLOCK - 1) == MOBA_TOPK
    rope_hi, rope_lo = _rope_tables(seq)
    x2 = x.reshape(batch * seq, d_model)
    for l in range(depth):
        (q, keys, v, u, z, k_mean), (w_out_b, w_gate_b, w_up_b, w_down_b) = _in_proj(
            x2, w_in[l], rope_hi, rope_lo,
            gmlp_ln_g[l].reshape(1, width), gmlp_ln_b[l].reshape(1, width),
            (w_out[l], w_gate[l], w_up[l], w_down[l]), seq=seq, width=width, tm=IN_PROJ_TILE, n_sub=IN_PROJ_SUBTILES)
        attn = _attention(q, keys, v, k_mean, batch=batch, seq=seq)
        bias_full = jnp.repeat(b_spatial[l].T, HEAD_DIM, axis=1)
        x2 = _post_attention(attn, u, z, x2, w_spatial[l], bias_full,
                             attn_out_g[l].reshape(1, width), gmlp_out_g[l].reshape(1, width),
                             w_out_b, ln1_g[l].reshape(1, d_model), ln1_b[l].reshape(1, d_model),
                             w_gate_b, w_up_b, w_down_b,
                             ln2_g[l].reshape(1, d_model), ln2_b[l].reshape(1, d_model),
                             alpha=alpha, tm=POST_TILE, hc=FFN_CHUNK, n_sub=POST_SUBTILES)
    return x2.reshape(batch, seq, d_model)
```

```python
import functools

import jax
import jax.numpy as jnp
from jax import lax
from jax.experimental import pallas as pl
from jax.experimental.pallas import tpu as pltpu

HEAD_DIM = 64
MOBA_BLOCK = 256
MOBA_TOPK = 3
GMLP_CHUNK = 128
ROPE_THETA = 10000.0
ROPE_SPLIT = 64
LN_EPS = 1e-5
RMS_EPS = 1e-6
NEG_INF = -1e30
LANES = 128
ACC_ROWS = HEAD_DIM + 16
F32 = jnp.float32
BF16 = jnp.bfloat16

IN_PROJ_TILE, IN_PROJ_SUBTILES = 1024, 8
POST_TILE, POST_SUBTILES = 512, 2
FFN_CHUNK = 256
VMEM_LIMIT = 48 * 1024 * 1024
POST_VMEM_LIMIT = 56 * 1024 * 1024


def _erf_gelu(t):
    return 0.5 * t * (1.0 + lax.erf(t * 0.7071067811865476))


def _in_proj_kernel(x_ref, w32_ref, rope_hi_ref, rope_lo_ref, lng_ref, lnb_ref, *rest,
                    width, n_sub, n_cast, n_pos_tiles):
    cast_in, rest = rest[:n_cast], rest[n_cast:]
    (q_ref, ka_ref, v_ref, u_ref, z_ref, kmean_ref), rest = rest[:6], rest[6:]
    cast_out, (w_ref,) = rest[:n_cast], rest[n_cast:]
    first_block = lax.rem(pl.program_id(0), n_pos_tiles) * (x_ref.shape[0] // MOBA_BLOCK)

    @pl.when(pl.program_id(0) == 0)
    def _():
        w_ref[...] = w32_ref[...].astype(BF16)

    lane = lax.broadcasted_iota(jnp.int32, (1, LANES), 1)
    lo = lane < HEAD_DIM
    sm = x_ref.shape[0] // n_sub
    n_seg = w_ref.shape[1] // width
    row_slices = [slice(s * sm, (s + 1) * sm) for s in range(n_sub)]

    first_half = (lane % HEAD_DIM) < HEAD_DIM // 2
    tables = {}

    def rope_tables(rows):
        if rows.start not in tables:
            cos_b, sin_b, cos_b_signed, sin_b_signed = (rope_lo_ref[i] for i in range(4))
            cos_rows, sin_rows = [], []
            for a in range(rows.start // ROPE_SPLIT, rows.stop // ROPE_SPLIT):
                cos_a, sin_a = rope_hi_ref[0, a:a + 1, :], rope_hi_ref[1, a:a + 1, :]
                cos_rows.append(cos_a * cos_b - sin_a * sin_b)
                sin_rows.append(sin_a * cos_b_signed + cos_a * sin_b_signed)
            tables[rows.start] = (jnp.concatenate(cos_rows, axis=0), jnp.concatenate(sin_rows, axis=0))
        return tables[rows.start]

    def rope(t, rows):
        cos, sin_signed = rope_tables(rows)
        rotated = jnp.where(first_half, pltpu.roll(t, 96, 1), pltpu.roll(t, 32, 1))
        return t * cos + rotated * sin_signed

    def group_norm(zs, sl):
        s_lo = jnp.sum(jnp.where(lo, zs, 0.0), axis=1, keepdims=True)
        s_hi = jnp.sum(jnp.where(lo, 0.0, zs), axis=1, keepdims=True)
        d = zs - jnp.where(lo, s_lo, s_hi) * (1.0 / HEAD_DIM)
        dd = d * d
        v_lo = jnp.sum(jnp.where(lo, dd, 0.0), axis=1, keepdims=True)
        v_hi = jnp.sum(jnp.where(lo, 0.0, dd), axis=1, keepdims=True)
        var = jnp.where(lo, v_lo, v_hi) * (1.0 / HEAD_DIM)
        return d * lax.rsqrt(var + LN_EPS) * lng_ref[:, sl] + lnb_ref[:, sl]

    k_sums = {}

    def emit_keys(k_rot, rows, c):
        local = rows.start // MOBA_BLOCK
        kb = k_rot.astype(BF16)
        for hd in range(2):
            in_head = (lane >= HEAD_DIM * hd) & (lane < HEAD_DIM * (hd + 1))
            onehot = jnp.where(lane == HEAD_DIM * (1 - hd) + first_block + local, 1.0, 0.0)
            slab = slice((2 * c + hd) * LANES, (2 * c + hd + 1) * LANES)
            ka_ref[rows, slab] = jnp.where(in_head, kb, onehot.astype(BF16))
        part = jnp.sum(k_rot, axis=0, keepdims=True) * (1.0 / MOBA_BLOCK)
        key = (local, c)
        k_sums[key] = part if rows.start % MOBA_BLOCK == 0 else k_sums[key] + part
        if rows.stop % MOBA_BLOCK == 0:
            kmean_ref[local, :, c * LANES:(c + 1) * LANES] = jnp.broadcast_to(k_sums.pop(key), (8, LANES))

    def epilogue(seg, h, rows):
        if seg == 2:
            v_ref[rows, :] = h.astype(BF16)
        elif seg == 3:
            u_ref[rows, :] = _erf_gelu(h).astype(BF16)
        else:
            if seg == 4:
                h = _erf_gelu(h)
            for c in range(width // LANES):
                sl = slice(c * LANES, (c + 1) * LANES)
                if seg == 0:
                    q_ref[rows, sl] = rope(h[:, sl] * (HEAD_DIM ** -0.5), rows).astype(BF16)
                elif seg == 1:
                    emit_keys(rope(h[:, sl], rows), rows, c)
                else:
                    z_ref[rows, sl] = group_norm(h[:, sl], sl).astype(BF16)

    order = (4, 3, 0, 1, 2)
    xs = [None] * n_sub
    hs = [[None] * n_seg for _ in range(n_sub)]
    last = n_sub - 1
    for s in range(n_sub):
        xs[s] = x_ref[row_slices[s], :].astype(BF16)
        done = None
        for seg in order:
            hs[s][seg] = jnp.dot(xs[s], w_ref[:, seg * width:(seg + 1) * width],
                                 preferred_element_type=F32)
            if s >= 1:
                epilogue(seg, hs[s - 1][seg], row_slices[s - 1])
            if s == last and done is not None:
                epilogue(done, hs[s][done], row_slices[s])
            done = seg
        for src, dst in list(zip(cast_in, cast_out))[s::n_sub]:
            dst[...] = src[...].astype(BF16)
    epilogue(order[-1], hs[last][order[-1]], row_slices[last])


def _in_proj(x2, w_in, rope_hi, rope_lo, lng, lnb, later_weights, *, seq, width, tm, n_sub):
    tokens, d_model = x2.shape
    n_steps = tokens // tm
    n_pos_tiles = seq // tm
    assert (tm // n_sub) % ROPE_SPLIT == 0 and (tm // ROPE_SPLIT) % 8 == 0
    assert tm % MOBA_BLOCK == 0 and MOBA_BLOCK % (tm // n_sub) == 0
    out = jax.ShapeDtypeStruct((tokens, width), BF16)
    tile = pl.BlockSpec((tm, width), lambda t: (t, 0))
    keys_out = jax.ShapeDtypeStruct((tokens, 2 * width), BF16)
    keys_tile = pl.BlockSpec((tm, 2 * width), lambda t: (t, 0))
    mean_out = jax.ShapeDtypeStruct((tokens // MOBA_BLOCK, 8, width), F32)
    mean_tile = pl.BlockSpec((tm // MOBA_BLOCK, 8, width), lambda t: (t, 0, 0))
    hi_spec = pl.BlockSpec((2, tm // ROPE_SPLIT, LANES), lambda t: (0, t % n_pos_tiles, 0))
    lo_spec = pl.BlockSpec(rope_lo.shape, lambda t: (0, 0, 0))
    row = pl.BlockSpec((1, width), lambda t: (0, 0))
    slabs = []
    for w in later_weights:
        assert w.shape[0] % (16 * n_steps) == 0
        slabs.append(pl.BlockSpec((w.shape[0] // n_steps, w.shape[1]), lambda t: (t, 0)))
    res = pl.pallas_call(
        functools.partial(_in_proj_kernel, width=width, n_sub=n_sub, n_cast=len(later_weights),
                          n_pos_tiles=n_pos_tiles),
        grid=(n_steps,),
        in_specs=[pl.BlockSpec((tm, d_model), lambda t: (t, 0)),
                  pl.BlockSpec(w_in.shape, lambda t: (0, 0), pipeline_mode=pl.Buffered(1)),
                  hi_spec, lo_spec, row, row, *slabs],
        out_specs=[tile, keys_tile, tile, tile, tile, mean_tile] + slabs,
        out_shape=[out, keys_out, out, out, out, mean_out]
        + [jax.ShapeDtypeStruct(w.shape, BF16) for w in later_weights],
        scratch_shapes=[pltpu.VMEM(w_in.shape, BF16)],
        compiler_params=pltpu.CompilerParams(dimension_semantics=("arbitrary",),
                                             vmem_limit_bytes=VMEM_LIMIT),
        name="in_proj",
    )(x2, w_in, rope_hi, rope_lo, lng, lnb, *later_weights)
    return res[:6], res[6:]


def _aligned(start, multiple):
    return start if isinstance(start, int) else pl.multiple_of(start, multiple)


def _attn_kernel(q_ref, ka0_ref, ka1_ref, v_ref, kmean_ref, o_ref, vt_ref, km_ref, qa_ref,
                 s0_ref, s1_ref, m0_ref, m1_ref, acc0_ref, acc1_ref, *, nb):
    blk = MOBA_BLOCK
    sub = blk // 8
    s_refs, m_refs = (s0_ref, s1_ref), (m0_ref, m1_ref)
    acc_refs = (acc0_ref, acc1_ref)
    ka_refs = (ka0_ref, ka1_ref)
    assert nb % 4 == 0
    lane = lax.broadcasted_iota(jnp.int32, (1, LANES), 1)
    head_lanes = [(lane >= HEAD_DIM * h) & (lane < HEAD_DIM * (h + 1)) for h in range(2)]

    km_pair = jnp.max(kmean_ref[...], axis=1)
    for h in range(2):
        km = jnp.where(head_lanes[h], km_pair, 0.0)
        hi = km.astype(BF16)
        km_ref[(2 * h) * nb:(2 * h + 1) * nb, :] = hi
        km_ref[(2 * h + 1) * nb:(2 * h + 2) * nb, :] = (km - hi.astype(F32)).astype(BF16)

    def build_values(j):
        ones_row = jnp.where(lax.broadcasted_iota(jnp.int32, (ACC_ROWS - HEAD_DIM, blk), 0) == 0, 1.0, 0.0)
        v_t = v_ref[j * blk:(j + 1) * blk, :].astype(F32).T
        for h in range(2):
            vt_ref[j, h] = jnp.concatenate(
                [v_t[HEAD_DIM * h:HEAD_DIM * (h + 1)], ones_row], axis=0).astype(BF16)

    row = lax.broadcasted_iota(jnp.int32, (nb, blk), 0)
    rowf = row.astype(F32)
    kk = lax.broadcasted_iota(jnp.int32, (blk, blk), 0)
    qq = lax.broadcasted_iota(jnp.int32, (blk, blk), 1)

    def select_blocks(r):
        q_t = q_ref[r * blk:(r + 1) * blk, :].astype(F32).T.astype(BF16)
        few_past = r <= MOBA_TOPK
        if not few_past:
            g_all = jnp.dot(km_ref[...], q_t, preferred_element_type=F32)
        for h in range(2):
            past = row < r
            bias = jnp.where((row <= r) if few_past else (row == r), 0.0, NEG_INF)
            if not few_past:
                g = g_all[(2 * h) * nb:(2 * h + 1) * nb] + g_all[(2 * h + 1) * nb:(2 * h + 2) * nb]
                g = jnp.where(past, g, NEG_INF)
            for _ in range(0 if few_past else MOBA_TOPK):
                top = jnp.max(g, axis=0, keepdims=True)
                first = jnp.min(jnp.where(g == top, rowf, float(nb)), axis=0, keepdims=True)
                pick = rowf == first
                bias = jnp.where(pick & past, 0.0, bias)
                g = jnp.where(pick, -jnp.inf, g)
            q_h = q_t[HEAD_DIM * h:HEAD_DIM * (h + 1)]
            bias = bias.astype(BF16)
            pad = jnp.zeros((HEAD_DIM - nb, blk), BF16)
            pieces = [q_h, bias, pad] if h == 0 else [bias, pad, q_h]
            qa_ref[r, h] = jnp.concatenate(pieces, axis=0)

    def score_tile(slot, t, r, rs, j, causal):
        m_ref, s_ref = m_refs[slot], s_refs[slot]
        k0 = _aligned(j * blk, blk)
        for h in range(2):
            s = jnp.dot(ka_refs[h][pl.ds(k0, blk), :], qa_ref[r, h],
                        preferred_element_type=F32)
            if causal is not None:
                s = s + causal
            s_ref[t, h] = s
            m_ref[rs, h] = jnp.maximum(m_ref[rs, h], jnp.max(s.reshape(sub, 8, blk), axis=0))

    def value_tile(slot, t, rs, j):
        m_ref, s_ref, acc_ref = m_refs[slot], s_refs[slot], acc_refs[slot]
        for h in range(2):
            p = jnp.exp(s_ref[t, h].reshape(sub, 8, blk) - m_ref[rs, h]).reshape(blk, blk)
            acc_ref[rs, h] += jnp.dot(vt_ref[j, h], p.astype(BF16), preferred_element_type=F32)

    def past_tile(u, pi):
        if isinstance(pi, int):
            return (0, u) if u < pi else (1, u - pi)
        first = u < pi
        return jnp.where(first, 0, 1), jnp.where(first, u, u - pi)

    def tile_of(t, pi):
        rows = (pi, nb - 1 - pi)
        return (t, rows[t]) if t < 2 else past_tile(t - 2, pi)

    def write_rows(pair, slot, rs):
        outs = []
        for h in range(2):
            a = acc_refs[slot][rs, h]
            outs.append(a[:HEAD_DIM] / a[HEAD_DIM:HEAD_DIM + 1])
        q0 = _aligned((pair, nb - 1 - pair)[rs] * blk, blk)
        o_ref[pl.ds(q0, blk), :] = jnp.concatenate(outs, axis=0).T.astype(BF16)

    def step(k, slot, extra=()):
        live = lambda p: not isinstance(p, int) or 0 <= p < nb // 2
        score_pair, value_pair, write_pair = k, k - 1, k - 2
        do_score, do_value, do_write = (live(p) for p in (score_pair, value_pair, write_pair))
        if do_score:
            causal = jnp.where(kk <= qq, 0.0, NEG_INF)
            for rs in range(2):
                for h in range(2):
                    m_refs[slot][rs, h] = jnp.full((8, blk), NEG_INF, F32)
        if do_value:
            for rs in range(2):
                for h in range(2):
                    acc_refs[1 - slot][rs, h] = jnp.zeros((ACC_ROWS, blk), F32)
        write_at = {1: 0, nb // 2: 1}
        for t in range(nb + 1):
            if do_score:
                rs, j = tile_of(t, score_pair)
                r = score_pair + rs * (nb - 1 - 2 * score_pair)
                score_tile(slot, t, r, rs, j, causal if t < 2 else None)
            if do_value:
                rs, j = tile_of(t, value_pair)
                value_tile(1 - slot, t, rs, j)
            if do_write and t in write_at:
                write_rows(write_pair, slot, write_at[t])
            for thunk in extra[2 * t:2 * t + 2]:
                thunk()
        if do_score:
            m_ref = m_refs[slot]
            for rs in range(2):
                for h in range(2):
                    m_ref[rs, h] = jnp.broadcast_to(jnp.max(m_ref[rs, h], axis=0, keepdims=True),
                                                    (8, blk))

    def last_steps(k):
        slot, pi = k % 2, k
        causal = jnp.where(kk <= qq, 0.0, NEG_INF)
        for rs in range(2):
            for h in range(2):
                m_refs[slot][rs, h] = jnp.full((8, blk), NEG_INF, F32)
                acc_refs[1 - slot][rs, h] = jnp.zeros((ACC_ROWS, blk), F32)

        def finish_max(rs):
            for h in range(2):
                m_refs[slot][rs, h] = jnp.broadcast_to(
                    jnp.max(m_refs[slot][rs, h], axis=0, keepdims=True), (8, blk))

        tiles = [(t,) + tile_of(t, pi) for t in range(nb + 1)]
        first = [x for x in tiles if x[1] == 0]
        second = [x for x in tiles if x[1] == 1]
        assert len(first) <= len(second)
        write_at = {1: 0, nb // 2: 1}
        early = list(first)
        for i, (t, rs, j) in enumerate(first + second):
            score_tile(slot, t, pi + rs * (nb - 1 - 2 * pi), rs, j, causal if t < 2 else None)
            value_tile(1 - slot, i, *tile_of(i, pi - 1))
            if i in write_at:
                write_rows(pi - 2, slot, write_at[i])
            if i == len(first) - 1:
                finish_max(0)
                for h in range(2):
                    acc_refs[slot][0, h] = jnp.zeros((ACC_ROWS, blk), F32)
            elif i >= len(first) and early:
                t0, _, j0 = early.pop(0)
                value_tile(slot, t0, 0, j0)
        assert not early
        finish_max(1)
        for h in range(2):
            acc_refs[slot][1, h] = jnp.zeros((ACC_ROWS, blk), F32)
        for i, (t, _, j) in enumerate(second):
            value_tile(slot, t, 1, j)
            if i == 1:
                write_rows(pi - 1, 1 - slot, 0)
            if i == 3:
                write_rows(pi, slot, 0)
            if i == len(second) // 2 + 1:
                write_rows(pi - 1, 1 - slot, 1)
        write_rows(pi, slot, 1)

    n_pairs = nb // 2
    select_blocks(0)
    select_blocks(nb - 1)
    prep = []
    for j in range(nb):
        prep.append(functools.partial(build_values, j))
        if 1 <= j < nb - 1:
            prep.append(functools.partial(select_blocks, j))
    assert len(prep) <= 2 * (nb + 1)
    step(0, 0, prep)
    step(1, 1)

    def two_steps(i2, _):
        step(2 * i2 + 2, 0)
        step(2 * i2 + 3, 1)
        return 0

    n_rolled = (n_pairs - 3) // 2
    lax.fori_loop(0, n_rolled, two_steps, 0)
    for k in range(2 + 2 * n_rolled, n_pairs - 1):
        step(k, k % 2)
    last_steps(n_pairs - 1)


def _attention(q, keys, v, k_mean, *, batch, seq):
    width = q.shape[-1]
    nb = seq // MOBA_BLOCK
    q3, v3 = (t.reshape(batch, seq, width) for t in (q, v))
    keys3 = keys.reshape(batch, seq, 2 * width)
    spec = pl.BlockSpec((None, seq, LANES), lambda b, p: (b, 0, p))
    head_specs = [pl.BlockSpec((None, seq, LANES), lambda b, p, h=h: (b, 0, 2 * p + h)) for h in range(2)]
    mean_spec = pl.BlockSpec((nb, 8, LANES), lambda b, p: (b, 0, p))
    out = pl.pallas_call(
        functools.partial(_attn_kernel, nb=nb),
        grid=(batch, width // LANES),
        in_specs=[spec, *head_specs, spec, mean_spec],
        out_specs=spec,
        out_shape=jax.ShapeDtypeStruct((batch, seq, width), BF16),
        scratch_shapes=[pltpu.VMEM((nb, 2, ACC_ROWS, MOBA_BLOCK), BF16),
                        pltpu.VMEM((4 * nb, LANES), BF16),
                        pltpu.VMEM((nb, 2, LANES, MOBA_BLOCK), BF16),
                        *[pltpu.VMEM((nb + 1, 2, MOBA_BLOCK, MOBA_BLOCK), F32)] * 2,
                        *[pltpu.VMEM((2, 2, 8, MOBA_BLOCK), F32)] * 2,
                        *[pltpu.VMEM((2, 2, ACC_ROWS, MOBA_BLOCK), F32)] * 2],
        compiler_params=pltpu.CompilerParams(dimension_semantics=("arbitrary", "arbitrary"),
                                             vmem_limit_bytes=VMEM_LIMIT),
        name="moba_attention",
    )(q3, keys3, keys3, v3, k_mean)
    return out.reshape(batch * seq, width)


def _layer_norm(y, g, b):
    mu = jnp.mean(y, axis=-1, keepdims=True)
    d = y - mu
    var = jnp.mean(d * d, axis=-1, keepdims=True)
    return d * lax.rsqrt(var + LN_EPS) * g + b


def _rms_norm(t, g):
    ms = jnp.mean(t * t, axis=-1, keepdims=True)
    return t * lax.rsqrt(ms + RMS_EPS) * g


def _mix_pieces(attn_ref, u_ref, z_ref, x_ref, ws_ref, bs_ref, ag_ref, gg_ref, w_ref, lg_ref, lb_ref,
                sg_ref, store, *, alpha, width, n_sub):
    c = GMLP_CHUNK
    sm = x_ref.shape[0] // n_sub
    n_pairs = ws_ref.shape[0] // 2
    ii = lax.broadcasted_iota(jnp.int32, (c, c), 0)
    jj = lax.broadcasted_iota(jnp.int32, (c, c), 1)
    lane = lax.broadcasted_iota(jnp.int32, (1, LANES), 1)
    lo = lane < HEAD_DIM
    w_cat = [jnp.concatenate([jnp.where(jj <= ii, ws_ref[2 * p + g], 0.0) for g in range(2)],
                             axis=1).astype(BF16) for p in range(n_pairs)]

    def z_stack(rows, sl):
        z = z_ref[rows, sl]
        zero = jnp.zeros_like(z)
        return jnp.concatenate([jnp.where(lo, z, zero), jnp.where(lo, zero, z)], axis=0)

    def gate_and_norm(s):
        for c2 in range(sm // (2 * c)):
            r0 = s * sm + c2 * 2 * c
            rows = [slice(r0, r0 + c), slice(r0 + c, r0 + 2 * c)]
            for p in range(n_pairs):
                sl = slice(p * LANES, (p + 1) * LANES)
                rhs = jnp.concatenate([z_stack(rows[0], sl), z_stack(rows[1], sl)], axis=1)
                mixed = jnp.dot(w_cat[p], rhs, preferred_element_type=F32)
                for k in range(2):
                    sg_ref[rows[k], sl] = (u_ref[rows[k], sl].astype(F32)
                                           * (mixed[:, k * LANES:(k + 1) * LANES] + bs_ref[:, sl]))
        rows = slice(s * sm, (s + 1) * sm)
        an = _rms_norm(attn_ref[rows, :].astype(F32), ag_ref[...]).astype(BF16)
        sn = _rms_norm(sg_ref[rows, :], gg_ref[...]).astype(BF16)
        return an, sn

    def project(normed, cols):
        an, sn = normed
        return (jnp.dot(an, w_ref[:width, cols], preferred_element_type=F32)
                + jnp.dot(sn, w_ref[width:, cols], preferred_element_type=F32))

    def residual_norm(s, mix):
        rows = slice(s * sm, (s + 1) * sm)
        store(rows, _layer_norm(alpha * x_ref[rows, :] + mix, lg_ref[...], lb_ref[...]))

    d_model = w_ref.shape[1]
    col_halves = [slice(0, d_model // 2), slice(d_model // 2, d_model)]
    normed, lefts, mixes = {}, {}, {}

    def do_norm(s):
        normed[s] = gate_and_norm(s)

    def do_left(s):
        lefts[s] = project(normed[s], col_halves[0])

    def do_right(s):
        mixes[s] = jnp.concatenate([lefts.pop(s), project(normed.pop(s), col_halves[1])], axis=1)

    def do_out(s):
        residual_norm(s, mixes.pop(s))

    pieces = []
    for s in range(n_sub + 2):
        if s < n_sub:
            pieces.append(functools.partial(do_norm, s))
        if 1 <= s <= n_sub:
            pieces.append(functools.partial(do_left, s - 1))
        if 2 <= s:
            pieces.append(functools.partial(do_out, s - 2))
        if 1 <= s <= n_sub:
            pieces.append(functools.partial(do_right, s - 1))
    return pieces


def _post_kernel(attn_ref, u_ref, z_ref, x_ref, attn0_ref, u0_ref, z0_ref, x0_ref, ws_ref, bs_ref,
                 ag_ref, gg_ref, wo_ref, l1g_ref, l1b_ref, wg_ref, wu_ref, wd_ref, l2g_ref, l2b_ref,
                 o_ref, x1_ref, sg_ref, acc_ref, *, alpha, width, hc, n_sub):
    t = pl.program_id(0)
    slot = t % 2
    mix_refs = (ws_ref, bs_ref, ag_ref, gg_ref, wo_ref, l1g_ref, l1b_ref, sg_ref)
    mix_kw = dict(alpha=alpha, width=width, n_sub=n_sub)

    @pl.when(t == 0)
    def _():
        def store_first(rows, value):
            x1_ref[0, rows, :] = value
        for piece in _mix_pieces(attn0_ref, u0_ref, z0_ref, x0_ref, *mix_refs, store_first, **mix_kw):
            piece()

    def store_next(rows, value):
        x1_ref[1 - slot, rows, :] = value

    pieces = _mix_pieces(attn_ref, u_ref, z_ref, x_ref, *mix_refs, store_next, **mix_kw)
    hidden = wg_ref.shape[1]
    n_chunks = hidden // hc
    assert hidden % hc == 0 and len(pieces) <= n_chunks
    xb = x1_ref[slot].astype(BF16)
    for c in range(n_chunks):
        sl = slice(c * hc, (c + 1) * hc)
        g = jnp.dot(xb, wg_ref[:, sl], preferred_element_type=F32)
        u = jnp.dot(xb, wu_ref[:, sl], preferred_element_type=F32)
        h = (g / (1.0 + jnp.exp(-g)) * u).astype(BF16)
        part = jnp.dot(h, wd_ref[sl, :], preferred_element_type=F32)
        if c == 0:
            acc_ref[...] = part
        else:
            acc_ref[...] += part
        if c < len(pieces):
            pieces[c]()
    o_ref[...] = _layer_norm(alpha * x1_ref[slot] + acc_ref[...], l2g_ref[...], l2b_ref[...])


def _post_attention(attn, u, z, x2, w_spatial, bias_full, ag, gg, w_out, l1g, l1b,
                    w_gate, w_up, w_down, l2g, l2b, *, alpha, tm, hc, n_sub):
    tokens, width = attn.shape
    d_model = x2.shape[1]
    n_tiles = tokens // tm
    assert (tm // n_sub) % (2 * GMLP_CHUNK) == 0
    nxt = lambda t: (jnp.minimum(t + 1, n_tiles - 1), 0)
    once = dict(pipeline_mode=pl.Buffered(1))
    half_n = pl.BlockSpec((tm, width), nxt)
    full_n = pl.BlockSpec((tm, d_model), nxt)
    half_0 = pl.BlockSpec((tm, width), lambda t: (0, 0), **once)
    full_0 = pl.BlockSpec((tm, d_model), lambda t: (0, 0), **once)
    const = lambda a: pl.BlockSpec(a.shape, lambda t: (0,) * a.ndim, **once)
    return pl.pallas_call(
        functools.partial(_post_kernel, alpha=alpha, width=width, hc=hc, n_sub=n_sub),
        grid=(n_tiles,),
        in_specs=[half_n, half_n, half_n, full_n, half_0, half_0, half_0, full_0,
                  const(w_spatial), const(bias_full), const(ag), const(gg), const(w_out),
                  const(l1g), const(l1b), const(w_gate), const(w_up), const(w_down),
                  const(l2g), const(l2b)],
        out_specs=pl.BlockSpec((tm, d_model), lambda t: (t, 0)),
        out_shape=jax.ShapeDtypeStruct((tokens, d_model), F32),
        scratch_shapes=[pltpu.VMEM((2, tm, d_model), F32),
                        pltpu.VMEM((tm, width), F32),
                        pltpu.VMEM((tm, d_model), F32)],
        compiler_params=pltpu.CompilerParams(dimension_semantics=("arbitrary",),
                                             vmem_limit_bytes=POST_VMEM_LIMIT),
        name="mixer_out_ffn",
    )(attn, u, z, x2, attn, u, z, x2, w_spatial, bias_full, ag, gg, w_out, l1g, l1b,
      w_gate, w_up, w_down, l2g, l2b)


def _rope_tables(seq):
    assert seq % ROPE_SPLIT == 0
    inv_freq = ROPE_THETA ** (-jnp.arange(0, HEAD_DIM, 2, dtype=F32) / HEAD_DIM)
    tile = lambda t: jnp.concatenate([t, t, t, t], axis=-1)

    def cos_sin(pos):
        ang = pos.astype(F32)[:, None] * inv_freq[None, :]
        return tile(jnp.cos(ang)), tile(jnp.sin(ang))

    cos_hi, sin_hi = cos_sin(jnp.arange(0, seq, ROPE_SPLIT, dtype=jnp.int32))
    cos_lo, sin_lo = cos_sin(jnp.arange(ROPE_SPLIT, dtype=jnp.int32))
    sign = jnp.where((jnp.arange(LANES) % HEAD_DIM) < HEAD_DIM // 2, -1.0, 1.0)[None, :]
    return jnp.stack([cos_hi, sin_hi]), jnp.stack([cos_lo, sin_lo, sign * cos_lo, sign * sin_lo])


def kernel(x, w_in, attn_out_g, gmlp_out_g, gmlp_ln_g, gmlp_ln_b, w_spatial, b_spatial, w_out,
           ln1_g, ln1_b, w_gate, w_up, w_down, ln2_g, ln2_b):
    batch, seq, d_model = x.shape
    depth = w_in.shape[0]
    width = attn_out_g.shape[-1]
    alpha = (2 * depth) ** 0.25
    assert w_in.shape[-1] == 5 * width and seq % MOBA_BLOCK == 0 and width % LANES == 0
    assert min(MOBA_TOPK, seq // MOBA_BLOCK - 1) == MOBA_TOPK
    rope_hi, rope_lo = _rope_tables(seq)
    x2 = x.reshape(batch * seq, d_model)
    for l in range(depth):
        (q, keys, v, u, z, k_mean), (w_out_b, w_gate_b, w_up_b, w_down_b) = _in_proj(
            x2, w_in[l], rope_hi, rope_lo,
            gmlp_ln_g[l].reshape(1, width), gmlp_ln_b[l].reshape(1, width),
            (w_out[l], w_gate[l], w_up[l], w_down[l]), seq=seq, width=width, tm=IN_PROJ_TILE, n_sub=IN_PROJ_SUBTILES)
        attn = _attention(q, keys, v, k_mean, batch=batch, seq=seq)
        bias_full = jnp.repeat(b_spatial[l].T, HEAD_DIM, axis=1)
        x2 = _post_attention(attn, u, z, x2, w_spatial[l], bias_full,
                             attn_out_g[l].reshape(1, width), gmlp_out_g[l].reshape(1, width),
                             w_out_b, ln1_g[l].reshape(1, d_model), ln1_b[l].reshape(1, d_model),
                             w_gate_b, w_up_b, w_down_b,
                             ln2_g[l].reshape(1, d_model), ln2_b[l].reshape(1, d_model),
                             alpha=alpha, tm=POST_TILE, hc=FFN_CHUNK, n_sub=POST_SUBTILES)
    return x2.reshape(batch, seq, d_model)
```

```python
import functools

import jax
import jax.numpy as jnp
from jax import lax
from jax.experimental import pallas as pl
from jax.experimental.pallas import tpu as pltpu

HEAD_DIM = 64
MOBA_BLOCK = 256
MOBA_TOPK = 3
GMLP_CHUNK = 128
ROPE_THETA = 10000.0
ROPE_SPLIT = 64
LN_EPS = 1e-5
RMS_EPS = 1e-6
NEG_INF = -1e30
LANES = 128
ACC_ROWS = HEAD_DIM + 16
F32 = jnp.float32
BF16 = jnp.bfloat16

IN_PROJ_TILE, IN_PROJ_SUBTILES = 1024, 8
POST_TILE, POST_SUBTILES = 512, 2
FFN_CHUNK = 256
VMEM_LIMIT = 48 * 1024 * 1024
POST_VMEM_LIMIT = 56 * 1024 * 1024


def _erf_gelu(t):
    return 0.5 * t * (1.0 + lax.erf(t * 0.7071067811865476))


def _in_proj_kernel(x_ref, w32_ref, rope_hi_ref, rope_lo_ref, lng_ref, lnb_ref, *rest,
                    width, n_sub, n_cast, n_pos_tiles):
    cast_in, rest = rest[:n_cast], rest[n_cast:]
    (q_ref, ka_ref, v_ref, u_ref, z_ref, kmean_ref), rest = rest[:6], rest[6:]
    cast_out, (w_ref,) = rest[:n_cast], rest[n_cast:]
    first_block = lax.rem(pl.program_id(0), n_pos_tiles) * (x_ref.shape[0] // MOBA_BLOCK)

    @pl.when(pl.program_id(0) == 0)
    def _():
        w_ref[...] = w32_ref[...].astype(BF16)

    lane = lax.broadcasted_iota(jnp.int32, (1, LANES), 1)
    lo = lane < HEAD_DIM
    sm = x_ref.shape[0] // n_sub
    n_seg = w_ref.shape[1] // width
    row_slices = [slice(s * sm, (s + 1) * sm) for s in range(n_sub)]

    first_half = (lane % HEAD_DIM) < HEAD_DIM // 2
    tables = {}

    def rope_tables(rows):
        if rows.start not in tables:
            cos_b, sin_b, cos_b_signed, sin_b_signed = (rope_lo_ref[i] for i in range(4))
            cos_rows, sin_rows = [], []
            for a in range(rows.start // ROPE_SPLIT, rows.stop // ROPE_SPLIT):
                cos_a, sin_a = rope_hi_ref[0, a:a + 1, :], rope_hi_ref[1, a:a + 1, :]
                cos_rows.append(cos_a * cos_b - sin_a * sin_b)
                sin_rows.append(sin_a * cos_b_signed + cos_a * sin_b_signed)
            tables[rows.start] = (jnp.concatenate(cos_rows, axis=0), jnp.concatenate(sin_rows, axis=0))
        return tables[rows.start]

    def rope(t, rows):
        cos, sin_signed = rope_tables(rows)
        rotated = jnp.where(first_half, pltpu.roll(t, 96, 1), pltpu.roll(t, 32, 1))
        return t * cos + rotated * sin_signed

    def group_norm(zs, sl):
        s_lo = jnp.sum(jnp.where(lo, zs, 0.0), axis=1, keepdims=True)
        s_hi = jnp.sum(jnp.where(lo, 0.0, zs), axis=1, keepdims=True)
        d = zs - jnp.where(lo, s_lo, s_hi) * (1.0 / HEAD_DIM)
        dd = d * d
        v_lo = jnp.sum(jnp.where(lo, dd, 0.0), axis=1, keepdims=True)
        v_hi = jnp.sum(jnp.where(lo, 0.0, dd), axis=1, keepdims=True)
        var = jnp.where(lo, v_lo, v_hi) * (1.0 / HEAD_DIM)
        return d * lax.rsqrt(var + LN_EPS) * lng_ref[:, sl] + lnb_ref[:, sl]

    k_sums = {}

    def emit_keys(k_rot, rows, c):
        local = rows.start // MOBA_BLOCK
        kb = k_rot.astype(BF16)
        for hd in range(2):
            in_head = (lane >= HEAD_DIM * hd) & (lane < HEAD_DIM * (hd + 1))
            onehot = jnp.where(lane == HEAD_DIM * (1 - hd) + first_block + local, 1.0, 0.0)
            slab = slice((2 * c + hd) * LANES, (2 * c + hd + 1) * LANES)
            ka_ref[rows, slab] = jnp.where(in_head, kb, onehot.astype(BF16))
        part = jnp.sum(k_rot, axis=0, keepdims=True) * (1.0 / MOBA_BLOCK)
        key = (local, c)
        k_sums[key] = part if rows.start % MOBA_BLOCK == 0 else k_sums[key] + part
        if rows.stop % MOBA_BLOCK == 0:
            kmean_ref[local, :, c * LANES:(c + 1) * LANES] = jnp.broadcast_to(k_sums.pop(key), (8, LANES))

    def epilogue(seg, h, rows):
        if seg == 2:
            v_ref[rows, :] = h.astype(BF16)
        elif seg == 3:
            u_ref[rows, :] = _erf_gelu(h).astype(BF16)
        else:
            if seg == 4:
                h = _erf_gelu(h)
            for c in range(width // LANES):
                sl = slice(c * LANES, (c + 1) * LANES)
                if seg == 0:
                    q_ref[rows, sl] = rope(h[:, sl] * (HEAD_DIM ** -0.5), rows).astype(BF16)
                elif seg == 1:
                    emit_keys(rope(h[:, sl], rows), rows, c)
                else:
                    z_ref[rows, sl] = group_norm(h[:, sl], sl).astype(BF16)

    order = (4, 3, 0, 1, 2)
    xs = [None] * n_sub
    hs = [[None] * n_seg for _ in range(n_sub)]
    last = n_sub - 1
    for s in range(n_sub):
        xs[s] = x_ref[row_slices[s], :].astype(BF16)
        done = None
        for seg in order:
            hs[s][seg] = jnp.dot(xs[s], w_ref[:, seg * width:(seg + 1) * width],
                                 preferred_element_type=F32)
            if s >= 1:
                epilogue(seg, hs[s - 1][seg], row_slices[s - 1])
            if s == last and done is not None:
                epilogue(done, hs[s][done], row_slices[s])
            done = seg
        for src, dst in list(zip(cast_in, cast_out))[s::n_sub]:
            dst[...] = src[...].astype(BF16)
    epilogue(order[-1], hs[last][order[-1]], row_slices[last])


def _in_proj(x2, w_in, rope_hi, rope_lo, lng, lnb, later_weights, *, seq, width, tm, n_sub):
    tokens, d_model = x2.shape
    n_steps = tokens // tm
    n_pos_tiles = seq // tm
    assert (tm // n_sub) % ROPE_SPLIT == 0 and (tm // ROPE_SPLIT) % 8 == 0
    assert tm % MOBA_BLOCK == 0 and MOBA_BLOCK % (tm // n_sub) == 0
    out = jax.ShapeDtypeStruct((tokens, width), BF16)
    tile = pl.BlockSpec((tm, width), lambda t: (t, 0))
    keys_out = jax.ShapeDtypeStruct((tokens, 2 * width), BF16)
    keys_tile = pl.BlockSpec((tm, 2 * width), lambda t: (t, 0))
    mean_out = jax.ShapeDtypeStruct((tokens // MOBA_BLOCK, 8, width), F32)
    mean_tile = pl.BlockSpec((tm // MOBA_BLOCK, 8, width), lambda t: (t, 0, 0))
    hi_spec = pl.BlockSpec((2, tm // ROPE_SPLIT, LANES), lambda t: (0, t % n_pos_tiles, 0))
    lo_spec = pl.BlockSpec(rope_lo.shape, lambda t: (0, 0, 0))
    row = pl.BlockSpec((1, width), lambda t: (0, 0))
    slabs = []
    for w in later_weights:
        assert w.shape[0] % (16 * n_steps) == 0
        slabs.append(pl.BlockSpec((w.shape[0] // n_steps, w.shape[1]), lambda t: (t, 0)))
    res = pl.pallas_call(
        functools.partial(_in_proj_kernel, width=width, n_sub=n_sub, n_cast=len(later_weights),
                          n_pos_tiles=n_pos_tiles),
        grid=(n_steps,),
        in_specs=[pl.BlockSpec((tm, d_model), lambda t: (t, 0)),
                  pl.BlockSpec(w_in.shape, lambda t: (0, 0), pipeline_mode=pl.Buffered(1)),
                  hi_spec, lo_spec, row, row, *slabs],
        out_specs=[tile, keys_tile, tile, tile, tile, mean_tile] + slabs,
        out_shape=[out, keys_out, out, out, out, mean_out]
        + [jax.ShapeDtypeStruct(w.shape, BF16) for w in later_weights],
        scratch_shapes=[pltpu.VMEM(w_in.shape, BF16)],
        compiler_params=pltpu.CompilerParams(dimension_semantics=("arbitrary",),
                                             vmem_limit_bytes=VMEM_LIMIT),
        name="in_proj",
    )(x2, w_in, rope_hi, rope_lo, lng, lnb, *later_weights)
    return res[:6], res[6:]


def _aligned(start, multiple):
    return start if isinstance(start, int) else pl.multiple_of(start, multiple)


def _attn_kernel(q_ref, ka0_ref, ka1_ref, v_ref, kmean_ref, o_ref, vt_ref, km_ref, qa_ref,
                 s0_ref, s1_ref, m0_ref, m1_ref, acc0_ref, acc1_ref, *, nb):
    blk = MOBA_BLOCK
    sub = blk // 8
    s_refs, m_refs = (s0_ref, s1_ref), (m0_ref, m1_ref)
    acc_refs = (acc0_ref, acc1_ref)
    ka_refs = (ka0_ref, ka1_ref)
    assert nb % 4 == 0
    lane = lax.broadcasted_iota(jnp.int32, (1, LANES), 1)
    head_lanes = [(lane >= HEAD_DIM * h) & (lane < HEAD_DIM * (h + 1)) for h in range(2)]

    km_pair = jnp.max(kmean_ref[...], axis=1)
    for h in range(2):
        km = jnp.where(head_lanes[h], km_pair, 0.0)
        hi = km.astype(BF16)
        km_ref[(2 * h) * nb:(2 * h + 1) * nb, :] = hi
        km_ref[(2 * h + 1) * nb:(2 * h + 2) * nb, :] = (km - hi.astype(F32)).astype(BF16)

    def build_values(j):
        ones_row = jnp.where(lax.broadcasted_iota(jnp.int32, (ACC_ROWS - HEAD_DIM, blk), 0) == 0, 1.0, 0.0)
        v_t = v_ref[j * blk:(j + 1) * blk, :].astype(F32).T
        for h in range(2):
            vt_ref[j, h] = jnp.concatenate(
                [v_t[HEAD_DIM * h:HEAD_DIM * (h + 1)], ones_row], axis=0).astype(BF16)

    row = lax.broadcasted_iota(jnp.int32, (nb, blk), 0)
    rowf = row.astype(F32)
    kk = lax.broadcasted_iota(jnp.int32, (blk, blk), 0)
    qq = lax.broadcasted_iota(jnp.int32, (blk, blk), 1)

    def select_blocks(r):
        q_t = q_ref[r * blk:(r + 1) * blk, :].astype(F32).T.astype(BF16)
        few_past = r <= MOBA_TOPK
        if not few_past:
            g_all = jnp.dot(km_ref[...], q_t, preferred_element_type=F32)
        for h in range(2):
            past = row < r
            bias = jnp.where((row <= r) if few_past else (row == r), 0.0, NEG_INF)
            if not few_past:
                g = g_all[(2 * h) * nb:(2 * h + 1) * nb] + g_all[(2 * h + 1) * nb:(2 * h + 2) * nb]
                g = jnp.where(past, g, NEG_INF)
            for _ in range(0 if few_past else MOBA_TOPK):
                top = jnp.max(g, axis=0, keepdims=True)
                first = jnp.min(jnp.where(g == top, rowf, float(nb)), axis=0, keepdims=True)
                pick = rowf == first
                bias = jnp.where(pick & past, 0.0, bias)
                g = jnp.where(pick, -jnp.inf, g)
            q_h = q_t[HEAD_DIM * h:HEAD_DIM * (h + 1)]
            bias = bias.astype(BF16)
            pad = jnp.zeros((HEAD_DIM - nb, blk), BF16)
            pieces = [q_h, bias, pad] if h == 0 else [bias, pad, q_h]
            qa_ref[r, h] = jnp.concatenate(pieces, axis=0)

    def score_tile(slot, t, r, rs, j, causal):
        m_ref, s_ref = m_refs[slot], s_refs[slot]
        k0 = _aligned(j * blk, blk)
        for h in range(2):
            s = jnp.dot(ka_refs[h][pl.ds(k0, blk), :], qa_ref[r, h],
                        preferred_element_type=F32)
            if causal is not None:
                s = s + causal
            s_ref[t, h] = s
            m_ref[rs, h] = jnp.maximum(m_ref[rs, h], jnp.max(s.reshape(sub, 8, blk), axis=0))

    def value_tile(slot, t, rs, j):
        m_ref, s_ref, acc_ref = m_refs[slot], s_refs[slot], acc_refs[slot]
        for h in range(2):
            p = jnp.exp(s_ref[t, h].reshape(sub, 8, blk) - m_ref[rs, h]).reshape(blk, blk)
            acc_ref[rs, h] += jnp.dot(vt_ref[j, h], p.astype(BF16), preferred_element_type=F32)

    def past_tile(u, pi):
        if isinstance(pi, int):
            return (0, u) if u < pi else (1, u - pi)
        first = u < pi
        return jnp.where(first, 0, 1), jnp.where(first, u, u - pi)

    def tile_of(t, pi):
        rows = (pi, nb - 1 - pi)
        return (t, rows[t]) if t < 2 else past_tile(t - 2, pi)

    def write_rows(pair, slot, rs):
        outs = []
        for h in range(2):
            a = acc_refs[slot][rs, h]
            outs.append(a[:HEAD_DIM] / a[HEAD_DIM:HEAD_DIM + 1])
        q0 = _aligned((pair, nb - 1 - pair)[rs] * blk, blk)
        o_ref[pl.ds(q0, blk), :] = jnp.concatenate(outs, axis=0).T.astype(BF16)

    def step(k, slot, extra=()):
        live = lambda p: not isinstance(p, int) or 0 <= p < nb // 2
        score_pair, value_pair, write_pair = k, k - 1, k - 2
        do_score, do_value, do_write = (live(p) for p in (score_pair, value_pair, write_pair))
        if do_score:
            causal = jnp.where(kk <= qq, 0.0, NEG_INF)
            for rs in range(2):
                for h in range(2):
                    m_refs[slot][rs, h] = jnp.full((8, blk), NEG_INF, F32)
        if do_value:
            for rs in range(2):
                for h in range(2):
                    acc_refs[1 - slot][rs, h] = jnp.zeros((ACC_ROWS, blk), F32)
        write_at = {1: 0, nb // 2: 1}
        for t in range(nb + 1):
            if do_score:
                rs, j = tile_of(t, score_pair)
                r = score_pair + rs * (nb - 1 - 2 * score_pair)
                score_tile(slot, t, r, rs, j, causal if t < 2 else None)
            if do_value:
                rs, j = tile_of(t, value_pair)
                value_tile(1 - slot, t, rs, j)
            if do_write and t in write_at:
                write_rows(write_pair, slot, write_at[t])
            for thunk in extra[2 * t:2 * t + 2]:
                thunk()
        if do_score:
            m_ref = m_refs[slot]
            for rs in range(2):
                for h in range(2):
                    m_ref[rs, h] = jnp.broadcast_to(jnp.max(m_ref[rs, h], axis=0, keepdims=True),
                                                    (8, blk))

    def last_steps(k):
        slot, pi = k % 2, k
        causal = jnp.where(kk <= qq, 0.0, NEG_INF)
        for rs in range(2):
            for h in range(2):
                m_refs[slot][rs, h] = jnp.full((8, blk), NEG_INF, F32)
                acc_refs[1 - slot][rs, h] = jnp.zeros((ACC_ROWS, blk), F32)

        def finish_max(rs):
            for h in range(2):
                m_refs[slot][rs, h] = jnp.broadcast_to(
                    jnp.max(m_refs[slot][rs, h], axis=0, keepdims=True), (8, blk))

        tiles = [(t,) + tile_of(t, pi) for t in range(nb + 1)]
        first = [x for x in tiles if x[1] == 0]
        second = [x for x in tiles if x[1] == 1]
        assert len(first) < len(second)
        write_at = {1: 0, nb // 2: 1}
        early = list(first)
        for i, (t, rs, j) in enumerate(first + second):
            score_tile(slot, t, pi + rs * (nb - 1 - 2 * pi), rs, j, causal if t < 2 else None)
            value_tile(1 - slot, i, *tile_of(i, pi - 1))
            if i in write_at:
                write_rows(pi - 2, slot, write_at[i])
            if i == len(first) - 1:
                finish_max(0)
                for h in range(2):
                    acc_refs[slot][0, h] = jnp.zeros((ACC_ROWS, blk), F32)
            elif i > len(first) and early:
                t0, _, j0 = early.pop(0)
                value_tile(slot, t0, 0, j0)
        assert not early
        finish_max(1)
        for h in range(2):
            acc_refs[slot][1, h] = jnp.zeros((ACC_ROWS, blk), F32)
        for i, (t, _, j) in enumerate(second):
            value_tile(slot, t, 1, j)
            if i == 1:
                write_rows(pi - 1, 1 - slot, 0)
            if i == 3:
                write_rows(pi, slot, 0)
            if i == len(second) // 2 + 1:
                write_rows(pi - 1, 1 - slot, 1)
        write_rows(pi, slot, 1)

    n_pairs = nb // 2
    select_blocks(0)
    select_blocks(nb - 1)
    prep = []
    for j in range(nb):
        prep.append(functools.partial(build_values, j))
        if 1 <= j < nb - 1:
            prep.append(functools.partial(select_blocks, j))
    assert len(prep) <= 2 * (nb + 1)
    step(0, 0, prep)
    step(1, 1)

    def two_steps(i2, _):
        step(2 * i2 + 2, 0)
        step(2 * i2 + 3, 1)
        return 0

    n_rolled = (n_pairs - 3) // 2
    lax.fori_loop(0, n_rolled, two_steps, 0)
    for k in range(2 + 2 * n_rolled, n_pairs - 1):
        step(k, k % 2)
    last_steps(n_pairs - 1)


def _attention(q, keys, v, k_mean, *, batch, seq):
    width = q.shape[-1]
    nb = seq // MOBA_BLOCK
    q3, v3 = (t.reshape(batch, seq, width) for t in (q, v))
    keys3 = keys.reshape(batch, seq, 2 * width)
    spec = pl.BlockSpec((None, seq, LANES), lambda b, p: (b, 0, p))
    head_specs = [pl.BlockSpec((None, seq, LANES), lambda b, p, h=h: (b, 0, 2 * p + h)) for h in range(2)]
    mean_spec = pl.BlockSpec((nb, 8, LANES), lambda b, p: (b, 0, p))
    out = pl.pallas_call(
        functools.partial(_attn_kernel, nb=nb),
        grid=(batch, width // LANES),
        in_specs=[spec, *head_specs, spec, mean_spec],
        out_specs=spec,
        out_shape=jax.ShapeDtypeStruct((batch, seq, width), BF16),
        scratch_shapes=[pltpu.VMEM((nb, 2, ACC_ROWS, MOBA_BLOCK), BF16),
                        pltpu.VMEM((4 * nb, LANES), BF16),
                        pltpu.VMEM((nb, 2, LANES, MOBA_BLOCK), BF16),
                        *[pltpu.VMEM((nb + 1, 2, MOBA_BLOCK, MOBA_BLOCK), F32)] * 2,
                        *[pltpu.VMEM((2, 2, 8, MOBA_BLOCK), F32)] * 2,
                        *[pltpu.VMEM((2, 2, ACC_ROWS, MOBA_BLOCK), F32)] * 2],
        compiler_params=pltpu.CompilerParams(dimension_semantics=("arbitrary", "arbitrary"),
                                             vmem_limit_bytes=VMEM_LIMIT),
        name="moba_attention",
    )(q3, keys3, keys3, v3, k_mean)
    return out.reshape(batch * seq, width)


def _layer_norm(y, g, b):
    mu = jnp.mean(y, axis=-1, keepdims=True)
    d = y - mu
    var = jnp.mean(d * d, axis=-1, keepdims=True)
    return d * lax.rsqrt(var + LN_EPS) * g + b


def _rms_norm(t, g):
    ms = jnp.mean(t * t, axis=-1, keepdims=True)
    return t * lax.rsqrt(ms + RMS_EPS) * g


def _mix_pieces(attn_ref, u_ref, z_ref, x_ref, ws_ref, bs_ref, ag_ref, gg_ref, w_ref, lg_ref, lb_ref,
                sg_ref, store, *, alpha, width, n_sub):
    c = GMLP_CHUNK
    sm = x_ref.shape[0] // n_sub
    n_pairs = ws_ref.shape[0] // 2
    ii = lax.broadcasted_iota(jnp.int32, (c, c), 0)
    jj = lax.broadcasted_iota(jnp.int32, (c, c), 1)
    lane = lax.broadcasted_iota(jnp.int32, (1, LANES), 1)
    lo = lane < HEAD_DIM
    w_cat = [jnp.concatenate([jnp.where(jj <= ii, ws_ref[2 * p + g], 0.0) for g in range(2)],
                             axis=1).astype(BF16) for p in range(n_pairs)]

    def z_stack(rows, sl):
        z = z_ref[rows, sl]
        zero = jnp.zeros_like(z)
        return jnp.concatenate([jnp.where(lo, z, zero), jnp.where(lo, zero, z)], axis=0)

    def gate_and_norm(s):
        for c2 in range(sm // (2 * c)):
            r0 = s * sm + c2 * 2 * c
            rows = [slice(r0, r0 + c), slice(r0 + c, r0 + 2 * c)]
            for p in range(n_pairs):
                sl = slice(p * LANES, (p + 1) * LANES)
                rhs = jnp.concatenate([z_stack(rows[0], sl), z_stack(rows[1], sl)], axis=1)
                mixed = jnp.dot(w_cat[p], rhs, preferred_element_type=F32)
                for k in range(2):
                    sg_ref[rows[k], sl] = (u_ref[rows[k], sl].astype(F32)
                                           * (mixed[:, k * LANES:(k + 1) * LANES] + bs_ref[:, sl]))
        rows = slice(s * sm, (s + 1) * sm)
        an = _rms_norm(attn_ref[rows, :].astype(F32), ag_ref[...]).astype(BF16)
        sn = _rms_norm(sg_ref[rows, :], gg_ref[...]).astype(BF16)
        return an, sn

    def project(normed, cols):
        an, sn = normed
        return (jnp.dot(an, w_ref[:width, cols], preferred_element_type=F32)
                + jnp.dot(sn, w_ref[width:, cols], preferred_element_type=F32))

    def residual_norm(s, mix):
        rows = slice(s * sm, (s + 1) * sm)
        store(rows, _layer_norm(alpha * x_ref[rows, :] + mix, lg_ref[...], lb_ref[...]))

    d_model = w_ref.shape[1]
    col_halves = [slice(0, d_model // 2), slice(d_model // 2, d_model)]
    normed, lefts, mixes = {}, {}, {}

    def do_norm(s):
        normed[s] = gate_and_norm(s)

    def do_left(s):
        lefts[s] = project(normed[s], col_halves[0])

    def do_right(s):
        mixes[s] = jnp.concatenate([lefts.pop(s), project(normed.pop(s), col_halves[1])], axis=1)

    def do_out(s):
        residual_norm(s, mixes.pop(s))

    pieces = []
    for s in range(n_sub + 2):
        if s < n_sub:
            pieces.append(functools.partial(do_norm, s))
        if 1 <= s <= n_sub:
            pieces.append(functools.partial(do_left, s - 1))
        if 2 <= s:
            pieces.append(functools.partial(do_out, s - 2))
        if 1 <= s <= n_sub:
            pieces.append(functools.partial(do_right, s - 1))
    return pieces


def _post_kernel(attn_ref, u_ref, z_ref, x_ref, attn0_ref, u0_ref, z0_ref, x0_ref, ws_ref, bs_ref,
                 ag_ref, gg_ref, wo_ref, l1g_ref, l1b_ref, wg_ref, wu_ref, wd_ref, l2g_ref, l2b_ref,
                 o_ref, x1_ref, sg_ref, acc_ref, *, alpha, width, hc, n_sub):
    t = pl.program_id(0)
    slot = t % 2
    mix_refs = (ws_ref, bs_ref, ag_ref, gg_ref, wo_ref, l1g_ref, l1b_ref, sg_ref)
    mix_kw = dict(alpha=alpha, width=width, n_sub=n_sub)

    @pl.when(t == 0)
    def _():
        def store_first(rows, value):
            x1_ref[0, rows, :] = value
        for piece in _mix_pieces(attn0_ref, u0_ref, z0_ref, x0_ref, *mix_refs, store_first, **mix_kw):
            piece()

    def store_next(rows, value):
        x1_ref[1 - slot, rows, :] = value

    pieces = _mix_pieces(attn_ref, u_ref, z_ref, x_ref, *mix_refs, store_next, **mix_kw)
    hidden = wg_ref.shape[1]
    n_chunks = hidden // hc
    assert hidden % hc == 0 and len(pieces) <= n_chunks
    xb = x1_ref[slot].astype(BF16)
    for c in range(n_chunks):
        sl = slice(c * hc, (c + 1) * hc)
        g = jnp.dot(xb, wg_ref[:, sl], preferred_element_type=F32)
        u = jnp.dot(xb, wu_ref[:, sl], preferred_element_type=F32)
        h = (g / (1.0 + jnp.exp(-g)) * u).astype(BF16)
        part = jnp.dot(h, wd_ref[sl, :], preferred_element_type=F32)
        if c == 0:
            acc_ref[...] = part
        else:
            acc_ref[...] += part
        if c < len(pieces):
            pieces[c]()
    o_ref[...] = _layer_norm(alpha * x1_ref[slot] + acc_ref[...], l2g_ref[...], l2b_ref[...])


def _post_attention(attn, u, z, x2, w_spatial, bias_full, ag, gg, w_out, l1g, l1b,
                    w_gate, w_up, w_down, l2g, l2b, *, alpha, tm, hc, n_sub):
    tokens, width = attn.shape
    d_model = x2.shape[1]
    n_tiles = tokens // tm
    assert (tm // n_sub) % (2 * GMLP_CHUNK) == 0
    nxt = lambda t: (jnp.minimum(t + 1, n_tiles - 1), 0)
    once = dict(pipeline_mode=pl.Buffered(1))
    half_n = pl.BlockSpec((tm, width), nxt)
    full_n = pl.BlockSpec((tm, d_model), nxt)
    half_0 = pl.BlockSpec((tm, width), lambda t: (0, 0), **once)
    full_0 = pl.BlockSpec((tm, d_model), lambda t: (0, 0), **once)
    const = lambda a: pl.BlockSpec(a.shape, lambda t: (0,) * a.ndim, **once)
    return pl.pallas_call(
        functools.partial(_post_kernel, alpha=alpha, width=width, hc=hc, n_sub=n_sub),
        grid=(n_tiles,),
        in_specs=[half_n, half_n, half_n, full_n, half_0, half_0, half_0, full_0,
                  const(w_spatial), const(bias_full), const(ag), const(gg), const(w_out),
                  const(l1g), const(l1b), const(w_gate), const(w_up), const(w_down),
                  const(l2g), const(l2b)],
        out_specs=pl.BlockSpec((tm, d_model), lambda t: (t, 0)),
        out_shape=jax.ShapeDtypeStruct((tokens, d_model), F32),
        scratch_shapes=[pltpu.VMEM((2, tm, d_model), F32),
                        pltpu.VMEM((tm, width), F32),
                        pltpu.VMEM((tm, d_model), F32)],
        compiler_params=pltpu.CompilerParams(dimension_semantics=("arbitrary",),
                                             vmem_limit_bytes=POST_VMEM_LIMIT),
        name="mixer_out_ffn",
    )(attn, u, z, x2, attn, u, z, x2, w_spatial, bias_full, ag, gg, w_out, l1g, l1b,
      w_gate, w_up, w_down, l2g, l2b)


def _rope_tables(seq):
    assert seq % ROPE_SPLIT == 0
    inv_freq = ROPE_THETA ** (-jnp.arange(0, HEAD_DIM, 2, dtype=F32) / HEAD_DIM)
    tile = lambda t: jnp.concatenate([t, t, t, t], axis=-1)

    def cos_sin(pos):
        ang = pos.astype(F32)[:, None] * inv_freq[None, :]
        return tile(jnp.cos(ang)), tile(jnp.sin(ang))

    cos_hi, sin_hi = cos_sin(jnp.arange(0, seq, ROPE_SPLIT, dtype=jnp.int32))
    cos_lo, sin_lo = cos_sin(jnp.arange(ROPE_SPLIT, dtype=jnp.int32))
    sign = jnp.where((jnp.arange(LANES) % HEAD_DIM) < HEAD_DIM // 2, -1.0, 1.0)[None, :]
    return jnp.stack([cos_hi, sin_hi]), jnp.stack([cos_lo, sin_lo, sign * cos_lo, sign * sin_lo])


def kernel(x, w_in, attn_out_g, gmlp_out_g, gmlp_ln_g, gmlp_ln_b, w_spatial, b_spatial, w_out,
           ln1_g, ln1_b, w_gate, w_up, w_down, ln2_g, ln2_b):
    batch, seq, d_model = x.shape
    depth = w_in.shape[0]
    width = attn_out_g.shape[-1]
    alpha = (2 * depth) ** 0.25
    assert w_in.shape[-1] == 5 * width and seq % MOBA_BLOCK == 0 and width % LANES == 0
    assert min(MOBA_TOPK, seq // MOBA_BLOCK - 1) == MOBA_TOPK
    rope_hi, rope_lo = _rope_tables(seq)
    x2 = x.reshape(batch * seq, d_model)
    for l in range(depth):
        (q, keys, v, u, z, k_mean), (w_out_b, w_gate_b, w_up_b, w_down_b) = _in_proj(
            x2, w_in[l], rope_hi, rope_lo,
            gmlp_ln_g[l].reshape(1, width), gmlp_ln_b[l].reshape(1, width),
            (w_out[l], w_gate[l], w_up[l], w_down[l]), seq=seq, width=width, tm=IN_PROJ_TILE, n_sub=IN_PROJ_SUBTILES)
        attn = _attention(q, keys, v, k_mean, batch=batch, seq=seq)
        bias_full = jnp.repeat(b_spatial[l].T, HEAD_DIM, axis=1)
        x2 = _post_attention(attn, u, z, x2, w_spatial[l], bias_full,
                             attn_out_g[l].reshape(1, width), gmlp_out_g[l].reshape(1, width),
                             w_out_b, ln1_g[l].reshape(1, d_model), ln1_b[l].reshape(1, d_model),
                             w_gate_b, w_up_b, w_down_b,
                             ln2_g[l].reshape(1, d_model), ln2_b[l].reshape(1, d_model),
                             alpha=alpha, tm=POST_TILE, hc=FFN_CHUNK, n_sub=POST_SUBTILES)
    return x2.reshape(batch, seq, d_model)
```
